```python
import jax, jax.numpy as jnp
from jax import lax
import numpy as np

D_MODEL = 1024
BATCH = 4
SEQ = 4096
DEPTH = 2

N_HEADS = 4
HEAD_DIM = 64
MIX_W = N_HEADS * HEAD_DIM
N_MIXERS = 4
MIX_WIDTH = N_MIXERS * MIX_W
ROPE_THETA = 500000.0
ROPE_DIM = HEAD_DIM // 4
Q_BLOCK = 128
EPS = 1e-6
NEG = -1e30
BIG = 1e9
DIL_CONFIGS = ((128, 1), (512, 4), (2048, 16))
CMP_LEN = 32
CMP_STRIDE = 16
CMP_HID = 256
SEL_LEN = 64
SEL_TOPN = 16
WIN_LEN = 512
MOBA_BLOCK = 256
MOBA_TOPK = 3

IN_SPLITS = (
    MIX_W, MIX_W, MIX_W, MIX_W,
    MIX_W, MIX_W, MIX_W, MIX_W,
    MIX_W, HEAD_DIM, HEAD_DIM, HEAD_DIM, HEAD_DIM, HEAD_DIM, HEAD_DIM,
    3 * N_HEADS, MIX_W,
    MIX_W, MIX_W, MIX_W, MIX_W,
)
N_IN = sum(IN_SPLITS)

kernel_name = 'hybrid_parallel_sparse_mixers'


def rms_norm(x, g):
    xf = x.astype(jnp.float32)
    y = xf * lax.rsqrt(jnp.mean(xf * xf, axis=-1, keepdims=True) + EPS)
    return (y * g.astype(jnp.float32)).astype(x.dtype)


def rope_tables(pos):
    inv_freq = 1.0 / (ROPE_THETA ** (np.arange(0, ROPE_DIM, 2, dtype=np.float32) / ROPE_DIM))
    ang = jnp.asarray(pos, jnp.float32)[:, None] * jnp.asarray(inv_freq, jnp.float32)[None, :]
    return jnp.cos(ang), jnp.sin(ang)


def apply_rope(t, cos, sin):
    half = ROPE_DIM // 2
    c = cos.astype(t.dtype)
    s = sin.astype(t.dtype)
    t1 = t[..., :half]
    t2 = t[..., half:ROPE_DIM]
    return jnp.concatenate([t1 * c - t2 * s, t2 * c + t1 * s, t[..., ROPE_DIM:]], axis=-1)


def to_heads(t):
    B, S, _ = t.shape
    return t.reshape(B, S, N_HEADS, HEAD_DIM).transpose(0, 2, 1, 3)


def from_heads(t):
    B, H, S, dh = t.shape
    return t.transpose(0, 2, 1, 3).reshape(B, S, H * dh)


def sweep_query_blocks(block_fn, n_blocks):
    out = lax.map(block_fn, jnp.arange(n_blocks))
    nb, B, H, qb, dh = out.shape
    return out.transpose(1, 2, 0, 3, 4).reshape(B, H, nb * qb, dh)


def query_block(q, i):
    s0 = i * Q_BLOCK
    return s0, s0 + jnp.arange(Q_BLOCK), lax.dynamic_slice_in_dim(q, s0, Q_BLOCK, axis=2)


def stick_breaking_attention(q, k, v):
    S = q.shape[2]
    scale = HEAD_DIM ** -0.5
    kpos = jnp.arange(S)

    def block(i):
        s0, qpos, qb = query_block(q, i)
        z = jnp.einsum('bhqd,bhkd->bhqk', qb, k).astype(jnp.float32) * scale
        past = kpos[None, :] < qpos[:, None]
        log_rest = jnp.where(past, jax.nn.log_sigmoid(-z), 0.0)
        between = lax.cumsum(log_rest, axis=3, reverse=True) - log_rest
        a = jnp.where(past, jnp.exp(jax.nn.log_sigmoid(z) + between), 0.0)
        return jnp.einsum('bhqk,bhkd->bhqd', a.astype(v.dtype), v)

    return sweep_query_blocks(block, S // Q_BLOCK)


def dilated_window_attention(q, k, v):
    B, H, S, dh = q.shape
    scale = HEAD_DIM ** -0.5

    def block(i):
        s0, qpos, qb = query_block(q, i)
        outs, lses = [], []
        for window, dil in DIL_CONFIGS:
            n_keys = window // dil + 1
            idx = qpos[:, None] - dil * jnp.arange(n_keys)[None, :]
            valid = idx >= 0
            flat = jnp.maximum(idx, 0).reshape(-1)
            kg = jnp.take(k, flat, axis=2).reshape(B, H, Q_BLOCK, n_keys, dh)
            vg = jnp.take(v, flat, axis=2).reshape(B, H, Q_BLOCK, n_keys, dh)
            s = jnp.einsum('bhqd,bhqnd->bhqn', qb, kg).astype(jnp.float32) * scale
            s = jnp.where(valid, s, NEG)
            lse = jax.nn.logsumexp(s, axis=-1, keepdims=True)
            p = jnp.exp(s - lse)
            outs.append(jnp.einsum('bhqn,bhqnd->bhqd', p.astype(v.dtype), vg))
            lses.append(lse)
        w = jax.nn.softmax(jnp.concatenate(lses, axis=-1), axis=-1)
        o = jnp.stack(outs, axis=-1)
        return jnp.einsum('bhqdc,bhqc->bhqd', o, w.astype(o.dtype))

    return sweep_query_blocks(block, S // Q_BLOCK)


def compress_tokens(t, pos_emb, w1, b1, w2):
    B, S, dh = t.shape
    n_cmp = (S - CMP_LEN) // CMP_STRIDE + 1
    idx = np.arange(n_cmp)[:, None] * CMP_STRIDE + np.arange(CMP_LEN)[None, :]
    blocks = t[:, idx] + pos_emb
    hid = jax.nn.gelu(blocks.reshape(B, n_cmp, CMP_LEN * dh) @ w1 + b1)
    return hid @ w2


def nsa_attention(q, kc, vc, ks, vs, kw, vw, gates):
    B, H, S, dh = q.shape
    scale = HEAD_DIM ** -0.5
    n_cmp = kc.shape[1]
    n_sel = S // SEL_LEN
    n_top = min(SEL_TOPN, n_sel)
    c_start = np.arange(n_cmp) * CMP_STRIDE
    c_end = jnp.asarray(c_start + CMP_LEN - 1)
    s_start = np.arange(n_sel) * SEL_LEN
    overlap = np.clip(np.minimum(c_start[:, None] + CMP_LEN, s_start[None, :] + SEL_LEN)
                      - np.maximum(c_start[:, None], s_start[None, :]), 0, None) / CMP_LEN
    overlap = jnp.asarray(overlap, jnp.float32)
    ks_blk = ks.reshape(B, n_sel, SEL_LEN, dh)
    vs_blk = vs.reshape(B, n_sel, SEL_LEN, dh)
    kw_pad = jnp.pad(kw, ((0, 0), (WIN_LEN, 0), (0, 0)))
    vw_pad = jnp.pad(vw, ((0, 0), (WIN_LEN, 0), (0, 0)))
    blk_ids = jnp.arange(n_sel)
    gather_blocks = jax.vmap(lambda tb, idx: tb[idx])
    n_sel_keys = n_top * SEL_LEN

    def block(i):
        s0, qpos, qb = query_block(q, i)
        sc = jnp.einsum('bhqd,bnd->bhqn', qb, kc).astype(jnp.float32) * scale
        c_valid = c_end[None, :] <= qpos[:, None]
        pc = jax.nn.softmax(jnp.where(c_valid, sc, NEG), axis=-1) * c_valid
        o_cmp = jnp.einsum('bhqn,bnd->bhqd', pc.astype(vc.dtype), vc)
        imp = jnp.einsum('bhqn,ns->bqs', pc, overlap)
        own = qpos // SEL_LEN
        s_valid = blk_ids[None, :] <= own[:, None]
        forced = (blk_ids[None, :] == 0) | (blk_ids[None, :] >= own[:, None] - 1)
        imp = jnp.where(s_valid, jnp.where(forced, BIG, imp), NEG)
        _, sel = lax.top_k(imp, n_top)
        kg = gather_blocks(ks_blk, sel).reshape(B, Q_BLOCK, n_sel_keys, dh)
        vg = gather_blocks(vs_blk, sel).reshape(B, Q_BLOCK, n_sel_keys, dh)
        key_pos = (sel[..., None] * SEL_LEN + jnp.arange(SEL_LEN)).reshape(B, Q_BLOCK, n_sel_keys)
        sel_valid = key_pos <= qpos[None, :, None]
        ss = jnp.einsum('bhqd,bqmd->bhqm', qb, kg).astype(jnp.float32) * scale
        ps = jax.nn.softmax(jnp.where(sel_valid[:, None], ss, NEG), axis=-1)
        o_sel = jnp.einsum('bhqm,bqmd->bhqd', ps.astype(vg.dtype), vg)
        kwb = lax.dynamic_slice_in_dim(kw_pad, s0, Q_BLOCK + WIN_LEN, axis=1)
        vwb = lax.dynamic_slice_in_dim(vw_pad, s0, Q_BLOCK + WIN_LEN, axis=1)
        kpos = s0 - WIN_LEN + jnp.arange(Q_BLOCK + WIN_LEN)
        dist = qpos[:, None] - kpos[None, :]
        w_valid = (dist >= 0) & (dist < WIN_LEN) & (kpos[None, :] >= 0)
        sw = jnp.einsum('bhqd,bkd->bhqk', qb, kwb).astype(jnp.float32) * scale
        pw = jax.nn.softmax(jnp.where(w_valid, sw, NEG), axis=-1)
        o_win = jnp.einsum('bhqk,bkd->bhqd', pw.astype(vwb.dtype), vwb)
        g = lax.dynamic_slice_in_dim(gates, s0, Q_BLOCK, axis=2)
        return g[..., 0:1] * o_cmp + g[..., 1:2] * o_sel + g[..., 2:3] * o_win

    return sweep_query_blocks(block, S // Q_BLOCK)


def moba_attention(q, k, v):
    B, H, S, dh = q.shape
    scale = HEAD_DIM ** -0.5
    n_blk = -(-S // MOBA_BLOCK)
    pad = n_blk * MOBA_BLOCK - S
    kp = jnp.pad(k, ((0, 0), (0, 0), (0, pad), (0, 0)))
    vp = jnp.pad(v, ((0, 0), (0, 0), (0, pad), (0, 0)))
    k_blk = kp.reshape(B, H, n_blk, MOBA_BLOCK, dh)
    v_blk = vp.reshape(B, H, n_blk, MOBA_BLOCK, dh)
    k_mean = jnp.mean(k_blk.astype(jnp.float32), axis=3)
    n_top = min(MOBA_TOPK, max(n_blk - 1, 1))
    n_sel_keys = n_top * MOBA_BLOCK
    blk_ids = jnp.arange(n_blk)
    gather_blocks = jax.vmap(jax.vmap(lambda tb, idx: tb[idx]))

    def block(i):
        s0, qpos, qb = query_block(q, i)
        own = s0 // MOBA_BLOCK
        gsc = jnp.einsum('bhqd,bhnd->bhqn', qb.astype(jnp.float32), k_mean)
        gsc = jnp.where(blk_ids[None, :] < own, gsc, NEG)
        _, sel = lax.top_k(gsc, n_top)
        sel_valid = jnp.repeat(sel < own, MOBA_BLOCK, axis=-1)
        kg = gather_blocks(k_blk, sel).reshape(B, H, Q_BLOCK, n_sel_keys, dh)
        vg = gather_blocks(v_blk, sel).reshape(B, H, Q_BLOCK, n_sel_keys, dh)
        ss = jnp.einsum('bhqd,bhqmd->bhqm', qb, kg).astype(jnp.float32) * scale
        ss = jnp.where(sel_valid, ss, NEG)
        own_start = own * MOBA_BLOCK
        ko = lax.dynamic_slice_in_dim(kp, own_start, MOBA_BLOCK, axis=2)
        vo = lax.dynamic_slice_in_dim(vp, own_start, MOBA_BLOCK, axis=2)
        opos = own_start + jnp.arange(MOBA_BLOCK)
        so = jnp.einsum('bhqd,bhkd->bhqk', qb, ko).astype(jnp.float32) * scale
        so = jnp.where(opos[None, :] <= qpos[:, None], so, NEG)
        p = jax.nn.softmax(jnp.concatenate([ss, so], axis=-1), axis=-1)
        o_sel = jnp.einsum('bhqm,bhqmd->bhqd', p[..., :n_sel_keys].astype(vg.dtype), vg)
        o_own = jnp.einsum('bhqk,bhkd->bhqd', p[..., n_sel_keys:].astype(vo.dtype), vo)
        return o_sel + o_own

    return sweep_query_blocks(block, S // Q_BLOCK)


def hybrid_layer(x, c, norm_pre, norm_post, w_mod, b_mod, w_in, w_out,
                 cmp_pos, cmp_w1, cmp_b1, cmp_w2, cos, sin, cos_c, sin_c):
    B, S, _ = x.shape
    mod = jax.nn.silu(c) @ w_mod + b_mod
    shift, scale, gate = jnp.split(mod[:, None, :], 3, axis=-1)
    h = rms_norm(x, norm_pre) * (1 + scale) + shift
    proj = h @ w_in
    (qa, ka, va, za, qb, kb, vb, zb, qc, kcr, vcr, ksr, vsr, kwr, vwr, gc, zc,
     qd, kd, vd, zd) = jnp.split(proj, np.cumsum(IN_SPLITS)[:-1].tolist(), axis=-1)
    oa = stick_breaking_attention(to_heads(qa), to_heads(ka), to_heads(va))
    ob = dilated_window_attention(apply_rope(to_heads(qb), cos, sin),
                                  apply_rope(to_heads(kb), cos, sin), to_heads(vb))
    kc = apply_rope(compress_tokens(kcr, cmp_pos[0], cmp_w1[0], cmp_b1[0], cmp_w2[0]), cos_c, sin_c)
    vc = compress_tokens(vcr, cmp_pos[1], cmp_w1[1], cmp_b1[1], cmp_w2[1])
    gates = jax.nn.sigmoid(gc.astype(jnp.float32)).reshape(B, S, N_HEADS, 3)
    gates = gates.transpose(0, 2, 1, 3).astype(x.dtype)
    oc = nsa_attention(apply_rope(to_heads(qc), cos, sin), kc, vc,
                       apply_rope(ksr, cos, sin), vsr, apply_rope(kwr, cos, sin), vwr, gates)
    od = moba_attention(apply_rope(to_heads(qd), cos, sin),
                        apply_rope(to_heads(kd), cos, sin), to_heads(vd))
    mixed = jnp.concatenate([from_heads(oa) * jax.nn.silu(za), from_heads(ob) * jax.nn.silu(zb),
                             from_heads(oc) * jax.nn.silu(zc), from_heads(od) * jax.nn.silu(zd)],
                            axis=-1)
    y = rms_norm(mixed @ w_out, norm_post)
    return x + gate * y


def setup_inputs(seed: int = 0) -> dict:
    key = jax.random.key(seed)
    ks = jax.random.split(key, 12)
    f = jnp.float32
    nrm = jax.random.normal
    x = nrm(ks[0], (BATCH, SEQ, D_MODEL), f)
    c = nrm(ks[1], (BATCH, D_MODEL), f)
    norm_pre = 1.0 + 0.05 * nrm(ks[2], (DEPTH, D_MODEL), f)
    norm_post = 1.0 + 0.05 * nrm(ks[3], (DEPTH, D_MODEL), f)
    w_mod = nrm(ks[4], (DEPTH, D_MODEL, 3 * D_MODEL), f) * (0.5 * D_MODEL ** -0.5)
    b_mod = 0.01 * nrm(ks[5], (DEPTH, 3 * D_MODEL), f)
    w_in = nrm(ks[6], (DEPTH, D_MODEL, N_IN), f) * D_MODEL ** -0.5
    w_out = nrm(ks[7], (DEPTH, MIX_WIDTH, D_MODEL), f) * MIX_WIDTH ** -0.5
    cmp_pos = 0.1 * nrm(ks[8], (DEPTH, 2, CMP_LEN, HEAD_DIM), f)
    cmp_w1 = nrm(ks[9], (DEPTH, 2, CMP_LEN * HEAD_DIM, CMP_HID), f) * (CMP_LEN * HEAD_DIM) ** -0.5
    cmp_b1 = 0.01 * nrm(ks[10], (DEPTH, 2, CMP_HID), f)
    cmp_w2 = nrm(ks[11], (DEPTH, 2, CMP_HID, HEAD_DIM), f) * CMP_HID ** -0.5
    return {'x': x, 'c': c, 'norm_pre': norm_pre, 'norm_post': norm_post,
            'w_mod': w_mod, 'b_mod': b_mod, 'w_in': w_in, 'w_out': w_out,
            'cmp_pos': cmp_pos, 'cmp_w1': cmp_w1, 'cmp_b1': cmp_b1, 'cmp_w2': cmp_w2}


def reference(x, c, norm_pre, norm_post, w_mod, b_mod, w_in, w_out, cmp_pos, cmp_w1, cmp_b1, cmp_w2):
    S = x.shape[1]
    cos, sin = rope_tables(np.arange(S))
    n_cmp = (S - CMP_LEN) // CMP_STRIDE + 1
    cos_c, sin_c = rope_tables(np.arange(n_cmp) * CMP_STRIDE + CMP_LEN - 1)
    for l in range(DEPTH):
        x = hybrid_layer(x, c, norm_pre[l], norm_post[l], w_mod[l], b_mod[l], w_in[l], w_out[l],
                         cmp_pos[l], cmp_w1[l], cmp_b1[l], cmp_w2[l], cos, sin, cos_c, sin_c)
    return x
```

```python
import functools

import numpy as np
import jax
import jax.numpy as jnp
from jax import lax
from jax.experimental import pallas as pl
from jax.experimental.pallas import tpu as pltpu

F32 = jnp.float32
BF16 = jnp.bfloat16

N_HEADS = 4
HEAD_DIM = 64
MIX_W = N_HEADS * HEAD_DIM
ROPE_THETA = 500000.0
ROPE_DIM = HEAD_DIM // 4
ROPE_HALF = ROPE_DIM // 2
EPS = 1e-6
NEG = -1e30
BIG = 1e9
SCALE = HEAD_DIM ** -0.5
DIL_CONFIGS = ((128, 1), (512, 4), (2048, 16))
CMP_LEN = 32
CMP_STRIDE = 16
CMP_HID = 256
SEL_LEN = 64
SEL_TOPN = 16
WIN_LEN = 512
MOBA_BLOCK = 256
MOBA_TOPK = 3

LANES = 128
TILE_W = 2 * LANES
N_PROJ_TILES = 17
DIL_GROUP = 2048
VMEM_LIMIT = 56 * 1024 * 1024


def _nt_dot(a, b):
    return lax.dot_general(a, b, (((1,), (1,)), ((), ())), preferred_element_type=F32)


def _dot(a, b):
    return jnp.dot(a, b, preferred_element_type=F32)


def _split_bf16(x):
    hi = x.astype(BF16)
    lo = (x - hi.astype(F32)).astype(BF16)
    return hi, lo


def _silu(x):
    return x * jax.nn.sigmoid(x)


def _head_masks(shape):
    lane = lax.broadcasted_iota(jnp.int32, shape, 1)
    return lane < HEAD_DIM


def _online_update(s, v, m, l, acc):
    m_new = jnp.maximum(m, jnp.max(s, axis=-1, keepdims=True))
    alpha = jnp.exp(m - m_new)
    p = jnp.exp(s - m_new)
    l = alpha * l + jnp.sum(p, axis=-1, keepdims=True)
    acc = alpha * acc + _dot(p.astype(BF16), v)
    return m_new, l, acc


def _mod_kernel(c_ref, w_ref, b_ref, o_ref):
    c = c_ref[...]
    o_ref[...] = jnp.dot(_silu(c), w_ref[...], preferred_element_type=F32,
                         precision=lax.Precision.HIGHEST) + b_ref[...]


def _modulation(c, w_mod, b_mod):
    B, D = c.shape
    N = w_mod.shape[1]
    rows = 8
    cp = jnp.zeros((rows, D), F32).at[:B].set(c)
    tn = N // 4
    out = pl.pallas_call(
        _mod_kernel,
        grid=(N // tn,),
        in_specs=[pl.BlockSpec((rows, D), lambda j: (0, 0)),
                  pl.BlockSpec((D, tn), lambda j: (0, j)),
                  pl.BlockSpec((1, tn), lambda j: (0, j))],
        out_specs=pl.BlockSpec((rows, tn), lambda j: (0, j)),
        out_shape=jax.ShapeDtypeStruct((rows, N), F32),
        name="modulation",
    )(cp, w_mod, b_mod[None])
    return out[:B]


def _rope_lane_tile(y, cos, sa, sb):
    return y * cos + pltpu.roll(y, LANES - ROPE_HALF, 1) * sa + pltpu.roll(y, ROPE_HALF, 1) * sb


def _proj_kernel(x_ref, mod_ref, g_ref, w_ref, cos_ref, sa_ref, sb_ref,
                 qa_ref, ka_ref, va_ref, za_ref,
                 qb_ref, kb_ref, vb_ref, zb_ref,
                 qc_ref, kcr_ref, vcr_ref, gc_ref, ks_ref, vs_ref, kw_ref, vw_ref, zc_ref,
                 qd_ref, kd_ref, vd_ref, zd_ref):
    D = x_ref.shape[-1]
    x = x_ref[0]
    ms = jnp.mean(x * x, axis=-1, keepdims=True)
    y = x * lax.rsqrt(ms + EPS) * g_ref[...]
    shift = mod_ref[0, :, 0:D]
    scale = mod_ref[0, :, D:2 * D]
    h = (y * (1.0 + scale) + shift).astype(BF16)
    cos, sa, sb = cos_ref[...], sa_ref[...], sb_ref[...]

    def tile(t):
        return _dot(h, w_ref[:, t * TILE_W:(t + 1) * TILE_W])

    def rope_lo(y):
        return _rope_lane_tile(y[:, :LANES], cos, sa, sb)

    def rope_all(y):
        return jnp.concatenate([rope_lo(y), _rope_lane_tile(y[:, LANES:], cos, sa, sb)], axis=1)

    qa_ref[0] = tile(0).astype(BF16)
    ka_ref[0] = tile(1).astype(BF16)
    va_ref[0] = tile(2).astype(BF16)
    za_ref[0] = tile(3).astype(BF16)
    qb_ref[0] = rope_all(tile(4))
    kb_ref[0] = rope_all(tile(5))
    vb_ref[0] = tile(6)
    zb_ref[0] = tile(7).astype(BF16)
    qc_ref[0] = rope_all(tile(8)).astype(BF16)
    t9 = tile(9)
    kcr_ref[0] = t9[:, 0:HEAD_DIM].astype(BF16)
    vcr_ref[0] = t9[:, HEAD_DIM:LANES].astype(BF16)
    gc_ref[0] = t9[:, LANES:]
    t10 = tile(10)
    ks_ref[0] = rope_lo(t10).astype(BF16)
    vs_ref[0] = t10[:, LANES:].astype(BF16)
    t11 = tile(11)
    kw_ref[0] = rope_lo(t11).astype(BF16)
    vw_ref[0] = t11[:, LANES:].astype(BF16)
    zc_ref[0] = tile(12).astype(BF16)
    qd_ref[0] = rope_all(tile(13)).astype(BF16)
    kd_ref[0] = rope_all(tile(14)).astype(BF16)
    vd_ref[0] = tile(15).astype(BF16)
    zd_ref[0] = tile(16).astype(BF16)


def _pack_w_in(w_in):
    D = w_in.shape[0]
    offs = np.cumsum([0, MIX_W, MIX_W, MIX_W, MIX_W, MIX_W, MIX_W, MIX_W, MIX_W,
                      MIX_W, HEAD_DIM, HEAD_DIM, HEAD_DIM, HEAD_DIM, HEAD_DIM, HEAD_DIM,
                      3 * N_HEADS, MIX_W, MIX_W, MIX_W, MIX_W, MIX_W])
    seg = lambda i: w_in[:, offs[i]:offs[i + 1]]
    zeros = lambda n: jnp.zeros((D, n), w_in.dtype)
    parts = [seg(0), seg(1), seg(2), seg(3), seg(4), seg(5), seg(6), seg(7), seg(8),
             seg(9), seg(10), seg(15), zeros(LANES - 3 * N_HEADS),
             seg(11), seg(11), seg(12), seg(12),
             seg(13), seg(13), seg(14), seg(14),
             seg(16), seg(17), seg(18), seg(19), seg(20)]
    return jnp.concatenate(parts, axis=1).astype(BF16)


def _rope_angles(pos):
    inv_freq = 1.0 / (ROPE_THETA ** (np.arange(0, ROPE_DIM, 2, dtype=np.float32) / ROPE_DIM))
    ang = jnp.asarray(pos, F32)[:, None] * jnp.asarray(inv_freq, F32)[None, :]
    return jnp.cos(ang), jnp.sin(ang)


def _rope_lane_tables(pos):
    cos, sin = _rope_angles(pos)
    n = cos.shape[0]
    rest = HEAD_DIM - ROPE_DIM
    cos_h = jnp.concatenate([cos, cos, jnp.ones((n, rest), F32)], axis=1)
    sa_h = jnp.concatenate([-sin, jnp.zeros((n, HEAD_DIM - ROPE_HALF), F32)], axis=1)
    sb_h = jnp.concatenate([jnp.zeros((n, ROPE_HALF), F32), sin, jnp.zeros((n, rest), F32)], axis=1)
    two = lambda t: jnp.concatenate([t, t], axis=1)
    return two(cos_h), two(sa_h), two(sb_h)


def _projection(x, mod, g_pre, w_cat, tables, tm):
    B, S, D = x.shape
    cos, sa, sb = tables
    row = lambda w: pl.BlockSpec((1, tm, w), lambda b, i: (b, i, 0))
    tab = pl.BlockSpec((tm, LANES), lambda b, i: (i, 0))
    sds = lambda w, dt: jax.ShapeDtypeStruct((B, S, w), dt)
    widths = [(MIX_W, BF16)] * 4 + [(MIX_W, F32)] * 3 + [(MIX_W, BF16)] + \
             [(MIX_W, BF16), (HEAD_DIM, BF16), (HEAD_DIM, BF16), (LANES, F32),
              (LANES, BF16), (LANES, BF16), (LANES, BF16), (LANES, BF16), (MIX_W, BF16)] + \
             [(MIX_W, BF16)] * 4
    return pl.pallas_call(
        _proj_kernel,
        grid=(B, S // tm),
        in_specs=[row(D),
                  pl.BlockSpec((1, 1, 3 * D), lambda b, i: (b, 0, 0)),
                  pl.BlockSpec((1, D), lambda b, i: (0, 0)),
                  pl.BlockSpec(w_cat.shape, lambda b, i: (0, 0)),
                  tab, tab, tab],
        out_specs=[row(w) for w, _ in widths],
        out_shape=[sds(w, dt) for w, dt in widths],
        compiler_params=pltpu.CompilerParams(
            dimension_semantics=("parallel", "parallel"), vmem_limit_bytes=VMEM_LIMIT),
        name="in_projection",
    )(x, mod[:, None, :], g_pre[None], w_cat, cos, sa, sb)


def _stick_kernel(q_ref, k_ref, v_ref, z_ref, o_ref, *, tq):
    i = pl.program_id(2)
    tk = tq
    is_a = _head_masks((tq, LANES))
    qs = (q_ref[0].astype(F32) * SCALE).astype(BF16)
    qh = (jnp.where(is_a, qs, 0), jnp.where(is_a, 0, qs))
    r_i = lax.broadcasted_iota(jnp.int32, (tk, tk), 0)
    c_i = lax.broadcasted_iota(jnp.int32, (tk, tk), 1)
    suffix = jnp.where(r_i >= c_i, 1.0, 0.0).astype(BF16)
    qpos = i * tq + lax.broadcasted_iota(jnp.int32, (tq, tk), 0)
    col = lax.broadcasted_iota(jnp.int32, (tq, tk), 1)

    def body(t, carry):
        j = i - t
        start = pl.multiple_of(j * tk, tk)
        ks = k_ref[0, pl.ds(start, tk), :]
        vs = v_ref[0, pl.ds(start, tk), :]
        past = (j * tk + col) < qpos
        out = []
        for h in range(2):
            acc, car = carry[2 * h], carry[2 * h + 1]
            z = _nt_dot(qh[h], ks)
            lr = jnp.minimum(-z, 0.0) - jnp.log1p(jnp.exp(-jnp.abs(z)))
            lr = jnp.where(past, lr, 0.0)
            hi, lo = _split_bf16(lr)
            cs = _dot(hi, suffix) + _dot(lo, suffix)
            a = jnp.where(past, jnp.exp(z + cs + car), 0.0)
            acc = acc + _dot(a.astype(BF16), vs)
            car = car + cs[:, 0:1]
            out += [acc, car]
        return tuple(out)

    zero_acc = jnp.zeros((tq, LANES), F32)
    zero_car = jnp.zeros((tq, 1), F32)
    acc_a, _, acc_b, _ = lax.fori_loop(0, i + 1, body, (zero_acc, zero_car, zero_acc, zero_car))
    out = jnp.where(is_a, acc_a, acc_b)
    o_ref[0] = (out * _silu(z_ref[0].astype(F32))).astype(BF16)


def _pair_specs(S, tq):
    qspec = pl.BlockSpec((1, tq, LANES), lambda b, p, i: (b, i, p))
    kspec = pl.BlockSpec((1, S, LANES), lambda b, p, i: (b, 0, p))
    return qspec, kspec


def _stick_breaking(q, k, v, z, tq=256):
    B, S, _ = q.shape
    qspec, kspec = _pair_specs(S, tq)
    return pl.pallas_call(
        functools.partial(_stick_kernel, tq=tq),
        grid=(B, 2, S // tq),
        in_specs=[qspec, kspec, kspec, qspec],
        out_specs=qspec,
        out_shape=jax.ShapeDtypeStruct((B, S, MIX_W), BF16),
        compiler_params=pltpu.CompilerParams(
            dimension_semantics=("parallel", "parallel", "arbitrary"), vmem_limit_bytes=VMEM_LIMIT),
        name="stick_breaking",
    )(q, k, v, z)


def _dil_kernel(q_ref, k_ref, v_ref, z_ref, o_ref, m_scr, l_scr, a_scr, *, G):
    g = pl.program_id(2)
    blk = LANES
    is_a = _head_masks((blk, LANES))
    row = lax.broadcasted_iota(jnp.int32, (blk, blk), 0)
    col = lax.broadcasted_iota(jnp.int32, (blk, blk), 1)
    n_sub = G // blk

    for ci, (_, d) in enumerate(DIL_CONFIGS):
        nb = G // (blk * d)
        sh = int(np.log2(nb))

        def sub(n, _, ci=ci, d=d, nb=nb, sh=sh):
            r = lax.shift_right_logical(n, sh)
            ub = jnp.bitwise_and(n, nb - 1)
            loc = ub * (blk * d) + r
            glob = g * G + loc
            has_prev = glob >= blk * d
            pstart = jnp.maximum(glob - blk * d, r)
            qs = (q_ref[0, pl.ds(loc, blk, stride=d), :] * SCALE).astype(BF16)
            kc = k_ref[0, pl.ds(glob, blk, stride=d), :].astype(BF16)
            vc = v_ref[0, pl.ds(glob, blk, stride=d), :].astype(BF16)
            kp = k_ref[0, pl.ds(pstart, blk, stride=d), :].astype(BF16)
            vp = v_ref[0, pl.ds(pstart, blk, stride=d), :].astype(BF16)
            cur_ok = row >= col
            prev_ok = jnp.logical_and(col >= row, has_prev)
            res = []
            for h in range(2):
                qh = jnp.where(is_a, qs, 0) if h == 0 else jnp.where(is_a, 0, qs)
                s_c = jnp.where(cur_ok, _nt_dot(qh, kc), NEG)
                s_p = jnp.where(prev_ok, _nt_dot(qh, kp), NEG)
                m = jnp.maximum(jnp.max(s_c, axis=-1, keepdims=True),
                                jnp.max(s_p, axis=-1, keepdims=True))
                p_c = jnp.exp(s_c - m)
                p_p = jnp.exp(s_p - m)
                l = jnp.sum(p_c, axis=-1, keepdims=True) + jnp.sum(p_p, axis=-1, keepdims=True)
                acc = _dot(p_c.astype(BF16), vc) + _dot(p_p.astype(BF16), vp)
                res.append((m, l, acc))
            rows = pl.ds(loc, blk, stride=d)
            m_scr[ci, rows, :] = jnp.where(is_a, res[0][0], res[1][0])
            l_scr[ci, rows, :] = jnp.where(is_a, res[0][1], res[1][1])
            a_scr[ci, rows, :] = jnp.where(is_a, res[0][2], res[1][2])
            return 0

        lax.fori_loop(0, n_sub, sub, 0)

    chunk = 256

    def combine(c, _):
        rows = pl.ds(pl.multiple_of(c * chunk, chunk), chunk)
        m0, m1, m2 = m_scr[0, rows, :], m_scr[1, rows, :], m_scr[2, rows, :]
        mm = jnp.maximum(jnp.maximum(m0, m1), m2)
        w0, w1, w2 = jnp.exp(m0 - mm), jnp.exp(m1 - mm), jnp.exp(m2 - mm)
        num = w0 * a_scr[0, rows, :] + w1 * a_scr[1, rows, :] + w2 * a_scr[2, rows, :]
        den = w0 * l_scr[0, rows, :] + w1 * l_scr[1, rows, :] + w2 * l_scr[2, rows, :]
        o_ref[0, rows, :] = (num / den * _silu(z_ref[0, rows, :].astype(F32))).astype(BF16)
        return 0

    lax.fori_loop(0, G // chunk, combine, 0)


def _dilated(q, k, v, z):
    B, S, _ = q.shape
    G = min(DIL_GROUP, S)
    gspec = pl.BlockSpec((1, G, LANES), lambda b, p, g: (b, g, p))
    kspec = pl.BlockSpec((1, S, LANES), lambda b, p, g: (b, 0, p))
    scr = pltpu.VMEM((len(DIL_CONFIGS), G, LANES), F32)
    return pl.pallas_call(
        functools.partial(_dil_kernel, G=G),
        grid=(B, 2, S // G),
        in_specs=[gspec, kspec, kspec, gspec],
        out_specs=gspec,
        out_shape=jax.ShapeDtypeStruct((B, S, MIX_W), BF16),
        scratch_shapes=[scr, scr, scr],
        compiler_params=pltpu.CompilerParams(
            dimension_semantics=("parallel", "parallel", "arbitrary"), vmem_limit_bytes=VMEM_LIMIT),
        name="dilated_window",
    )(q, k, v, z)


def _compress_kernel(t_ref, p_ref, w1_ref, b1_ref, w2_ref, cos_ref, sa_ref, sb_ref, o_ref):
    half = t_ref.shape[-1]
    n = t_ref.shape[-2]
    t = t_ref[0, 0].astype(F32)
    lo = _dot((t + p_ref[0, 0:1, :]).astype(BF16), w1_ref[0, 0:half, :])
    hi = _dot((t + p_ref[0, 1:2, :]).astype(BF16), w1_ref[0, half:, :])
    pre = lo + pltpu.roll(hi, n - 1, 0) + b1_ref[0]
    hid = 0.5 * pre * (1.0 + jnp.tanh(np.sqrt(2.0 / np.pi).astype(np.float32)
                                      * (pre + 0.044715 * (pre * pre * pre))))
    out = _dot(hid.astype(BF16), w2_ref[0])
    o_ref[0, 0] = _rope_lane_tile(out, cos_ref[0], sa_ref[0], sb_ref[0]).astype(BF16)


def _compress(kcr, vcr, cmp_pos, cmp_w1, cmp_b1, cmp_w2, tables_c):
    B, S, dh = kcr.shape
    n = S // CMP_STRIDE
    half = CMP_STRIDE * dh
    t = jnp.stack([kcr, vcr]).reshape(2, B, n, half)
    pos = cmp_pos.reshape(2, 2, half)
    w1 = cmp_w1.astype(BF16)
    w2 = jnp.concatenate([cmp_w2, cmp_w2], axis=-1).astype(BF16)
    cos, sa, sb = tables_c
    ident = (jnp.ones_like(cos), jnp.zeros_like(sa), jnp.zeros_like(sb))
    cos2, sa2, sb2 = (jnp.stack([a, b]) for a, b in zip((cos, sa, sb), ident))
    kv = lambda *shape: pl.BlockSpec((1,) + shape, lambda s, b: (s,) + (0,) * len(shape))
    return pl.pallas_call(
        _compress_kernel,
        grid=(2, B),
        in_specs=[pl.BlockSpec((1, 1, n, half), lambda s, b: (s, b, 0, 0)),
                  kv(2, half), kv(2 * half, CMP_HID), kv(1, CMP_HID), kv(CMP_HID, LANES),
                  kv(n, LANES), kv(n, LANES), kv(n, LANES)],
        out_specs=pl.BlockSpec((1, 1, n, LANES), lambda s, b: (s, b, 0, 0)),
        out_shape=jax.ShapeDtypeStruct((2, B, n, LANES), BF16),
        compiler_params=pltpu.CompilerParams(
            dimension_semantics=("parallel", "parallel"), vmem_limit_bytes=VMEM_LIMIT),
        name="compress_tokens",
    )(t, pos, w1, cmp_b1[:, None, :], w2, cos2, sa2, sb2)


def _nsa_kernel(q_ref, kc_ref, vc_ref, ks_ref, vs_ref, kw_ref, vw_ref, gc_ref, z_ref,
                ov_ref, ex_ref, o_ref, bias_scr, *, tq, n_sel):
    i = pl.program_id(1)
    tk = tq
    R = N_HEADS * tq
    is_a = _head_masks((tq, LANES))
    qs = (q_ref[0].astype(F32) * SCALE).astype(BF16)
    q01, q23 = qs[:, :LANES], qs[:, LANES:]
    q4 = jnp.concatenate([jnp.where(is_a, q01, 0), jnp.where(is_a, 0, q01),
                          jnp.where(is_a, q23, 0), jnp.where(is_a, 0, q23)], axis=0)
    stack = lambda x: jnp.concatenate([x] * N_HEADS, axis=0)
    qpos1 = i * tq + lax.broadcasted_iota(jnp.int32, (tq, 1), 0)

    n_c = kc_ref.shape[1]
    c_end = lax.broadcasted_iota(jnp.int32, (tq, n_c), 1) * CMP_STRIDE + (CMP_LEN - 1)
    c_valid = stack(c_end <= qpos1)
    sc = jnp.where(c_valid, _nt_dot(q4, kc_ref[0]), NEG)
    pc = jnp.exp(sc - jnp.max(sc, axis=-1, keepdims=True)) * jnp.where(c_valid, 1.0, 0.0)
    lc = jnp.sum(pc, axis=-1, keepdims=True)
    pc = pc * jnp.where(lc > 0.0, 1.0 / lc, 0.0)
    o_cmp = _dot(pc.astype(BF16), vc_ref[0])

    pcs = pc[0:tq] + pc[tq:2 * tq] + pc[2 * tq:3 * tq] + pc[3 * tq:]
    hi, lo = _split_bf16(pcs)
    imp = _dot(hi, ov_ref[...]) + _dot(lo, ov_ref[...])
    blk = lax.broadcasted_iota(jnp.int32, (tq, LANES), 1)
    own = lax.shift_right_logical(qpos1, int(np.log2(SEL_LEN)))
    s_valid = blk <= own
    forced = jnp.logical_or(blk == 0, blk >= own - 1)
    imp = jnp.where(s_valid, jnp.where(forced, BIG, imp), NEG)
    rank = jnp.zeros((tq, LANES), F32)
    for s in range(n_sel):
        c = imp[:, s:s + 1]
        beats = jnp.logical_or(c > imp, jnp.logical_and(c == imp, blk > s))
        rank = rank + jnp.where(beats, 1.0, 0.0)
    chosen = jnp.logical_and(rank < float(min(SEL_TOPN, n_sel)), s_valid)
    sel = jnp.where(chosen, 1.0, 0.0).astype(BF16)
    for j in range(bias_scr.shape[0]):
        bias_scr[j] = (_dot(sel, ex_ref[:, j * tk:(j + 1) * tk]) - 1.0) * (-NEG)

    row = lax.broadcasted_iota(jnp.int32, (R, tk), 0)
    col = lax.broadcasted_iota(jnp.int32, (R, tk), 1)
    qrow = jnp.bitwise_and(row, tq - 1)
    init = (jnp.full((R, 1), NEG, F32), jnp.zeros((R, 1), F32), jnp.zeros((R, LANES), F32))
    diag = pl.ds(pl.multiple_of(i * tk, tk), tk)

    s_d = jnp.where(col <= qrow, _nt_dot(q4, ks_ref[0, diag, :]) + stack(bias_scr[i]), NEG)
    state = _online_update(s_d, vs_ref[0, diag, :], *init)

    def sel_body(j, st):
        rows = pl.ds(pl.multiple_of(j * tk, tk), tk)
        s = _nt_dot(q4, ks_ref[0, rows, :]) + stack(bias_scr[j])
        return _online_update(s, vs_ref[0, rows, :], *st)

    _, l_s, a_s = lax.fori_loop(0, i, sel_body, state)
    o_sel = a_s / l_s

    s_d = jnp.where(col <= qrow, _nt_dot(q4, kw_ref[0, diag, :]), NEG)
    state = _online_update(s_d, vw_ref[0, diag, :], *init)
    n_win = WIN_LEN // tk

    def win_body(t, st):
        j = i - t
        rows = pl.ds(pl.multiple_of(j * tk, tk), tk)
        dist = qrow - col + t * tk
        s = jnp.where(dist < WIN_LEN, _nt_dot(q4, kw_ref[0, rows, :]), NEG)
        return _online_update(s, vw_ref[0, rows, :], *st)

    _, l_w, a_w = lax.fori_loop(1, jnp.minimum(i, n_win) + 1, win_body, state)
    o_win = a_w / l_w

    gates = jax.nn.sigmoid(gc_ref[0])
    outs = []
    for h in range(N_HEADS):
        rows = slice(h * tq, (h + 1) * tq)
        outs.append(gates[:, 3 * h:3 * h + 1] * o_cmp[rows]
                    + gates[:, 3 * h + 1:3 * h + 2] * o_sel[rows]
                    + gates[:, 3 * h + 2:3 * h + 3] * o_win[rows])
    out = jnp.concatenate([jnp.where(is_a, outs[0], outs[1]), jnp.where(is_a, outs[2], outs[3])], axis=1)
    o_ref[0] = (out * _silu(z_ref[0].astype(F32))).astype(BF16)


def _nsa(q, kc2, vc2, ks2, vs2, kw2, vw2, gc, z, tq=128):
    B, S, _ = q.shape
    n_c = kc2.shape[1]
    n_sel = S // SEL_LEN
    n_cmp = (S - CMP_LEN) // CMP_STRIDE + 1
    c_start = np.arange(n_c) * CMP_STRIDE
    s_start = np.arange(LANES) * SEL_LEN
    overlap = np.clip(np.minimum(c_start[:, None] + CMP_LEN, s_start[None, :] + SEL_LEN)
                      - np.maximum(c_start[:, None], s_start[None, :]), 0, None) / CMP_LEN
    overlap[n_cmp:, :] = 0.0
    overlap[:, n_sel:] = 0.0
    expand = (np.arange(S)[None, :] // SEL_LEN == np.arange(LANES)[:, None]).astype(np.float32)
    row = lambda w: pl.BlockSpec((1, tq, w), lambda b, i: (b, i, 0))
    full = lambda n: pl.BlockSpec((1, n, LANES), lambda b, i: (b, 0, 0))
    const = lambda a: pl.BlockSpec(a.shape, lambda b, i: (0, 0))
    ov = jnp.asarray(overlap, BF16)
    ex = jnp.asarray(expand, BF16)
    return pl.pallas_call(
        functools.partial(_nsa_kernel, tq=tq, n_sel=n_sel),
        grid=(B, S // tq),
        in_specs=[row(MIX_W), full(n_c), full(n_c), full(S), full(S), full(S), full(S),
                  row(LANES), row(MIX_W), const(ov), const(ex)],
        out_specs=row(MIX_W),
        out_shape=jax.ShapeDtypeStruct((B, S, MIX_W), BF16),
        scratch_shapes=[pltpu.VMEM((S // tq, tq, tq), F32)],
        compiler_params=pltpu.CompilerParams(
            dimension_semantics=("parallel", "arbitrary"), vmem_limit_bytes=VMEM_LIMIT),
        name="native_sparse",
    )(q, kc2, vc2, ks2, vs2, kw2, vw2, gc, z, ov, ex)


def _moba_kernel(q_ref, k_ref, v_ref, z_ref, o_ref, km_scr, bias_scr, *, n_blk):
    i = pl.program_id(2)
    tq = tk = MOBA_BLOCK
    is_a = _head_masks((tq, LANES))

    @pl.when(i == 0)
    def _():
        km_scr[...] = jnp.zeros_like(km_scr)
        for b in range(n_blk):
            kb = k_ref[0, b * tk:(b + 1) * tk, :].astype(F32)
            km_scr[b:b + 1, :] = jnp.mean(kb, axis=0, keepdims=True)

    q = q_ref[0]
    qs = (q.astype(F32) * SCALE).astype(BF16)
    km_hi, km_lo = _split_bf16(km_scr[...])
    blk = lax.broadcasted_iota(jnp.int32, (tq, LANES), 1)
    n_top = min(MOBA_TOPK, max(n_blk - 1, 1))
    qh = []
    for h in range(2):
        keep = is_a if h == 0 else jnp.logical_not(is_a)
        qh.append(jnp.where(keep, qs, 0))
        qg = jnp.where(keep, q, 0)
        gsc = _nt_dot(qg, km_hi) + _nt_dot(qg, km_lo)
        gsc = jnp.where(blk < i, gsc, NEG)
        rank = jnp.zeros((tq, LANES), F32)
        for s in range(n_blk):
            c = gsc[:, s:s + 1]
            beats = jnp.logical_or(c > gsc, jnp.logical_and(c == gsc, blk > s))
            rank = rank + jnp.where(beats, 1.0, 0.0)
        chosen = jnp.logical_and(rank < float(n_top), blk < i)
        bias = jnp.where(chosen, 0.0, NEG)
        for s in range(n_blk):
            bias_scr[h, s] = jnp.broadcast_to(bias[:, s:s + 1], (tq, LANES))

    row = lax.broadcasted_iota(jnp.int32, (tq, tk), 0)
    col = lax.broadcasted_iota(jnp.int32, (tq, tk), 1)
    diag = pl.ds(pl.multiple_of(i * tk, tk), tk)
    k_d, v_d = k_ref[0, diag, :], v_ref[0, diag, :]
    state = []
    for h in range(2):
        init = (jnp.full((tq, 1), NEG, F32), jnp.zeros((tq, 1), F32), jnp.zeros((tq, LANES), F32))
        s_d = jnp.where(col <= row, _nt_dot(qh[h], k_d), NEG)
        state += list(_online_update(s_d, v_d, *init))

    def body(j, st):
        rows = pl.ds(pl.multiple_of(j * tk, tk), tk)
        ks, vs = k_ref[0, rows, :], v_ref[0, rows, :]
        out = []
        for h in range(2):
            b = bias_scr[h, j]
            s = _nt_dot(qh[h], ks) + jnp.concatenate([b, b], axis=1)
            out += list(_online_update(s, vs, *st[3 * h:3 * h + 3]))
        return tuple(out)

    st = lax.fori_loop(0, i, body, tuple(state))
    out = jnp.where(is_a, st[2] / st[1], st[5] / st[4])
    o_ref[0] = (out * _silu(z_ref[0].astype(F32))).astype(BF16)


def _moba(q, k, v, z):
    B, S, _ = q.shape
    tq = MOBA_BLOCK
    n_blk = S // MOBA_BLOCK
    qspec, kspec = _pair_specs(S, tq)
    return pl.pallas_call(
        functools.partial(_moba_kernel, n_blk=n_blk),
        grid=(B, 2, S // tq),
        in_specs=[qspec, kspec, kspec, qspec],
        out_specs=qspec,
        out_shape=jax.ShapeDtypeStruct((B, S, MIX_W), BF16),
        scratch_shapes=[pltpu.VMEM((LANES, LANES), F32),
                        pltpu.VMEM((2, n_blk, tq, LANES), F32)],
        compiler_params=pltpu.CompilerParams(
            dimension_semantics=("parallel", "parallel", "arbitrary"), vmem_limit_bytes=VMEM_LIMIT),
        name="moba",
    )(q, k, v, z)


def _out_kernel(ma_ref, mb_ref, mc_ref, md_ref, w_ref, x_ref, mod_ref, g_ref, o_ref):
    D = x_ref.shape[-1]
    acc = _dot(ma_ref[0], w_ref[0:MIX_W, :])
    acc = acc + _dot(mb_ref[0], w_ref[MIX_W:2 * MIX_W, :])
    acc = acc + _dot(mc_ref[0], w_ref[2 * MIX_W:3 * MIX_W, :])
    acc = acc + _dot(md_ref[0], w_ref[3 * MIX_W:, :])
    ms = jnp.mean(acc * acc, axis=-1, keepdims=True)
    y = acc * lax.rsqrt(ms + EPS) * g_ref[...]
    o_ref[0] = x_ref[0] + mod_ref[0, :, 2 * D:] * y


def _out_projection(mixed, w_out, x, mod, g_post, tm):
    B, S, D = x.shape
    row = lambda w: pl.BlockSpec((1, tm, w), lambda b, i: (b, i, 0))
    return pl.pallas_call(
        _out_kernel,
        grid=(B, S // tm),
        in_specs=[row(MIX_W)] * 4 + [pl.BlockSpec(w_out.shape, lambda b, i: (0, 0)), row(D),
                                     pl.BlockSpec((1, 1, 3 * D), lambda b, i: (b, 0, 0)),
                                     pl.BlockSpec((1, D), lambda b, i: (0, 0))],
        out_specs=row(D),
        out_shape=jax.ShapeDtypeStruct((B, S, D), F32),
        compiler_params=pltpu.CompilerParams(
            dimension_semantics=("parallel", "parallel"), vmem_limit_bytes=VMEM_LIMIT),
        name="out_projection",
    )(*mixed, w_out.astype(BF16), x, mod[:, None, :], g_post[None])


def _layer(x, c, norm_pre, norm_post, w_mod, b_mod, w_in, w_out, cmp_pos, cmp_w1, cmp_b1, cmp_w2,
           tables, tables_c, tm):
    mod = _modulation(c, w_mod, b_mod)
    (qa, ka, va, za, qb, kb, vb, zb, qc, kcr, vcr, gc, ks2, vs2, kw2, vw2, zc,
     qd, kd, vd, zd) = _projection(x, mod, norm_pre, _pack_w_in(w_in), tables, tm)
    oa = _stick_breaking(qa, ka, va, za)
    ob = _dilated(qb, kb, vb, zb)
    kvc = _compress(kcr, vcr, cmp_pos, cmp_w1, cmp_b1, cmp_w2, tables_c)
    oc = _nsa(qc, kvc[0], kvc[1], ks2, vs2, kw2, vw2, gc, zc)
    od = _moba(qd, kd, vd, zd)
    return _out_projection((oa, ob, oc, od), w_out, x, mod, norm_post, tm)


def kernel(x, c, norm_pre, norm_post, w_mod, b_mod, w_in, w_out, cmp_pos, cmp_w1, cmp_b1, cmp_w2):
    S = x.shape[1]
    tables = _rope_lane_tables(np.arange(S))
    tables_c = _rope_lane_tables(np.arange(S // CMP_STRIDE) * CMP_STRIDE + CMP_LEN - 1)
    tm = min(512, S)
    for l in range(norm_pre.shape[0]):
        x = _layer(x, c, norm_pre[l], norm_post[l], w_mod[l], b_mod[l], w_in[l], w_out[l],
                   cmp_pos[l], cmp_w1[l], cmp_b1[l], cmp_w2[l], tables, tables_c, tm)
    return x
```

```python
import functools

import numpy as np
import jax
import jax.numpy as jnp
from jax import lax
from jax.experimental import pallas as pl
from jax.experimental.pallas import tpu as pltpu

F32 = jnp.float32
BF16 = jnp.bfloat16

N_HEADS = 4
HEAD_DIM = 64
MIX_W = N_HEADS * HEAD_DIM
ROPE_THETA = 500000.0
ROPE_DIM = HEAD_DIM // 4
ROPE_HALF = ROPE_DIM // 2
EPS = 1e-6
NEG = -1e30
BIG = 1e9
SCALE = HEAD_DIM ** -0.5
DIL_CONFIGS = ((128, 1), (512, 4), (2048, 16))
CMP_LEN = 32
CMP_STRIDE = 16
CMP_HID = 256
SEL_LEN = 64
SEL_TOPN = 16
WIN_LEN = 512
MOBA_BLOCK = 256
MOBA_TOPK = 3

LANES = 128
SUBLANES = 8
TILE_W = 2 * LANES
DIL_GROUP = 2048
VMEM_LIMIT = 56 * 1024 * 1024


def _nt_dot(a, b):
    return lax.dot_general(a, b, (((1,), (1,)), ((), ())), preferred_element_type=F32)


def _dot(a, b):
    return jnp.dot(a, b, preferred_element_type=F32)


def _split_bf16(x):
    hi = x.astype(BF16)
    lo = (x - hi.astype(F32)).astype(BF16)
    return hi, lo


def _silu(x):
    return x * jax.nn.sigmoid(x)


def _head_masks(shape):
    lane = lax.broadcasted_iota(jnp.int32, shape, 1)
    return lane < HEAD_DIM


def _pv_pair(vt, p, tq):
    return jnp.concatenate([_dot(vt[:HEAD_DIM], p[:, :tq]), _dot(vt[HEAD_DIM:], p[:, tq:])], axis=1)


def _online_update_pair_t(s, vt, tq, m, l, acc):
    m_new = jnp.maximum(m, jnp.max(s, axis=0, keepdims=True))
    alpha = jnp.exp(m - m_new)
    p = jnp.exp(s - m_new)
    l = alpha * l + jnp.sum(p, axis=0, keepdims=True)
    acc = alpha * acc + _pv_pair(vt, p.astype(BF16), tq)
    return m_new, l, acc


def _mod_kernel(c_ref, w_ref, b_ref, o_ref):
    c = c_ref[...]
    o_ref[...] = jnp.dot(_silu(c), w_ref[...], preferred_element_type=F32,
                         precision=lax.Precision.HIGHEST) + b_ref[...]


def _modulation(c, w_mod, b_mod):
    B, D = c.shape
    N = w_mod.shape[1]
    rows = SUBLANES
    cp = jnp.zeros((rows, D), F32).at[:B].set(c)
    tn = N // 4
    out = pl.pallas_call(
        _mod_kernel,
        grid=(N // tn,),
        in_specs=[pl.BlockSpec((rows, D), lambda j: (0, 0)),
                  pl.BlockSpec((D, tn), lambda j: (0, j)),
                  pl.BlockSpec((1, tn), lambda j: (0, j))],
        out_specs=pl.BlockSpec((rows, tn), lambda j: (0, j)),
        out_shape=jax.ShapeDtypeStruct((rows, N), F32),
        name="modulation",
    )(cp, w_mod, b_mod[None])
    return out[:B]


def _rope_lane_tile(y, cos, sa, sb):
    return y * cos + pltpu.roll(y, LANES - ROPE_HALF, 1) * sa + pltpu.roll(y, ROPE_HALF, 1) * sb


def _proj_kernel(x_ref, mod_ref, g_ref, w_ref, wt_ref, cos_ref, sa_ref, sb_ref,
                 qa_ref, ka_ref, za_ref,
                 qb_ref, kb_ref, vb_ref, zb_ref,
                 qc_ref, kcr_ref, vcr_ref, gc_ref, ks_ref, kw_ref, zc_ref,
                 qd_ref, kd_ref, zd_ref, vat_ref, vdt_ref, vst_ref, vwt_ref):
    D = x_ref.shape[-1]
    x = x_ref[0]
    ms = jnp.mean(x * x, axis=-1, keepdims=True)
    y = x * lax.rsqrt(ms + EPS) * g_ref[...]
    shift = mod_ref[0, :, 0:D]
    scale = mod_ref[0, :, D:2 * D]
    h = (y * (1.0 + scale) + shift).astype(BF16)
    cos, sa, sb = cos_ref[...], sa_ref[...], sb_ref[...]

    def tile(t):
        return _dot(h, w_ref[:, t * TILE_W:(t + 1) * TILE_W])

    def rope_lo(y):
        return _rope_lane_tile(y[:, :LANES], cos, sa, sb)

    def rope_all(y):
        return jnp.concatenate([rope_lo(y), _rope_lane_tile(y[:, LANES:], cos, sa, sb)], axis=1)

    qa_ref[0] = tile(0).astype(BF16)
    ka_ref[0] = tile(1).astype(BF16)
    za_ref[0] = tile(2).astype(BF16)
    qb_ref[0] = rope_all(tile(3))
    kb_ref[0] = rope_all(tile(4))
    vb_ref[0] = tile(5)
    zb_ref[0] = tile(6).astype(BF16)
    qc_ref[0] = rope_all(tile(7)).astype(BF16)
    t8 = tile(8)
    kcr_ref[0] = t8[:, 0:HEAD_DIM]
    vcr_ref[0] = t8[:, HEAD_DIM:LANES]
    gc_ref[0] = t8[:, LANES:]
    t9 = rope_all(tile(9)).astype(BF16)
    ks_ref[0] = t9[:, :LANES]
    kw_ref[0] = t9[:, LANES:]
    zc_ref[0] = tile(10).astype(BF16)
    qd_ref[0] = rope_all(tile(11)).astype(BF16)
    kd_ref[0] = rope_all(tile(12)).astype(BF16)
    zd_ref[0] = tile(13).astype(BF16)
    vt = _nt_dot(wt_ref[...], h).astype(BF16)
    vat_ref[0] = vt[0:MIX_W]
    vdt_ref[0] = vt[MIX_W:2 * MIX_W]
    vst_ref[0] = vt[2 * MIX_W:2 * MIX_W + HEAD_DIM]
    vwt_ref[0] = vt[2 * MIX_W + HEAD_DIM:]


def _pack_w_in(w_in):
    D = w_in.shape[0]
    offs = np.cumsum([0] + [MIX_W] * 9 + [HEAD_DIM] * 6 + [3 * N_HEADS] + [MIX_W] * 5)
    seg = lambda i: w_in[:, offs[i]:offs[i + 1]]
    zeros = lambda n: jnp.zeros((D, n), w_in.dtype)
    parts = [seg(0), seg(1), seg(3), seg(4), seg(5), seg(6), seg(7), seg(8),
             seg(9), seg(10), seg(15), zeros(LANES - 3 * N_HEADS),
             seg(11), seg(11), seg(13), seg(13),
             seg(16), seg(17), seg(18), seg(20)]
    w_cat = jnp.concatenate(parts, axis=1).astype(BF16)
    w_t = jnp.concatenate([seg(2), seg(19), seg(12), seg(14)], axis=1).T.astype(BF16)
    return w_cat, w_t


def _rope_angles(pos):
    inv_freq = 1.0 / (ROPE_THETA ** (np.arange(0, ROPE_DIM, 2, dtype=np.float32) / ROPE_DIM))
    ang = jnp.asarray(pos, F32)[:, None] * jnp.asarray(inv_freq, F32)[None, :]
    return jnp.cos(ang), jnp.sin(ang)


def _rope_lane_tables(pos):
    cos, sin = _rope_angles(pos)
    n = cos.shape[0]
    rest = HEAD_DIM - ROPE_DIM
    cos_h = jnp.concatenate([cos, cos, jnp.ones((n, rest), F32)], axis=1)
    sa_h = jnp.concatenate([-sin, jnp.zeros((n, HEAD_DIM - ROPE_HALF), F32)], axis=1)
    sb_h = jnp.concatenate([jnp.zeros((n, ROPE_HALF), F32), sin, jnp.zeros((n, rest), F32)], axis=1)
    two = lambda t: jnp.concatenate([t, t], axis=1)
    return two(cos_h), two(sa_h), two(sb_h)


def _projection(x, mod, g_pre, w_cat, w_t, tables, tm):
    B, S, D = x.shape
    cos, sa, sb = tables
    row = lambda w: pl.BlockSpec((1, tm, w), lambda b, i: (b, i, 0))
    tab = pl.BlockSpec((tm, LANES), lambda b, i: (i, 0))
    const = lambda a: pl.BlockSpec(a.shape, lambda b, i: (0,) * a.ndim)
    widths = [(MIX_W, BF16)] * 3 + [(MIX_W, F32)] * 3 + [(MIX_W, BF16)] + \
             [(MIX_W, BF16), (HEAD_DIM, F32), (HEAD_DIM, F32), (LANES, F32),
              (LANES, BF16), (LANES, BF16), (MIX_W, BF16)] + \
             [(MIX_W, BF16)] * 3
    t_rows = (MIX_W, MIX_W, HEAD_DIM, HEAD_DIM)
    t_specs = [pl.BlockSpec((1, r, tm), lambda b, i: (b, 0, i)) for r in t_rows]
    t_shapes = [jax.ShapeDtypeStruct((B, r, S), BF16) for r in t_rows]
    return pl.pallas_call(
        _proj_kernel,
        grid=(B, S // tm),
        in_specs=[row(D),
                  pl.BlockSpec((1, 1, 3 * D), lambda b, i: (b, 0, 0)),
                  pl.BlockSpec((1, D), lambda b, i: (0, 0)),
                  const(w_cat), const(w_t), tab, tab, tab],
        out_specs=[row(w) for w, _ in widths] + t_specs,
        out_shape=[jax.ShapeDtypeStruct((B, S, w), dt) for w, dt in widths] + t_shapes,
        compiler_params=pltpu.CompilerParams(
            dimension_semantics=("parallel", "parallel"), vmem_limit_bytes=VMEM_LIMIT),
        name="in_projection",
    )(x, mod[:, None, :], g_pre[None], w_cat, w_t, cos, sa, sb)


def _pair_specs(S, tq):
    qspec = pl.BlockSpec((1, tq, LANES), lambda b, p, i: (b, i, p))
    kspec = pl.BlockSpec((1, S, LANES), lambda b, p, i: (b, 0, p))
    vtspec = pl.BlockSpec((1, LANES, S), lambda b, p, i: (b, p, 0))
    return qspec, kspec, vtspec


def _tile_rows(ref, j, tk):
    return ref[0, pl.ds(pl.multiple_of(j * tk, tk), tk), :]


def _tile_rows_2d(ref, j, tk):
    return ref[pl.ds(pl.multiple_of(j * tk, tk), tk), :]


def _tile_lanes(ref, j, tk):
    return ref[0, :, pl.ds(pl.multiple_of(j * tk, tk), tk)]


def _stack_pair(q, is_a):
    return jnp.concatenate([jnp.where(is_a, q, 0), jnp.where(is_a, 0, q)], axis=0)


def _pipelined(n, first, produce, consume, state):
    def body(t, carry):
        blk, st = carry
        return produce(t), consume(blk, st)
    blk, st = lax.fori_loop(0, n, body, (first, state))
    return consume(blk, st)


def _finish_pair(acc, tq, z_ref, o_ref):
    out = jnp.concatenate([acc[:, :tq], acc[:, tq:]], axis=0).T
    o_ref[0] = (out * _silu(z_ref[0].astype(F32))).astype(BF16)


def _stick_kernel(q_ref, k_ref, vt_ref, z_ref, o_ref, *, tq):
    i = pl.program_id(2)
    tk = tq
    is_a = _head_masks((tq, LANES))
    qs = (q_ref[0].astype(F32) * SCALE).astype(BF16)
    qcat = _stack_pair(qs, is_a)
    r_i = lax.broadcasted_iota(jnp.int32, (tk, tk), 0)
    c_i = lax.broadcasted_iota(jnp.int32, (tk, tk), 1)
    suffix = jnp.where(c_i >= r_i, 1.0, 0.0).astype(BF16)
    past1 = lax.broadcasted_iota(jnp.int32, (tk, tq), 0) < lax.broadcasted_iota(jnp.int32, (tk, tq), 1)
    past = jnp.concatenate([past1, past1], axis=1)

    def log_weights(j, diag):
        z = _nt_dot(_tile_rows(k_ref, j, tk), qcat)
        sp = jnp.maximum(z, 0.0) + jnp.log1p(jnp.exp(-jnp.abs(z)))
        if diag:
            sp = jnp.where(past, sp, 0.0)
        hi, lo = _split_bf16(sp)
        cs = _dot(suffix, hi) + _dot(suffix, lo)
        x = z - cs
        if diag:
            x = jnp.where(past, x, NEG)
        return j, x, cs[0:1, :]

    def accumulate(blk, st):
        j, x, tot = blk
        acc, car = st
        a = jnp.exp(x - car).astype(BF16)
        return acc + _pv_pair(_tile_lanes(vt_ref, j, tk), a, tq), car + tot

    init = (jnp.zeros((HEAD_DIM, 2 * tq), F32), jnp.zeros((1, 2 * tq), F32))
    acc, _ = _pipelined(i, log_weights(i, True), lambda t: log_weights(i - 1 - t, False),
                        accumulate, init)
    _finish_pair(acc, tq, z_ref, o_ref)


def _stick_breaking(q, k, vt, z, tq=256):
    B, S, _ = q.shape
    qspec, kspec, vtspec = _pair_specs(S, tq)
    return pl.pallas_call(
        functools.partial(_stick_kernel, tq=tq),
        grid=(B, 2, S // tq),
        in_specs=[qspec, kspec, vtspec, qspec],
        out_specs=qspec,
        out_shape=jax.ShapeDtypeStruct((B, S, MIX_W), BF16),
        compiler_params=pltpu.CompilerParams(
            dimension_semantics=("parallel", "parallel", "arbitrary"), vmem_limit_bytes=VMEM_LIMIT),
        name="stick_breaking",
    )(q, k, vt, z)


def _dil_kernel(q_ref, k_ref, v_ref, z_ref, o_ref, m_scr, l_scr, a_scr, *, G):
    g = pl.program_id(2)
    blk = LANES
    is_a = _head_masks((blk, LANES))
    row = lax.broadcasted_iota(jnp.int32, (blk, blk), 0)
    col = lax.broadcasted_iota(jnp.int32, (blk, blk), 1)
    n_sub = G // blk

    for ci, (_, d) in enumerate(DIL_CONFIGS):
        nb = G // (blk * d)
        sh = int(np.log2(nb))

        def sub(n, _, ci=ci, d=d, nb=nb, sh=sh):
            r = lax.shift_right_logical(n, sh)
            ub = jnp.bitwise_and(n, nb - 1)
            loc = ub * (blk * d) + r
            glob = g * G + loc
            has_prev = glob >= blk * d
            pstart = jnp.maximum(glob - blk * d, r)
            qs = (q_ref[0, pl.ds(loc, blk, stride=d), :] * SCALE).astype(BF16)
            kc = k_ref[0, pl.ds(glob, blk, stride=d), :].astype(BF16)
            vc = v_ref[0, pl.ds(glob, blk, stride=d), :].astype(BF16)
            kp = k_ref[0, pl.ds(pstart, blk, stride=d), :].astype(BF16)
            vp = v_ref[0, pl.ds(pstart, blk, stride=d), :].astype(BF16)
            cur_ok = row >= col
            prev_ok = jnp.logical_and(col >= row, has_prev)
            res = []
            for h in range(2):
                qh = jnp.where(is_a, qs, 0) if h == 0 else jnp.where(is_a, 0, qs)
                s_c = jnp.where(cur_ok, _nt_dot(qh, kc), NEG)
                s_p = jnp.where(prev_ok, _nt_dot(qh, kp), NEG)
                m = jnp.maximum(jnp.max(s_c, axis=-1, keepdims=True),
                                jnp.max(s_p, axis=-1, keepdims=True))
                p_c = jnp.exp(s_c - m)
                p_p = jnp.exp(s_p - m)
                l = jnp.sum(p_c, axis=-1, keepdims=True) + jnp.sum(p_p, axis=-1, keepdims=True)
                acc = _dot(p_c.astype(BF16), vc) + _dot(p_p.astype(BF16), vp)
                res.append((m, l, acc))
            rows = pl.ds(loc, blk, stride=d)
            m_scr[ci, rows, :] = jnp.where(is_a, res[0][0], res[1][0])
            l_scr[ci, rows, :] = jnp.where(is_a, res[0][1], res[1][1])
            a_scr[ci, rows, :] = jnp.where(is_a, res[0][2], res[1][2])
            return 0

        lax.fori_loop(0, n_sub, sub, 0)

    chunk = 256

    def combine(c, _):
        rows = pl.ds(pl.multiple_of(c * chunk, chunk), chunk)
        m0, m1, m2 = m_scr[0, rows, :], m_scr[1, rows, :], m_scr[2, rows, :]
        mm = jnp.maximum(jnp.maximum(m0, m1), m2)
        w0, w1, w2 = jnp.exp(m0 - mm), jnp.exp(m1 - mm), jnp.exp(m2 - mm)
        num = w0 * a_scr[0, rows, :] + w1 * a_scr[1, rows, :] + w2 * a_scr[2, rows, :]
        den = w0 * l_scr[0, rows, :] + w1 * l_scr[1, rows, :] + w2 * l_scr[2, rows, :]
        o_ref[0, rows, :] = (num / den * _silu(z_ref[0, rows, :].astype(F32))).astype(BF16)
        return 0

    lax.fori_loop(0, G // chunk, combine, 0)


def _dilated(q, k, v, z):
    B, S, _ = q.shape
    G = min(DIL_GROUP, S)
    gspec = pl.BlockSpec((1, G, LANES), lambda b, p, g: (b, g, p))
    kspec = pl.BlockSpec((1, S, LANES), lambda b, p, g: (b, 0, p))
    scr = pltpu.VMEM((len(DIL_CONFIGS), G, LANES), F32)
    return pl.pallas_call(
        functools.partial(_dil_kernel, G=G),
        grid=(B, 2, S // G),
        in_specs=[gspec, kspec, kspec, gspec],
        out_specs=gspec,
        out_shape=jax.ShapeDtypeStruct((B, S, MIX_W), BF16),
        scratch_shapes=[scr, scr, scr],
        compiler_params=pltpu.CompilerParams(
            dimension_semantics=("parallel", "parallel", "arbitrary"), vmem_limit_bytes=VMEM_LIMIT),
        name="dilated_window",
    )(q, k, v, z)


def _compress_kernel(k_ref, v_ref, p_ref, w1_ref, b1_ref, w2k_ref, w2vt_ref, cos_ref, sa_ref, sb_ref,
                     kc_ref, vc_ref):
    n = kc_ref.shape[1]
    dh = k_ref.shape[-1]
    for s, (t_ref, o_ref) in enumerate(((k_ref, kc_ref), (v_ref, vc_ref))):
        lo = jnp.zeros((n, CMP_HID), F32)
        hi = jnp.zeros((n, CMP_HID), F32)
        for l in range(CMP_STRIDE):
            t = t_ref[0, pl.ds(l, n, stride=CMP_STRIDE), :]
            lo = lo + _dot((t + p_ref[s, l:l + 1, :]).astype(BF16),
                           w1_ref[s, l * dh:(l + 1) * dh, :])
            lh = l + CMP_STRIDE
            hi = hi + _dot((t + p_ref[s, lh:lh + 1, :]).astype(BF16),
                           w1_ref[s, lh * dh:(lh + 1) * dh, :])
        pre = lo + pltpu.roll(hi, n - 1, 0) + b1_ref[s]
        hid = 0.5 * pre * (1.0 + jnp.tanh(np.sqrt(2.0 / np.pi).astype(np.float32)
                                          * (pre + 0.044715 * (pre * pre * pre))))
        hid = hid.astype(BF16)
        if s == 0:
            out = _dot(hid, w2k_ref[...])
            o_ref[0] = _rope_lane_tile(out, cos_ref[...], sa_ref[...], sb_ref[...]).astype(BF16)
        else:
            o_ref[0] = _nt_dot(w2vt_ref[...], hid).astype(BF16)


def _compress(kcr, vcr, cmp_pos, cmp_w1, cmp_b1, cmp_w2, tables_c):
    B, S, dh = kcr.shape
    n = S // CMP_STRIDE
    w1 = cmp_w1.astype(BF16)
    w2k = jnp.concatenate([cmp_w2[0], cmp_w2[0]], axis=-1).astype(BF16)
    w2vt = cmp_w2[1].T.astype(BF16)
    b1 = cmp_b1[:, None, :]
    cos, sa, sb = tables_c
    const = lambda a: pl.BlockSpec(a.shape, lambda b: (0,) * a.ndim)
    tspec = pl.BlockSpec((1, S, dh), lambda b: (b, 0, 0))
    return pl.pallas_call(
        _compress_kernel,
        grid=(B,),
        in_specs=[tspec, tspec, const(cmp_pos), const(w1), const(b1), const(w2k), const(w2vt),
                  const(cos), const(sa), const(sb)],
        out_specs=[pl.BlockSpec((1, n, LANES), lambda b: (b, 0, 0)),
                   pl.BlockSpec((1, dh, n), lambda b: (b, 0, 0))],
        out_shape=[jax.ShapeDtypeStruct((B, n, LANES), BF16),
                   jax.ShapeDtypeStruct((B, dh, n), BF16)],
        compiler_params=pltpu.CompilerParams(
            dimension_semantics=("parallel",), vmem_limit_bytes=VMEM_LIMIT),
        name="compress_tokens",
    )(kcr, vcr, cmp_pos, w1, b1, w2k, w2vt, cos, sa, sb)


def _nsa_kernel(q_ref, kc_ref, vct_ref, ks_ref, vst_ref, kw_ref, vwt_ref, gc_ref, z_ref,
                ovt_ref, ext_ref, o_ref, *, tq, tk, n_sel):
    i = pl.program_id(1)
    Q = N_HEADS * tq
    is_a = _head_masks((tq, LANES))
    qs = (q_ref[0].astype(F32) * SCALE).astype(BF16)
    q4 = jnp.concatenate([_stack_pair(qs[:, :LANES], is_a), _stack_pair(qs[:, LANES:], is_a)], axis=0)
    heads = lambda x: jnp.concatenate([x] * N_HEADS, axis=1)
    qpos1 = i * tq + lax.broadcasted_iota(jnp.int32, (1, tq), 1)
    qpos = heads(qpos1)

    n_c = kc_ref.shape[1]
    c_end = lax.broadcasted_iota(jnp.int32, (n_c, Q), 0) * CMP_STRIDE + (CMP_LEN - 1)
    c_valid = c_end <= qpos
    sc = jnp.where(c_valid, _nt_dot(kc_ref[0], q4), NEG)
    pc = jnp.exp(sc - jnp.max(sc, axis=0, keepdims=True)) * jnp.where(c_valid, 1.0, 0.0)
    lc = jnp.sum(pc, axis=0, keepdims=True)
    pc = pc * jnp.where(lc > 0.0, 1.0 / lc, 0.0)
    o_cmp = _dot(vct_ref[0], pc.astype(BF16))

    pcs = pc[:, 0:tq] + pc[:, tq:2 * tq] + pc[:, 2 * tq:3 * tq] + pc[:, 3 * tq:]
    hi, lo = _split_bf16(pcs)
    imp = _dot(ovt_ref[...], hi) + _dot(ovt_ref[...], lo)
    nsp = imp.shape[0]
    blk = lax.broadcasted_iota(jnp.int32, (nsp, tq), 0)
    own = lax.shift_right_logical(qpos1, int(np.log2(SEL_LEN)))
    s_valid = blk <= own
    forced = jnp.logical_or(blk == 0, blk >= own - 1)
    imp = jnp.where(s_valid, jnp.where(forced, BIG, imp), NEG)
    rank = jnp.zeros((nsp, tq), F32)
    for s in range(n_sel):
        c = imp[s:s + 1, :]
        beats = jnp.logical_or(c > imp, jnp.logical_and(c == imp, blk > s))
        rank = rank + jnp.where(beats, 1.0, 0.0)
    chosen = jnp.logical_and(rank < float(min(SEL_TOPN, n_sel)), s_valid)
    sel = jnp.where(chosen, 1.0, 0.0).astype(BF16)

    kpos0 = lax.broadcasted_iota(jnp.int32, (tk, Q), 0)
    J = (i * tq) // tk

    def sel_scores(j):
        bias = (_dot(_tile_rows_2d(ext_ref, j, tk), sel) - 1.0) * (-NEG)
        return _nt_dot(_tile_rows(ks_ref, j, tk), q4) + heads(bias)

    def consume(vt_ref):
        def fn(blk_s, st):
            j, s = blk_s
            m, l, acc = st
            m_new = jnp.maximum(m, jnp.max(s, axis=0, keepdims=True))
            alpha = jnp.exp(m - m_new)
            p = jnp.exp(s - m_new)
            l = alpha * l + jnp.sum(p, axis=0, keepdims=True)
            acc = alpha * acc + _dot(_tile_lanes(vt_ref, j, tk), p.astype(BF16))
            return m_new, l, acc
        return fn

    init = (jnp.full((1, Q), NEG, F32), jnp.zeros((1, Q), F32), jnp.zeros((HEAD_DIM, Q), F32))

    first = (J, jnp.where(J * tk + kpos0 <= qpos, sel_scores(J), NEG))
    _, l_s, a_s = _pipelined(J, first, lambda t: (t, sel_scores(t)), consume(vst_ref), init)
    o_sel = a_s / l_s

    def win_scores(j):
        dist = qpos - (j * tk + kpos0)
        ok = jnp.logical_and(dist >= 0, dist < WIN_LEN)
        return j, jnp.where(ok, _nt_dot(_tile_rows(kw_ref, j, tk), q4), NEG)

    n_back = jnp.minimum(J, -(-WIN_LEN // tk))
    _, l_w, a_w = _pipelined(n_back, win_scores(J), lambda t: win_scores(J - 1 - t),
                             consume(vwt_ref), init)
    o_win = a_w / l_w

    gates = jax.nn.sigmoid(gc_ref[0]).T
    outs = []
    for h in range(N_HEADS):
        cols = slice(h * tq, (h + 1) * tq)
        outs.append(gates[3 * h:3 * h + 1, :] * o_cmp[:, cols]
                    + gates[3 * h + 1:3 * h + 2, :] * o_sel[:, cols]
                    + gates[3 * h + 2:3 * h + 3, :] * o_win[:, cols])
    out = jnp.concatenate(outs, axis=0).T
    o_ref[0] = (out * _silu(z_ref[0].astype(F32))).astype(BF16)


def _nsa(q, kc2, vct, ks2, vst, kw2, vwt, gc, z, tq=128, tk=256):
    B, S, _ = q.shape
    n_c = kc2.shape[1]
    n_sel = S // SEL_LEN
    nsp = -(-n_sel // SUBLANES) * SUBLANES
    n_cmp = (S - CMP_LEN) // CMP_STRIDE + 1
    c_start = np.arange(n_c) * CMP_STRIDE
    s_start = np.arange(nsp) * SEL_LEN
    overlap_t = np.clip(np.minimum(c_start[None, :] + CMP_LEN, s_start[:, None] + SEL_LEN)
                        - np.maximum(c_start[None, :], s_start[:, None]), 0, None) / CMP_LEN
    overlap_t[:, n_cmp:] = 0.0
    overlap_t[n_sel:, :] = 0.0
    expand_t = (np.arange(S)[:, None] // SEL_LEN == np.arange(nsp)[None, :]).astype(np.float32)
    row = lambda w: pl.BlockSpec((1, tq, w), lambda b, i: (b, i, 0))
    full = lambda a: pl.BlockSpec((1,) + a.shape[1:], lambda b, i: (b, 0, 0))
    const = lambda a: pl.BlockSpec(a.shape, lambda b, i: (0, 0))
    ovt = jnp.asarray(overlap_t, BF16)
    ext = jnp.asarray(expand_t, BF16)
    return pl.pallas_call(
        functools.partial(_nsa_kernel, tq=tq, tk=tk, n_sel=n_sel),
        grid=(B, S // tq),
        in_specs=[row(MIX_W), full(kc2), full(vct), full(ks2), full(vst), full(kw2), full(vwt),
                  row(LANES), row(MIX_W), const(ovt), const(ext)],
        out_specs=row(MIX_W),
        out_shape=jax.ShapeDtypeStruct((B, S, MIX_W), BF16),
        compiler_params=pltpu.CompilerParams(
            dimension_semantics=("parallel", "parallel"), vmem_limit_bytes=VMEM_LIMIT),
        name="native_sparse",
    )(q, kc2, vct, ks2, vst, kw2, vwt, gc, z, ovt, ext)


def _moba_kernel(q_ref, k_ref, vt_ref, z_ref, o_ref, km_scr, bias_scr, *, n_blk):
    i = pl.program_id(2)
    tq = tk = MOBA_BLOCK
    nb_pad = km_scr.shape[0]
    is_a = _head_masks((tq, LANES))

    @pl.when(i == 0)
    def _():
        km_scr[...] = jnp.zeros_like(km_scr)
        for b in range(n_blk):
            kb = k_ref[0, b * tk:(b + 1) * tk, :].astype(F32)
            km_scr[b:b + 1, :] = jnp.mean(kb, axis=0, keepdims=True)

    q = q_ref[0]
    qcat = _stack_pair((q.astype(F32) * SCALE).astype(BF16), is_a)
    qgcat = _stack_pair(q, is_a)
    km_hi, km_lo = _split_bf16(km_scr[...])
    blk = lax.broadcasted_iota(jnp.int32, (nb_pad, 2 * tq), 0)
    n_top = min(MOBA_TOPK, max(n_blk - 1, 1))
    gsc = _nt_dot(km_hi, qgcat) + _nt_dot(km_lo, qgcat)
    gsc = jnp.where(blk < i, gsc, NEG)
    rank = jnp.zeros((nb_pad, 2 * tq), F32)
    for s in range(n_blk):
        c = gsc[s:s + 1, :]
        beats = jnp.logical_or(c > gsc, jnp.logical_and(c == gsc, blk > s))
        rank = rank + jnp.where(beats, 1.0, 0.0)
    chosen = jnp.logical_and(rank < float(n_top), blk < i)
    bias_scr[...] = jnp.where(chosen, 0.0, NEG)

    causal1 = lax.broadcasted_iota(jnp.int32, (tk, tq), 0) <= lax.broadcasted_iota(jnp.int32, (tk, tq), 1)
    causal = jnp.concatenate([causal1, causal1], axis=1)

    def scores(j):
        return _nt_dot(_tile_rows(k_ref, j, tk), qcat)

    def consume(blk_s, st):
        j, s = blk_s
        return _online_update_pair_t(s, _tile_lanes(vt_ref, j, tk), tq, *st)

    first = (i, jnp.where(causal, scores(i), NEG))
    init = (jnp.full((1, 2 * tq), NEG, F32), jnp.zeros((1, 2 * tq), F32),
            jnp.zeros((HEAD_DIM, 2 * tq), F32))
    _, l, acc = _pipelined(i, first, lambda t: (t, scores(t) + bias_scr[pl.ds(t, 1), :]),
                           consume, init)
    _finish_pair(acc / l, tq, z_ref, o_ref)


def _moba(q, k, vt, z):
    B, S, _ = q.shape
    tq = MOBA_BLOCK
    n_blk = S // MOBA_BLOCK
    nb_pad = -(-n_blk // SUBLANES) * SUBLANES
    qspec, kspec, vtspec = _pair_specs(S, tq)
    return pl.pallas_call(
        functools.partial(_moba_kernel, n_blk=n_blk),
        grid=(B, 2, S // tq),
        in_specs=[qspec, kspec, vtspec, qspec],
        out_specs=qspec,
        out_shape=jax.ShapeDtypeStruct((B, S, MIX_W), BF16),
        scratch_shapes=[pltpu.VMEM((nb_pad, LANES), F32),
                        pltpu.VMEM((nb_pad, 2 * tq), F32)],
        compiler_params=pltpu.CompilerParams(
            dimension_semantics=("parallel", "parallel", "arbitrary"), vmem_limit_bytes=VMEM_LIMIT),
        name="moba",
    )(q, k, vt, z)


def _out_kernel(ma_ref, mb_ref, mc_ref, md_ref, w_ref, x_ref, mod_ref, g_ref, o_ref):
    D = x_ref.shape[-1]
    acc = _dot(ma_ref[0], w_ref[0:MIX_W, :])
    acc = acc + _dot(mb_ref[0], w_ref[MIX_W:2 * MIX_W, :])
    acc = acc + _dot(mc_ref[0], w_ref[2 * MIX_W:3 * MIX_W, :])
    acc = acc + _dot(md_ref[0], w_ref[3 * MIX_W:, :])
    ms = jnp.mean(acc * acc, axis=-1, keepdims=True)
    y = acc * lax.rsqrt(ms + EPS) * g_ref[...]
    o_ref[0] = x_ref[0] + mod_ref[0, :, 2 * D:] * y


def _out_projection(mixed, w_out, x, mod, g_post, tm):
    B, S, D = x.shape
    row = lambda w: pl.BlockSpec((1, tm, w), lambda b, i: (b, i, 0))
    return pl.pallas_call(
        _out_kernel,
        grid=(B, S // tm),
        in_specs=[row(MIX_W)] * 4 + [pl.BlockSpec(w_out.shape, lambda b, i: (0, 0)), row(D),
                                     pl.BlockSpec((1, 1, 3 * D), lambda b, i: (b, 0, 0)),
                                     pl.BlockSpec((1, D), lambda b, i: (0, 0))],
        out_specs=row(D),
        out_shape=jax.ShapeDtypeStruct((B, S, D), F32),
        compiler_params=pltpu.CompilerParams(
            dimension_semantics=("parallel", "parallel"), vmem_limit_bytes=VMEM_LIMIT),
        name="out_projection",
    )(*mixed, w_out.astype(BF16), x, mod[:, None, :], g_post[None])


def _layer(x, c, norm_pre, norm_post, w_mod, b_mod, w_in, w_out, cmp_pos, cmp_w1, cmp_b1, cmp_w2,
           tables, tables_c, tm):
    mod = _modulation(c, w_mod, b_mod)
    w_cat, w_t = _pack_w_in(w_in)
    (qa, ka, za, qb, kb, vb, zb, qc, kcr, vcr, gc, ks2, kw2, zc,
     qd, kd, zd, vat, vdt, vst, vwt) = _projection(x, mod, norm_pre, w_cat, w_t, tables, tm)
    oa = _stick_breaking(qa, ka, vat, za)
    ob = _dilated(qb, kb, vb, zb)
    kc2, vct = _compress(kcr, vcr, cmp_pos, cmp_w1, cmp_b1, cmp_w2, tables_c)
    oc = _nsa(qc, kc2, vct, ks2, vst, kw2, vwt, gc, zc)
    od = _moba(qd, kd, vdt, zd)
    return _out_projection((oa, ob, oc, od), w_out, x, mod, norm_post, tm)


def kernel(x, c, norm_pre, norm_post, w_mod, b_mod, w_in, w_out, cmp_pos, cmp_w1, cmp_b1, cmp_w2):
    S = x.shape[1]
    tables = _rope_lane_tables(np.arange(S))
    tables_c = _rope_lane_tables(np.arange(S // CMP_STRIDE) * CMP_STRIDE + CMP_LEN - 1)
    tm = min(512, S)
    for l in range(norm_pre.shape[0]):
        x = _layer(x, c, norm_pre[l], norm_post[l], w_mod[l], b_mod[l], w_in[l], w_out[l],
                   cmp_pos[l], cmp_w1[l], cmp_b1[l], cmp_w2[l], tables, tables_c, tm)
    return x
```

```python
import functools

import numpy as np
import jax
import jax.numpy as jnp
from jax import lax
from jax.experimental import pallas as pl
from jax.experimental.pallas import tpu as pltpu

F32 = jnp.float32
BF16 = jnp.bfloat16

N_HEADS = 4
HEAD_DIM = 64
MIX_W = N_HEADS * HEAD_DIM
ROPE_THETA = 500000.0
ROPE_DIM = HEAD_DIM // 4
ROPE_HALF = ROPE_DIM // 2
EPS = 1e-6
NEG = -1e30
BIG = 1e9
SCALE = HEAD_DIM ** -0.5
LOG2E = 1.4426950408889634
DIL_CONFIGS = ((128, 1), (512, 4), (2048, 16))
CMP_LEN = 32
CMP_STRIDE = 16
CMP_HID = 256
SEL_LEN = 64
SEL_TOPN = 16
WIN_LEN = 512
MOBA_BLOCK = 256
MOBA_TOPK = 3

N_PAIRS = N_HEADS // 2
LANES = 128
SUBLANES = 8
TILE_W = 2 * LANES
DIL_GROUP = 2048
VMEM_LIMIT = 56 * 1024 * 1024


def _nt_dot(a, b):
    return lax.dot_general(a, b, (((1,), (1,)), ((), ())), preferred_element_type=F32)


def _dot(a, b):
    return jnp.dot(a, b, preferred_element_type=F32)


def _split_bf16(x):
    hi = x.astype(BF16)
    lo = (x - hi.astype(F32)).astype(BF16)
    return hi, lo


def _silu(x):
    return x * jax.nn.sigmoid(x)


def _head_masks(shape):
    lane = lax.broadcasted_iota(jnp.int32, shape, 1)
    return lane < HEAD_DIM


def _pv_pair(vt, p, tq):
    return jnp.concatenate([_dot(vt[:HEAD_DIM], p[:, :tq]), _dot(vt[HEAD_DIM:], p[:, tq:])], axis=1)


def _mod_kernel(c_ref, w_ref, b_ref, o_ref):
    c = c_ref[...]
    o_ref[...] = jnp.dot(_silu(c), w_ref[...], preferred_element_type=F32,
                         precision=lax.Precision.HIGHEST) + b_ref[...]


def _modulation(c, w_mod, b_mod):
    B, D = c.shape
    N = w_mod.shape[1]
    rows = SUBLANES
    cp = jnp.zeros((rows, D), F32).at[:B].set(c)
    tn = N // 4
    out = pl.pallas_call(
        _mod_kernel,
        grid=(N // tn,),
        in_specs=[pl.BlockSpec((rows, D), lambda j: (0, 0)),
                  pl.BlockSpec((D, tn), lambda j: (0, j)),
                  pl.BlockSpec((1, tn), lambda j: (0, j))],
        out_specs=pl.BlockSpec((rows, tn), lambda j: (0, j)),
        out_shape=jax.ShapeDtypeStruct((rows, N), F32),
        name="modulation",
    )(cp, w_mod, b_mod[None])
    return out[:B]


def _rope_lane_tile(y, cos, sa, sb):
    return y * cos + pltpu.roll(y, LANES - ROPE_HALF, 1) * sa + pltpu.roll(y, ROPE_HALF, 1) * sb


def _proj_kernel(x_ref, mod_ref, g_ref, w_ref, wt_ref, cos_ref, sa_ref, sb_ref,
                 qa_ref, ka_ref, za_ref,
                 qb_ref, kb_ref, vb_ref, zb_ref,
                 qc_ref, kcr_ref, vcr_ref, gc_ref, ks_ref, kw_ref, zc_ref,
                 qd_ref, kd_ref, zd_ref, vat_ref, vdt_ref, vst_ref, vwt_ref):
    D = x_ref.shape[-1]
    x = x_ref[0]
    ms = jnp.mean(x * x, axis=-1, keepdims=True)
    y = x * lax.rsqrt(ms + EPS) * g_ref[...]
    shift = mod_ref[0, :, 0:D]
    scale = mod_ref[0, :, D:2 * D]
    h = (y * (1.0 + scale) + shift).astype(BF16)
    cos, sa, sb = cos_ref[...], sa_ref[...], sb_ref[...]

    def tile(t):
        return _dot(h, w_ref[:, t * TILE_W:(t + 1) * TILE_W])

    def rope_lo(y):
        return _rope_lane_tile(y[:, :LANES], cos, sa, sb)

    def rope_all(y):
        return jnp.concatenate([rope_lo(y), _rope_lane_tile(y[:, LANES:], cos, sa, sb)], axis=1)

    qa_ref[0] = tile(0).astype(BF16)
    ka_ref[0] = tile(1).astype(BF16)
    za_ref[0] = tile(2).astype(BF16)
    qb_ref[0] = rope_all(tile(3))
    kb_ref[0] = rope_all(tile(4))
    vb_ref[0] = tile(5)
    zb_ref[0] = tile(6).astype(BF16)
    qc_ref[0] = rope_all(tile(7)).astype(BF16)
    t8 = tile(8)
    kcr_ref[0] = t8[:, 0:HEAD_DIM]
    vcr_ref[0] = t8[:, HEAD_DIM:LANES]
    gc_ref[0] = t8[:, LANES:]
    t9 = rope_all(tile(9)).astype(BF16)
    ks_ref[0] = t9[:, :LANES]
    kw_ref[0] = t9[:, LANES:]
    zc_ref[0] = tile(10).astype(BF16)
    qd_ref[0] = rope_all(tile(11)).astype(BF16)
    kd_ref[0] = rope_all(tile(12)).astype(BF16)
    zd_ref[0] = tile(13).astype(BF16)
    vt = _nt_dot(wt_ref[...], h).astype(BF16)
    vat_ref[0] = vt[0:MIX_W]
    vdt_ref[0] = vt[MIX_W:2 * MIX_W]
    vst_ref[0] = vt[2 * MIX_W:2 * MIX_W + HEAD_DIM]
    vwt_ref[0] = vt[2 * MIX_W + HEAD_DIM:]


def _pack_w_in(w_in):
    D = w_in.shape[0]
    offs = np.cumsum([0] + [MIX_W] * 9 + [HEAD_DIM] * 6 + [3 * N_HEADS] + [MIX_W] * 5)
    seg = lambda i: w_in[:, offs[i]:offs[i + 1]]
    zeros = lambda n: jnp.zeros((D, n), w_in.dtype)
    parts = [seg(0), seg(1), seg(3), seg(4), seg(5), seg(6), seg(7), seg(8),
             seg(9), seg(10), seg(15), zeros(LANES - 3 * N_HEADS),
             seg(11), seg(11), seg(13), seg(13),
             seg(16), seg(17), seg(18), seg(20)]
    w_cat = jnp.concatenate(parts, axis=1).astype(BF16)
    w_t = jnp.concatenate([seg(2), seg(19), seg(12), seg(14)], axis=1).T.astype(BF16)
    return w_cat, w_t


def _rope_angles(pos):
    inv_freq = 1.0 / (ROPE_THETA ** (np.arange(0, ROPE_DIM, 2, dtype=np.float32) / ROPE_DIM))
    ang = jnp.asarray(pos, F32)[:, None] * jnp.asarray(inv_freq, F32)[None, :]
    return jnp.cos(ang), jnp.sin(ang)


def _rope_lane_tables(pos):
    cos, sin = _rope_angles(pos)
    n = cos.shape[0]
    rest = HEAD_DIM - ROPE_DIM
    cos_h = jnp.concatenate([cos, cos, jnp.ones((n, rest), F32)], axis=1)
    sa_h = jnp.concatenate([-sin, jnp.zeros((n, HEAD_DIM - ROPE_HALF), F32)], axis=1)
    sb_h = jnp.concatenate([jnp.zeros((n, ROPE_HALF), F32), sin, jnp.zeros((n, rest), F32)], axis=1)
    two = lambda t: jnp.concatenate([t, t], axis=1)
    return two(cos_h), two(sa_h), two(sb_h)


def _projection(x, mod, g_pre, w_cat, w_t, tables, tm):
    B, S, D = x.shape
    cos, sa, sb = tables
    row = lambda w: pl.BlockSpec((1, tm, w), lambda b, i: (b, i, 0))
    tab = pl.BlockSpec((tm, LANES), lambda b, i: (i, 0))
    const = lambda a: pl.BlockSpec(a.shape, lambda b, i: (0,) * a.ndim)
    widths = [(MIX_W, BF16)] * 3 + [(MIX_W, F32)] * 3 + [(MIX_W, BF16)] + \
             [(MIX_W, BF16), (HEAD_DIM, F32), (HEAD_DIM, F32), (LANES, F32),
              (LANES, BF16), (LANES, BF16), (MIX_W, BF16)] + \
             [(MIX_W, BF16)] * 3
    t_rows = (MIX_W, MIX_W, HEAD_DIM, HEAD_DIM)
    t_specs = [pl.BlockSpec((1, r, tm), lambda b, i: (b, 0, i)) for r in t_rows]
    t_shapes = [jax.ShapeDtypeStruct((B, r, S), BF16) for r in t_rows]
    return pl.pallas_call(
        _proj_kernel,
        grid=(B, S // tm),
        in_specs=[row(D),
                  pl.BlockSpec((1, 1, 3 * D), lambda b, i: (b, 0, 0)),
                  pl.BlockSpec((1, D), lambda b, i: (0, 0)),
                  const(w_cat), const(w_t), tab, tab, tab],
        out_specs=[row(w) for w, _ in widths] + t_specs,
        out_shape=[jax.ShapeDtypeStruct((B, S, w), dt) for w, dt in widths] + t_shapes,
        compiler_params=pltpu.CompilerParams(
            dimension_semantics=("parallel", "parallel"), vmem_limit_bytes=VMEM_LIMIT),
        name="in_projection",
    )(x, mod[:, None, :], g_pre[None], w_cat, w_t, cos, sa, sb)


def _all_pair_specs(S, tq):
    qspec = pl.BlockSpec((1, tq, MIX_W), lambda b, i: (b, i, 0))
    kspec = pl.BlockSpec((1, S, MIX_W), lambda b, i: (b, 0, 0))
    vtspec = pl.BlockSpec((1, MIX_W, S), lambda b, i: (b, 0, 0))
    return qspec, kspec, vtspec


def _pair_lanes(p):
    return slice(p * LANES, (p + 1) * LANES)


def _tile_rows(ref, j, tk):
    return ref[0, pl.ds(pl.multiple_of(j * tk, tk), tk), :]


def _tile_rows_2d(ref, j, tk):
    return ref[pl.ds(pl.multiple_of(j * tk, tk), tk), :]


def _tile_lanes(ref, j, tk):
    return ref[0, :, pl.ds(pl.multiple_of(j * tk, tk), tk)]


def _stack_pair(q, is_a):
    return jnp.concatenate([jnp.where(is_a, q, 0), jnp.where(is_a, 0, q)], axis=0)


def _pipelined(n, first, produce, consume, state, scr):
    def put(slot, tiles):
        for a, tile in enumerate(tiles):
            scr[slot, a] = tile

    def get(slot):
        return [scr[slot, a] for a in range(scr.shape[1])]

    def body(t, carry):
        small, st = carry
        slot = jnp.bitwise_and(t, 1)
        st = consume((small, get(slot)), st)
        small, tiles = produce(t)
        put(1 - slot, tiles)
        return small, st

    put(0, first[1])
    small, st = lax.fori_loop(0, n, body, (first[0], state))
    return consume((small, get(jnp.bitwise_and(n, 1))), st)


def _finish_pairs(accs, tq, z_ref, o_ref):
    heads = [a[:, h * tq:(h + 1) * tq] for a in accs for h in range(2)]
    out = jnp.concatenate(heads, axis=0).T
    o_ref[0] = (out * _silu(z_ref[0].astype(F32))).astype(BF16)


def _stick_kernel(q_ref, k_ref, vt_ref, z_ref, o_ref, tile_scr, *, tq):
    i = pl.program_id(1)
    tk = tq
    is_a = _head_masks((tq, LANES))
    qs = (q_ref[0].astype(F32) * SCALE).astype(BF16)
    qcat = [_stack_pair(qs[:, _pair_lanes(p)], is_a) for p in range(N_PAIRS)]
    r_i = lax.broadcasted_iota(jnp.int32, (tk, tk), 0)
    c_i = lax.broadcasted_iota(jnp.int32, (tk, tk), 1)
    suffix = jnp.where(c_i >= r_i, 1.0, 0.0).astype(BF16)
    past1 = lax.broadcasted_iota(jnp.int32, (tk, tq), 0) < lax.broadcasted_iota(jnp.int32, (tk, tq), 1)
    past = jnp.concatenate([past1, past1], axis=1)

    def log_weights(j, diag):
        ks = _tile_rows(k_ref, j, tk)
        zs = [_nt_dot(ks[:, _pair_lanes(p)], qcat[p]) for p in range(N_PAIRS)]
        sps = [jnp.maximum(z, 0.0) + jnp.log(1.0 + jnp.exp2(jnp.abs(z) * (-LOG2E))) for z in zs]
        if diag:
            sps = [jnp.where(past, sp, 0.0) for sp in sps]
        css = [_dot(suffix, sp.astype(BF16)) for sp in sps]
        xs = [z - cs for z, cs in zip(zs, css)]
        if diag:
            xs = [jnp.where(past, x, NEG) for x in xs]
        return (j, [jnp.sum(sp, axis=0, keepdims=True) for sp in sps]), xs

    def accumulate(blk, st):
        (j, tots), xs = blk
        vts = _tile_lanes(vt_ref, j, tk)
        ws = [jnp.exp(xs[p] - st[p][1]).astype(BF16) for p in range(N_PAIRS)]
        return [(st[p][0] + _pv_pair(vts[_pair_lanes(p)], ws[p], tq), st[p][1] + tots[p])
                for p in range(N_PAIRS)]

    init = [(jnp.zeros((HEAD_DIM, 2 * tq), F32), jnp.zeros((1, 2 * tq), F32))] * N_PAIRS
    st = _pipelined(i, log_weights(i, True), lambda t: log_weights(i - 1 - t, False),
                    accumulate, init, tile_scr)
    _finish_pairs([acc for acc, _ in st], tq, z_ref, o_ref)


def _stick_breaking(q, k, vt, z, tq=256):
    B, S, _ = q.shape
    qspec, kspec, vtspec = _all_pair_specs(S, tq)
    return pl.pallas_call(
        functools.partial(_stick_kernel, tq=tq),
        grid=(B, S // tq),
        in_specs=[qspec, kspec, vtspec, qspec],
        out_specs=qspec,
        out_shape=jax.ShapeDtypeStruct((B, S, MIX_W), BF16),
        scratch_shapes=[pltpu.VMEM((2, N_PAIRS, tq, 2 * tq), F32)],
        compiler_params=pltpu.CompilerParams(
            dimension_semantics=("parallel", "arbitrary"), vmem_limit_bytes=VMEM_LIMIT),
        name="stick_breaking",
    )(q, k, vt, z)


def _dil_kernel(q_ref, k_ref, v_ref, z_ref, o_ref, m_scr, l_scr, a_scr, *, G):
    g = pl.program_id(2)
    blk = LANES
    is_a = _head_masks((blk, LANES))
    qrow = jnp.bitwise_and(lax.broadcasted_iota(jnp.int32, (2 * blk, 2 * blk), 0), blk - 1)
    col = lax.broadcasted_iota(jnp.int32, (2 * blk, 2 * blk), 1)
    band = jnp.logical_and(col >= qrow, col <= qrow + blk)
    n_sub = G // blk
    unroll = 8

    for ci, (_, d) in enumerate(DIL_CONFIGS):
        nb = G // (blk * d)
        sh = int(np.log2(nb))

        def sub(n, ci=ci, d=d, nb=nb, sh=sh):
            r = lax.shift_right_logical(n, sh)
            ub = jnp.bitwise_and(n, nb - 1)
            loc = ub * (blk * d) + r
            glob = g * G + loc
            has_prev = glob >= blk * d
            pstart = jnp.maximum(glob - blk * d, r)
            qs = (q_ref[0, pl.ds(loc, blk, stride=d), :] * SCALE).astype(BF16)
            kk = jnp.concatenate([k_ref[0, pl.ds(pstart, blk, stride=d), :],
                                  k_ref[0, pl.ds(glob, blk, stride=d), :]], axis=0).astype(BF16)
            vv = jnp.concatenate([v_ref[0, pl.ds(pstart, blk, stride=d), :],
                                  v_ref[0, pl.ds(glob, blk, stride=d), :]], axis=0).astype(BF16)
            ok = jnp.logical_and(band, jnp.logical_or(col >= blk, has_prev))
            s = jnp.where(ok, _nt_dot(_stack_pair(qs, is_a), kk), NEG)
            m = jnp.max(s, axis=-1, keepdims=True)
            p = jnp.exp(s - m)
            l = jnp.sum(p, axis=-1, keepdims=True)
            acc = _dot(p.astype(BF16), vv)
            rows = pl.ds(loc, blk, stride=d)
            m_scr[ci, rows, :] = jnp.where(is_a, m[:blk], m[blk:])
            l_scr[ci, rows, :] = jnp.where(is_a, l[:blk], l[blk:])
            a_scr[ci, rows, :] = jnp.where(is_a, acc[:blk], acc[blk:])

        def trip(t, _, sub=sub):
            for u in range(unroll):
                sub(t * unroll + u)
            return 0

        lax.fori_loop(0, n_sub // unroll, trip, 0)

    chunk = 256

    def combine(c, _):
        rows = pl.ds(pl.multiple_of(c * chunk, chunk), chunk)
        m0, m1, m2 = m_scr[0, rows, :], m_scr[1, rows, :], m_scr[2, rows, :]
        mm = jnp.maximum(jnp.maximum(m0, m1), m2)
        w0, w1, w2 = jnp.exp(m0 - mm), jnp.exp(m1 - mm), jnp.exp(m2 - mm)
        num = w0 * a_scr[0, rows, :] + w1 * a_scr[1, rows, :] + w2 * a_scr[2, rows, :]
        den = w0 * l_scr[0, rows, :] + w1 * l_scr[1, rows, :] + w2 * l_scr[2, rows, :]
        o_ref[0, rows, :] = (num / den * _silu(z_ref[0, rows, :].astype(F32))).astype(BF16)
        return 0

    lax.fori_loop(0, G // chunk, combine, 0)


def _dilated(q, k, v, z):
    B, S, _ = q.shape
    G = min(DIL_GROUP, S)
    gspec = pl.BlockSpec((1, G, LANES), lambda b, p, g: (b, g, p))
    kspec = pl.BlockSpec((1, S, LANES), lambda b, p, g: (b, 0, p))
    scr = pltpu.VMEM((len(DIL_CONFIGS), G, LANES), F32)
    return pl.pallas_call(
        functools.partial(_dil_kernel, G=G),
        grid=(B, 2, S // G),
        in_specs=[gspec, kspec, kspec, gspec],
        out_specs=gspec,
        out_shape=jax.ShapeDtypeStruct((B, S, MIX_W), BF16),
        scratch_shapes=[scr, scr, scr],
        compiler_params=pltpu.CompilerParams(
            dimension_semantics=("parallel", "parallel", "arbitrary"), vmem_limit_bytes=VMEM_LIMIT),
        name="dilated_window",
    )(q, k, v, z)


def _compress_kernel(k_ref, v_ref, p_ref, w1_ref, b1_ref, w2k_ref, w2vt_ref, cos_ref, sa_ref, sb_ref,
                     kc_ref, vc_ref):
    n = kc_ref.shape[1]
    dh = k_ref.shape[-1]
    for s, (t_ref, o_ref) in enumerate(((k_ref, kc_ref), (v_ref, vc_ref))):
        lo = jnp.zeros((n, CMP_HID), F32)
        hi = jnp.zeros((n, CMP_HID), F32)
        for l in range(CMP_STRIDE):
            t = t_ref[0, pl.ds(l, n, stride=CMP_STRIDE), :]
            lo = lo + _dot((t + p_ref[s, l:l + 1, :]).astype(BF16),
                           w1_ref[s, l * dh:(l + 1) * dh, :])
            lh = l + CMP_STRIDE
            hi = hi + _dot((t + p_ref[s, lh:lh + 1, :]).astype(BF16),
                           w1_ref[s, lh * dh:(lh + 1) * dh, :])
        pre = lo + pltpu.roll(hi, n - 1, 0) + b1_ref[s]
        hid = 0.5 * pre * (1.0 + jnp.tanh(np.sqrt(2.0 / np.pi).astype(np.float32)
                                          * (pre + 0.044715 * (pre * pre * pre))))
        hid = hid.astype(BF16)
        if s == 0:
            out = _dot(hid, w2k_ref[...])
            o_ref[0] = _rope_lane_tile(out, cos_ref[...], sa_ref[...], sb_ref[...]).astype(BF16)
        else:
            o_ref[0] = _nt_dot(w2vt_ref[...], hid).astype(BF16)


def _compress(kcr, vcr, cmp_pos, cmp_w1, cmp_b1, cmp_w2, tables_c):
    B, S, dh = kcr.shape
    n = S // CMP_STRIDE
    w1 = cmp_w1.astype(BF16)
    w2k = jnp.concatenate([cmp_w2[0], cmp_w2[0]], axis=-1).astype(BF16)
    w2vt = cmp_w2[1].T.astype(BF16)
    b1 = cmp_b1[:, None, :]
    cos, sa, sb = tables_c
    const = lambda a: pl.BlockSpec(a.shape, lambda b: (0,) * a.ndim)
    tspec = pl.BlockSpec((1, S, dh), lambda b: (b, 0, 0))
    return pl.pallas_call(
        _compress_kernel,
        grid=(B,),
        in_specs=[tspec, tspec, const(cmp_pos), const(w1), const(b1), const(w2k), const(w2vt),
                  const(cos), const(sa), const(sb)],
        out_specs=[pl.BlockSpec((1, n, LANES), lambda b: (b, 0, 0)),
                   pl.BlockSpec((1, dh, n), lambda b: (b, 0, 0))],
        out_shape=[jax.ShapeDtypeStruct((B, n, LANES), BF16),
                   jax.ShapeDtypeStruct((B, dh, n), BF16)],
        compiler_params=pltpu.CompilerParams(
            dimension_semantics=("parallel",), vmem_limit_bytes=VMEM_LIMIT),
        name="compress_tokens",
    )(kcr, vcr, cmp_pos, w1, b1, w2k, w2vt, cos, sa, sb)


def _nsa_kernel(q_ref, kc_ref, vct_ref, ks_ref, vst_ref, kw_ref, vwt_ref, gc_ref, z_ref,
                ovt_ref, ext_ref, o_ref, tile_scr, *, tq, tk, n_sel):
    i = pl.program_id(1)
    Q = N_HEADS * tq
    is_a = _head_masks((tq, LANES))
    qs = (q_ref[0].astype(F32) * SCALE).astype(BF16)
    q4 = jnp.concatenate([_stack_pair(qs[:, :LANES], is_a), _stack_pair(qs[:, LANES:], is_a)], axis=0)
    heads = lambda x: jnp.concatenate([x] * N_HEADS, axis=1)
    qpos1 = i * tq + lax.broadcasted_iota(jnp.int32, (1, tq), 1)
    qpos = heads(qpos1)

    kpos0 = lax.broadcasted_iota(jnp.int32, (tk, Q), 0)
    J = (i * tq) // tk

    n_win = -(-WIN_LEN // tk) + 1
    s_w, vt_w = [], []
    for a in range(n_win):
        j_a = J - (n_win - 1) + a
        kpos = j_a * tk + kpos0
        dist = qpos - kpos
        ok = jnp.logical_and(jnp.logical_and(dist >= 0, dist < WIN_LEN), kpos >= 0)
        j_c = jnp.maximum(j_a, 0)
        s_w.append(jnp.where(ok, _nt_dot(_tile_rows(kw_ref, j_c, tk), q4), NEG))
        vt_w.append(_tile_lanes(vwt_ref, j_c, tk))
    m_w = functools.reduce(jnp.maximum, [jnp.max(s, axis=0, keepdims=True) for s in s_w])
    p_w = [jnp.exp(s - m_w) for s in s_w]
    l_w = sum(jnp.sum(p, axis=0, keepdims=True) for p in p_w)
    o_win = sum(_dot(vt, p.astype(BF16)) for vt, p in zip(vt_w, p_w)) / l_w

    n_c = kc_ref.shape[1]
    c_end = lax.broadcasted_iota(jnp.int32, (n_c, Q), 0) * CMP_STRIDE + (CMP_LEN - 1)
    c_valid = c_end <= qpos
    sc = jnp.where(c_valid, _nt_dot(kc_ref[0], q4), NEG)
    pc = jnp.exp(sc - jnp.max(sc, axis=0, keepdims=True)) * jnp.where(c_valid, 1.0, 0.0)
    lc = jnp.sum(pc, axis=0, keepdims=True)
    pc = pc * jnp.where(lc > 0.0, 1.0 / lc, 0.0)
    o_cmp = _dot(vct_ref[0], pc.astype(BF16))

    pcs = pc[:, 0:tq] + pc[:, tq:2 * tq] + pc[:, 2 * tq:3 * tq] + pc[:, 3 * tq:]
    hi, lo = _split_bf16(pcs)
    imp = _dot(ovt_ref[...], hi) + _dot(ovt_ref[...], lo)
    nsp = imp.shape[0]
    blk = lax.broadcasted_iota(jnp.int32, (nsp, tq), 0)
    own = lax.shift_right_logical(qpos1, int(np.log2(SEL_LEN)))
    s_valid = blk <= own
    forced = jnp.logical_or(blk == 0, blk >= own - 1)
    imp = jnp.where(s_valid, jnp.where(forced, BIG, imp), NEG)
    rank = jnp.zeros((nsp, tq), F32)
    for s in range(n_sel):
        c = imp[s:s + 1, :]
        beats = jnp.logical_or(c > imp, jnp.logical_and(c == imp, blk > s))
        rank = rank + jnp.where(beats, 1.0, 0.0)
    chosen = jnp.logical_and(rank < float(min(SEL_TOPN, n_sel)), s_valid)
    sel = jnp.where(chosen, 1.0, 0.0).astype(BF16)

    def sel_scores(j, diag):
        bias = (_dot(_tile_rows_2d(ext_ref, j, tk), sel) - 1.0) * (-NEG)
        s = _nt_dot(_tile_rows(ks_ref, j, tk), q4) + heads(bias)
        if diag:
            s = jnp.where(j * tk + kpos0 <= qpos, s, NEG)
        return (j, jnp.max(s, axis=0, keepdims=True)), [s]

    def consume(blk_s, st):
        (j, s_max), (s,) = blk_s
        m, l, acc = st
        m_new = jnp.maximum(m, s_max)
        alpha = jnp.exp(m - m_new)
        p = jnp.exp(s - m_new)
        l = alpha * l + jnp.sum(p, axis=0, keepdims=True)
        acc = alpha * acc + _dot(_tile_lanes(vst_ref, j, tk), p.astype(BF16))
        return m_new, l, acc

    init = (jnp.full((1, Q), NEG, F32), jnp.zeros((1, Q), F32), jnp.zeros((HEAD_DIM, Q), F32))

    _, l_s, a_s = _pipelined(J, sel_scores(J, True), lambda t: sel_scores(t, False), consume, init,
                             tile_scr)
    o_sel = a_s / l_s

    gates = jax.nn.sigmoid(gc_ref[0]).T
    outs = []
    for h in range(N_HEADS):
        cols = slice(h * tq, (h + 1) * tq)
        outs.append(gates[3 * h:3 * h + 1, :] * o_cmp[:, cols]
                    + gates[3 * h + 1:3 * h + 2, :] * o_sel[:, cols]
                    + gates[3 * h + 2:3 * h + 3, :] * o_win[:, cols])
    out = jnp.concatenate(outs, axis=0).T
    o_ref[0] = (out * _silu(z_ref[0].astype(F32))).astype(BF16)


def _nsa(q, kc2, vct, ks2, vst, kw2, vwt, gc, z, tq=128, tk=256):
    B, S, _ = q.shape
    n_c = kc2.shape[1]
    n_sel = S // SEL_LEN
    nsp = -(-n_sel // SUBLANES) * SUBLANES
    n_cmp = (S - CMP_LEN) // CMP_STRIDE + 1
    c_start = np.arange(n_c) * CMP_STRIDE
    s_start = np.arange(nsp) * SEL_LEN
    overlap_t = np.clip(np.minimum(c_start[None, :] + CMP_LEN, s_start[:, None] + SEL_LEN)
                        - np.maximum(c_start[None, :], s_start[:, None]), 0, None) / CMP_LEN
    overlap_t[:, n_cmp:] = 0.0
    overlap_t[n_sel:, :] = 0.0
    expand_t = (np.arange(S)[:, None] // SEL_LEN == np.arange(nsp)[None, :]).astype(np.float32)
    row = lambda w: pl.BlockSpec((1, tq, w), lambda b, i: (b, i, 0))
    full = lambda a: pl.BlockSpec((1,) + a.shape[1:], lambda b, i: (b, 0, 0))
    const = lambda a: pl.BlockSpec(a.shape, lambda b, i: (0, 0))
    ovt = jnp.asarray(overlap_t, BF16)
    ext = jnp.asarray(expand_t, BF16)
    return pl.pallas_call(
        functools.partial(_nsa_kernel, tq=tq, tk=tk, n_sel=n_sel),
        grid=(B, S // tq),
        in_specs=[row(MIX_W), full(kc2), full(vct), full(ks2), full(vst), full(kw2), full(vwt),
                  row(LANES), row(MIX_W), const(ovt), const(ext)],
        out_specs=row(MIX_W),
        out_shape=jax.ShapeDtypeStruct((B, S, MIX_W), BF16),
        scratch_shapes=[pltpu.VMEM((2, 1, tk, N_HEADS * tq), F32)],
        compiler_params=pltpu.CompilerParams(
            dimension_semantics=("parallel", "parallel"), vmem_limit_bytes=VMEM_LIMIT),
        name="native_sparse",
    )(q, kc2, vct, ks2, vst, kw2, vwt, gc, z, ovt, ext)


def _moba_kernel(q_ref, k_ref, vt_ref, z_ref, o_ref, km_scr, bias_scr, tile_scr, *, n_blk):
    i = pl.program_id(1)
    tq = tk = MOBA_BLOCK
    nb_pad = km_scr.shape[0]
    is_a = _head_masks((tq, LANES))

    @pl.when(i == 0)
    def _():
        km_scr[...] = jnp.zeros_like(km_scr)
        for b in range(n_blk):
            kb = k_ref[0, b * tk:(b + 1) * tk, :].astype(F32)
            km_scr[b:b + 1, :] = jnp.mean(kb, axis=0, keepdims=True)

    q = q_ref[0]
    qs = (q.astype(F32) * SCALE).astype(BF16)
    blk = lax.broadcasted_iota(jnp.int32, (nb_pad, 2 * tq), 0)
    n_top = min(MOBA_TOPK, max(n_blk - 1, 1))
    qcat = []
    for p in range(N_PAIRS):
        qcat.append(_stack_pair(qs[:, _pair_lanes(p)], is_a))
        qgcat = _stack_pair(q[:, _pair_lanes(p)], is_a)
        km_hi, km_lo = _split_bf16(km_scr[:, _pair_lanes(p)])
        gsc = _nt_dot(km_hi, qgcat) + _nt_dot(km_lo, qgcat)
        gsc = jnp.where(blk < i, gsc, NEG)
        rank = jnp.zeros((nb_pad, 2 * tq), F32)
        for s in range(n_blk):
            c = gsc[s:s + 1, :]
            beats = jnp.logical_or(c > gsc, jnp.logical_and(c == gsc, blk > s))
            rank = rank + jnp.where(beats, 1.0, 0.0)
        attends = jnp.logical_or(jnp.logical_and(rank < float(n_top), blk < i), blk == i)
        bias_scr[p] = jnp.where(attends, 0.0, NEG)

    causal1 = lax.broadcasted_iota(jnp.int32, (tk, tq), 0) <= lax.broadcasted_iota(jnp.int32, (tk, tq), 1)
    causal = jnp.concatenate([causal1, causal1], axis=1)

    def scores(j, diag):
        ks = _tile_rows(k_ref, j, tk)
        ss = [_nt_dot(ks[:, _pair_lanes(p)], qcat[p]) for p in range(N_PAIRS)]
        if diag:
            ss = [jnp.where(causal, s, NEG) for s in ss]
        return (j, [jnp.max(s, axis=0, keepdims=True) for s in ss]), ss

    def consume(blk_s, st):
        (j, maxes), ss = blk_s
        vts = _tile_lanes(vt_ref, j, tk)
        pr = range(N_PAIRS)
        bias = [bias_scr[p, pl.ds(j, 1), :] for p in pr]
        m_new = [jnp.maximum(st[p][0], maxes[p] + bias[p]) for p in pr]
        ps = [jnp.exp(ss[p] - (m_new[p] - bias[p])) for p in pr]
        pvs = [_pv_pair(vts[_pair_lanes(p)], ps[p].astype(BF16), tq) for p in pr]
        out = []
        for p in pr:
            m, l, acc = st[p]
            alpha = jnp.exp(m - m_new[p])
            out.append((m_new[p], alpha * l + jnp.sum(ps[p], axis=0, keepdims=True),
                        alpha * acc + pvs[p]))
        return out

    init = [(jnp.full((1, 2 * tq), NEG, F32), jnp.zeros((1, 2 * tq), F32),
             jnp.zeros((HEAD_DIM, 2 * tq), F32))] * N_PAIRS
    st = _pipelined(i, scores(i, True), lambda t: scores(t, False), consume, init, tile_scr)
    _finish_pairs([acc / l for _, l, acc in st], tq, z_ref, o_ref)


def _moba(q, k, vt, z):
    B, S, _ = q.shape
    tq = MOBA_BLOCK
    n_blk = S // MOBA_BLOCK
    nb_pad = -(-n_blk // SUBLANES) * SUBLANES
    qspec, kspec, vtspec = _all_pair_specs(S, tq)
    return pl.pallas_call(
        functools.partial(_moba_kernel, n_blk=n_blk),
        grid=(B, S // tq),
        in_specs=[qspec, kspec, vtspec, qspec],
        out_specs=qspec,
        out_shape=jax.ShapeDtypeStruct((B, S, MIX_W), BF16),
        scratch_shapes=[pltpu.VMEM((nb_pad, MIX_W), F32),
                        pltpu.VMEM((N_PAIRS, nb_pad, 2 * tq), F32),
                        pltpu.VMEM((2, N_PAIRS, tq, 2 * tq), F32)],
        compiler_params=pltpu.CompilerParams(
            dimension_semantics=("parallel", "arbitrary"), vmem_limit_bytes=VMEM_LIMIT),
        name="moba",
    )(q, k, vt, z)


def _out_kernel(ma_ref, mb_ref, mc_ref, md_ref, w_ref, x_ref, mod_ref, g_ref, o_ref):
    D = x_ref.shape[-1]
    acc = _dot(ma_ref[0], w_ref[0:MIX_W, :])
    acc = acc + _dot(mb_ref[0], w_ref[MIX_W:2 * MIX_W, :])
    acc = acc + _dot(mc_ref[0], w_ref[2 * MIX_W:3 * MIX_W, :])
    acc = acc + _dot(md_ref[0], w_ref[3 * MIX_W:, :])
    ms = jnp.mean(acc * acc, axis=-1, keepdims=True)
    y = acc * lax.rsqrt(ms + EPS) * g_ref[...]
    o_ref[0] = x_ref[0] + mod_ref[0, :, 2 * D:] * y


def _out_projection(mixed, w_out, x, mod, g_post, tm):
    B, S, D = x.shape
    row = lambda w: pl.BlockSpec((1, tm, w), lambda b, i: (b, i, 0))
    return pl.pallas_call(
        _out_kernel,
        grid=(B, S // tm),
        in_specs=[row(MIX_W)] * 4 + [pl.BlockSpec(w_out.shape, lambda b, i: (0, 0)), row(D),
                                     pl.BlockSpec((1, 1, 3 * D), lambda b, i: (b, 0, 0)),
                                     pl.BlockSpec((1, D), lambda b, i: (0, 0))],
        out_specs=row(D),
        out_shape=jax.ShapeDtypeStruct((B, S, D), F32),
        compiler_params=pltpu.CompilerParams(
            dimension_semantics=("parallel", "parallel"), vmem_limit_bytes=VMEM_LIMIT),
        name="out_projection",
    )(*mixed, w_out.astype(BF16), x, mod[:, None, :], g_post[None])


def _layer(x, c, norm_pre, norm_post, w_mod, b_mod, w_in, w_out, cmp_pos, cmp_w1, cmp_b1, cmp_w2,
           tables, tables_c, tm):
    mod = _modulation(c, w_mod, b_mod)
    w_cat, w_t = _pack_w_in(w_in)
    (qa, ka, za, qb, kb, vb, zb, qc, kcr, vcr, gc, ks2, kw2, zc,
     qd, kd, zd, vat, vdt, vst, vwt) = _projection(x, mod, norm_pre, w_cat, w_t, tables, tm)
    oa = _stick_breaking(qa, ka, vat, za)
    ob = _dilated(qb, kb, vb, zb)
    kc2, vct = _compress(kcr, vcr, cmp_pos, cmp_w1, cmp_b1, cmp_w2, tables_c)
    oc = _nsa(qc, kc2, vct, ks2, vst, kw2, vwt, gc, zc)
    od = _moba(qd, kd, vdt, zd)
    return _out_projection((oa, ob, oc, od), w_out, x, mod, norm_post, tm)


def kernel(x, c, norm_pre, norm_post, w_mod, b_mod, w_in, w_out, cmp_pos, cmp_w1, cmp_b1, cmp_w2):
    S = x.shape[1]
    tables = _rope_lane_tables(np.arange(S))
    tables_c = _rope_lane_tables(np.arange(S // CMP_STRIDE) * CMP_STRIDE + CMP_LEN - 1)
    tm = min(512, S)
    for l in range(norm_pre.shape[0]):
        x = _layer(x, c, norm_pre[l], norm_post[l], w_mod[l], b_mod[l], w_in[l], w_out[l],
                   cmp_pos[l], cmp_w1[l], cmp_b1[l], cmp_w2[l], tables, tables_c, tm)
    return x
```

```python
import functools

import numpy as np
import jax
import jax.numpy as jnp
from jax import lax
from jax.experimental import pallas as pl
from jax.experimental.pallas import tpu as pltpu

F32 = jnp.float32
BF16 = jnp.bfloat16

N_HEADS = 4
HEAD_DIM = 64
MIX_W = N_HEADS * HEAD_DIM
ROPE_THETA = 500000.0
ROPE_DIM = HEAD_DIM // 4
ROPE_HALF = ROPE_DIM // 2
EPS = 1e-6
NEG = -1e30
BIG = 1e9
SCALE = HEAD_DIM ** -0.5
LOG2E = 1.4426950408889634
DIL_CONFIGS = ((128, 1), (512, 4), (2048, 16))
CMP_LEN = 32
CMP_STRIDE = 16
CMP_HID = 256
SEL_LEN = 64
SEL_TOPN = 16
WIN_LEN = 512
MOBA_BLOCK = 256
MOBA_TOPK = 3

N_PAIRS = N_HEADS // 2
LANES = 128
SUBLANES = 8
TILE_W = 2 * LANES
DIL_GROUP = 2048
VMEM_LIMIT = 56 * 1024 * 1024


def _nt_dot(a, b):
    return lax.dot_general(a, b, (((1,), (1,)), ((), ())), preferred_element_type=F32)


def _dot(a, b):
    return jnp.dot(a, b, preferred_element_type=F32)


def _split_bf16(x):
    hi = x.astype(BF16)
    lo = (x - hi.astype(F32)).astype(BF16)
    return hi, lo


def _silu(x):
    return x * jax.nn.sigmoid(x)


def _head_masks(shape):
    lane = lax.broadcasted_iota(jnp.int32, shape, 1)
    return lane < HEAD_DIM


def _pv_pair(vt, p, tq):
    return jnp.concatenate([_dot(vt[:HEAD_DIM], p[:, :tq]), _dot(vt[HEAD_DIM:], p[:, tq:])], axis=1)


def _mod_kernel(c_ref, w_ref, b_ref, o_ref):
    c = c_ref[...]
    o_ref[...] = jnp.dot(_silu(c), w_ref[...], preferred_element_type=F32,
                         precision=lax.Precision.HIGHEST) + b_ref[...]


def _modulation(c, w_mod, b_mod):
    B, D = c.shape
    N = w_mod.shape[1]
    rows = SUBLANES
    cp = jnp.zeros((rows, D), F32).at[:B].set(c)
    tn = N // 4
    out = pl.pallas_call(
        _mod_kernel,
        grid=(N // tn,),
        in_specs=[pl.BlockSpec((rows, D), lambda j: (0, 0)),
                  pl.BlockSpec((D, tn), lambda j: (0, j)),
                  pl.BlockSpec((1, tn), lambda j: (0, j))],
        out_specs=pl.BlockSpec((rows, tn), lambda j: (0, j)),
        out_shape=jax.ShapeDtypeStruct((rows, N), F32),
        name="modulation",
    )(cp, w_mod, b_mod[None])
    return out[:B]


def _rope_lane_tile(y, cos, sa, sb):
    return y * cos + pltpu.roll(y, LANES - ROPE_HALF, 1) * sa + pltpu.roll(y, ROPE_HALF, 1) * sb


def _proj_kernel(x_ref, mod_ref, g_ref, w_ref, wt_ref, cos_ref, sa_ref, sb_ref,
                 qa_ref, ka_ref, za_ref,
                 qb_ref, kb_ref, vb_ref, zb_ref,
                 qc_ref, kcr_ref, vcr_ref, gc_ref, ks_ref, kw_ref, zc_ref,
                 qd_ref, kd_ref, zd_ref, vat_ref, vdt_ref, vst_ref, vwt_ref):
    D = x_ref.shape[-1]
    x = x_ref[0]
    ms = jnp.mean(x * x, axis=-1, keepdims=True)
    y = x * lax.rsqrt(ms + EPS) * g_ref[...]
    shift = mod_ref[0, :, 0:D]
    scale = mod_ref[0, :, D:2 * D]
    h = (y * (1.0 + scale) + shift).astype(BF16)
    cos, sa, sb = cos_ref[...], sa_ref[...], sb_ref[...]

    def tile(t):
        return _dot(h, w_ref[:, t * TILE_W:(t + 1) * TILE_W])

    def rope_lo(y):
        return _rope_lane_tile(y[:, :LANES], cos, sa, sb)

    def rope_all(y):
        return jnp.concatenate([rope_lo(y), _rope_lane_tile(y[:, LANES:], cos, sa, sb)], axis=1)

    T = PROJ_TILES
    qa_ref[0] = tile(T["qa"]).astype(BF16)
    ka_ref[0] = tile(T["ka"]).astype(BF16)
    za_ref[0] = tile(T["za"]).astype(BF16)
    qb_ref[0] = rope_all(tile(T["qb"]))
    kb_ref[0] = rope_all(tile(T["kb"]))
    vb_ref[0] = tile(T["vb"])
    zb_ref[0] = tile(T["zb"]).astype(BF16)
    qc_ref[0] = rope_all(tile(T["qc"])).astype(BF16)
    small = tile(T["c_small"])
    kcr_ref[0] = small[:, 0:HEAD_DIM]
    vcr_ref[0] = small[:, HEAD_DIM:LANES]
    gc_ref[0] = small[:, LANES:]
    keys = rope_all(tile(T["c_keys"])).astype(BF16)
    ks_ref[0] = keys[:, :LANES]
    kw_ref[0] = keys[:, LANES:]
    zc_ref[0] = tile(T["zc"]).astype(BF16)
    qd_ref[0] = rope_all(tile(T["qd"])).astype(BF16)
    kd_ref[0] = rope_all(tile(T["kd"])).astype(BF16)
    zd_ref[0] = tile(T["zd"]).astype(BF16)
    vt = _nt_dot(wt_ref[...], h).astype(BF16)
    vat_ref[0] = vt[0:MIX_W]
    vdt_ref[0] = vt[MIX_W:2 * MIX_W]
    vst_ref[0] = vt[2 * MIX_W:2 * MIX_W + HEAD_DIM]
    vwt_ref[0] = vt[2 * MIX_W + HEAD_DIM:]


PROJ_TILES = dict(qa=0, ka=1, za=3, qb=4, kb=5, vb=6, zb=7, qc=8, c_small=9, c_keys=10,
                  zc=11, qd=12, kd=13, zd=15)


def _pack_w_in(w_in):
    D = w_in.shape[0]
    w = w_in.astype(BF16)
    offs = np.cumsum([0] + [MIX_W] * 9 + [HEAD_DIM] * 6 + [3 * N_HEADS] + [MIX_W] * 5)
    seg = lambda i: w[:, offs[i]:offs[i + 1]]
    span = lambda i, j: w[:, offs[i]:offs[j]]
    w_cat = jnp.concatenate(
        [span(0, 9),
         span(9, 11), seg(15), jnp.zeros((D, LANES - 3 * N_HEADS), BF16),
         seg(11), seg(11), seg(13), seg(13),
         span(16, 21)], axis=1)
    w_t = jnp.concatenate([seg(2), seg(19), seg(12), seg(14)], axis=1).T
    return w_cat, w_t


def _rope_angles(pos):
    inv_freq = 1.0 / (ROPE_THETA ** (np.arange(0, ROPE_DIM, 2, dtype=np.float32) / ROPE_DIM))
    ang = jnp.asarray(pos, F32)[:, None] * jnp.asarray(inv_freq, F32)[None, :]
    return jnp.cos(ang), jnp.sin(ang)


def _rope_lane_tables(pos):
    cos, sin = _rope_angles(pos)
    n = cos.shape[0]
    rest = HEAD_DIM - ROPE_DIM
    cos_h = jnp.concatenate([cos, cos, jnp.ones((n, rest), F32)], axis=1)
    sa_h = jnp.concatenate([-sin, jnp.zeros((n, HEAD_DIM - ROPE_HALF), F32)], axis=1)
    sb_h = jnp.concatenate([jnp.zeros((n, ROPE_HALF), F32), sin, jnp.zeros((n, rest), F32)], axis=1)
    two = lambda t: jnp.concatenate([t, t], axis=1)
    return two(cos_h), two(sa_h), two(sb_h)


def _projection(x, mod, g_pre, w_cat, w_t, tables, tm):
    B, S, D = x.shape
    cos, sa, sb = tables
    row = lambda w: pl.BlockSpec((1, tm, w), lambda b, i: (b, i, 0))
    tab = pl.BlockSpec((tm, LANES), lambda b, i: (i, 0))
    const = lambda a: pl.BlockSpec(a.shape, lambda b, i: (0,) * a.ndim)
    widths = [(MIX_W, BF16)] * 3 + [(MIX_W, F32)] * 3 + [(MIX_W, BF16)] + \
             [(MIX_W, BF16), (HEAD_DIM, F32), (HEAD_DIM, F32), (LANES, F32),
              (LANES, BF16), (LANES, BF16), (MIX_W, BF16)] + \
             [(MIX_W, BF16)] * 3
    t_rows = (MIX_W, MIX_W, HEAD_DIM, HEAD_DIM)
    t_specs = [pl.BlockSpec((1, r, tm), lambda b, i: (b, 0, i)) for r in t_rows]
    t_shapes = [jax.ShapeDtypeStruct((B, r, S), BF16) for r in t_rows]
    return pl.pallas_call(
        _proj_kernel,
        grid=(B, S // tm),
        in_specs=[row(D),
                  pl.BlockSpec((1, 1, 3 * D), lambda b, i: (b, 0, 0)),
                  pl.BlockSpec((1, D), lambda b, i: (0, 0)),
                  const(w_cat), const(w_t), tab, tab, tab],
        out_specs=[row(w) for w, _ in widths] + t_specs,
        out_shape=[jax.ShapeDtypeStruct((B, S, w), dt) for w, dt in widths] + t_shapes,
        compiler_params=pltpu.CompilerParams(
            dimension_semantics=("parallel", "parallel"), vmem_limit_bytes=VMEM_LIMIT),
        name="in_projection",
    )(x, mod[:, None, :], g_pre[None], w_cat, w_t, cos, sa, sb)


def _all_pair_specs(S, tq):
    qspec = pl.BlockSpec((1, tq, MIX_W), lambda b, i: (b, i, 0))
    kspec = pl.BlockSpec((1, S, MIX_W), lambda b, i: (b, 0, 0))
    vtspec = pl.BlockSpec((1, MIX_W, S), lambda b, i: (b, 0, 0))
    return qspec, kspec, vtspec


def _pair_lanes(p):
    return slice(p * LANES, (p + 1) * LANES)


def _tile_rows(ref, j, tk):
    return ref[0, pl.ds(pl.multiple_of(j * tk, tk), tk), :]


def _tile_rows_2d(ref, j, tk):
    return ref[pl.ds(pl.multiple_of(j * tk, tk), tk), :]


def _tile_lanes(ref, j, tk):
    return ref[0, :, pl.ds(pl.multiple_of(j * tk, tk), tk)]


def _stack_pair(q, is_a):
    return jnp.concatenate([jnp.where(is_a, q, 0), jnp.where(is_a, 0, q)], axis=0)


def _pipelined(n, first, produce, consume, state, scr):
    def put(slot, tiles):
        for a, tile in enumerate(tiles):
            scr[slot, a] = tile

    def get(slot):
        return [scr[slot, a] for a in range(scr.shape[1])]

    def step(t, slot, carry):
        small, st = carry
        small_new, tiles = produce(t)
        put(1 - slot, tiles)
        return small_new, consume((small, get(slot)), st)

    put(0, first[1])
    carry = lax.fori_loop(0, lax.shift_right_logical(n, 1),
                          lambda u, c: step(2 * u + 1, 1, step(2 * u, 0, c)), (first[0], state))

    def odd_tail(c):
        small, st = step(n - 1, 0, c)
        return consume((small, get(1)), st)

    return lax.cond(jnp.bitwise_and(n, 1) == 1, odd_tail,
                    lambda c: consume((c[0], get(0)), c[1]), carry)


def _finish_pairs(accs, tq, z_ref, o_ref):
    heads = [a[:, h * tq:(h + 1) * tq] for a in accs for h in range(2)]
    out = jnp.concatenate(heads, axis=0).T
    o_ref[0] = (out * _silu(z_ref[0].astype(F32))).astype(BF16)


def _stick_kernel(q_ref, k_ref, vt_ref, z_ref, o_ref, tile_scr, *, tq):
    i = pl.program_id(1)
    tk = tq
    is_a = _head_masks((tq, LANES))
    qs = (q_ref[0].astype(F32) * SCALE).astype(BF16)
    qcat = [_stack_pair(qs[:, _pair_lanes(p)], is_a) for p in range(N_PAIRS)]
    r_i = lax.broadcasted_iota(jnp.int32, (tk, tk), 0)
    c_i = lax.broadcasted_iota(jnp.int32, (tk, tk), 1)
    suffix = jnp.where(c_i >= r_i, 1.0, 0.0).astype(BF16)
    past1 = lax.broadcasted_iota(jnp.int32, (tk, tq), 0) < lax.broadcasted_iota(jnp.int32, (tk, tq), 1)
    past = jnp.concatenate([past1, past1], axis=1)

    def log_weights(j, diag):
        ks = _tile_rows(k_ref, j, tk)
        zs = [_nt_dot(ks[:, _pair_lanes(p)], qcat[p]) for p in range(N_PAIRS)]
        sps = [jnp.maximum(z, 0.0) + jnp.log(1.0 + jnp.exp2(jnp.abs(z) * (-LOG2E))) for z in zs]
        if diag:
            sps = [jnp.where(past, sp, 0.0) for sp in sps]
        css = [_dot(suffix, sp.astype(BF16)) for sp in sps]
        xs = [z - cs for z, cs in zip(zs, css)]
        if diag:
            xs = [jnp.where(past, x, NEG) for x in xs]
        return (j, [jnp.sum(sp, axis=0, keepdims=True) for sp in sps]), xs

    def accumulate(blk, st):
        (j, tots), xs = blk
        vts = _tile_lanes(vt_ref, j, tk)
        ws = [jnp.exp(xs[p] - st[p][1]).astype(BF16) for p in range(N_PAIRS)]
        return [(st[p][0] + _pv_pair(vts[_pair_lanes(p)], ws[p], tq), st[p][1] + tots[p])
                for p in range(N_PAIRS)]

    init = [(jnp.zeros((HEAD_DIM, 2 * tq), F32), jnp.zeros((1, 2 * tq), F32))] * N_PAIRS
    st = _pipelined(i, log_weights(i, True), lambda t: log_weights(i - 1 - t, False),
                    accumulate, init, tile_scr)
    _finish_pairs([acc for acc, _ in st], tq, z_ref, o_ref)


def _stick_breaking(q, k, vt, z, tq=256):
    B, S, _ = q.shape
    qspec, kspec, vtspec = _all_pair_specs(S, tq)
    return pl.pallas_call(
        functools.partial(_stick_kernel, tq=tq),
        grid=(B, S // tq),
        in_specs=[qspec, kspec, vtspec, qspec],
        out_specs=qspec,
        out_shape=jax.ShapeDtypeStruct((B, S, MIX_W), BF16),
        scratch_shapes=[pltpu.VMEM((2, N_PAIRS, tq, 2 * tq), F32)],
        compiler_params=pltpu.CompilerParams(
            dimension_semantics=("parallel", "arbitrary"), vmem_limit_bytes=VMEM_LIMIT),
        name="stick_breaking",
    )(q, k, vt, z)


def _dil_kernel(q_ref, k_ref, v_ref, z_ref, o_ref, m_scr, l_scr, a_scr, *, G):
    g = pl.program_id(2)
    blk = LANES
    is_a = _head_masks((blk, LANES))
    qrow = jnp.bitwise_and(lax.broadcasted_iota(jnp.int32, (2 * blk, 2 * blk), 0), blk - 1)
    col = lax.broadcasted_iota(jnp.int32, (2 * blk, 2 * blk), 1)
    band = jnp.logical_and(col >= qrow, col <= qrow + blk)
    n_sub = G // blk
    unroll = 8

    for ci, (_, d) in enumerate(DIL_CONFIGS):
        nb = G // (blk * d)
        sh = int(np.log2(nb))

        def sub(n, ci=ci, d=d, nb=nb, sh=sh):
            r = lax.shift_right_logical(n, sh)
            ub = jnp.bitwise_and(n, nb - 1)
            loc = ub * (blk * d) + r
            glob = g * G + loc
            has_prev = glob >= blk * d
            pstart = jnp.maximum(glob - blk * d, r)
            qs = (q_ref[0, pl.ds(loc, blk, stride=d), :] * SCALE).astype(BF16)
            kk = jnp.concatenate([k_ref[0, pl.ds(pstart, blk, stride=d), :],
                                  k_ref[0, pl.ds(glob, blk, stride=d), :]], axis=0).astype(BF16)
            vv = jnp.concatenate([v_ref[0, pl.ds(pstart, blk, stride=d), :],
                                  v_ref[0, pl.ds(glob, blk, stride=d), :]], axis=0).astype(BF16)
            ok = jnp.logical_and(band, jnp.logical_or(col >= blk, has_prev))
            s = jnp.where(ok, _nt_dot(_stack_pair(qs, is_a), kk), NEG)
            m = jnp.max(s, axis=-1, keepdims=True)
            p = jnp.exp(s - m)
            l = jnp.sum(p, axis=-1, keepdims=True)
            acc = _dot(p.astype(BF16), vv)
            rows = pl.ds(loc, blk, stride=d)
            m_scr[ci, rows, :] = jnp.where(is_a, m[:blk], m[blk:])
            l_scr[ci, rows, :] = jnp.where(is_a, l[:blk], l[blk:])
            a_scr[ci, rows, :] = jnp.where(is_a, acc[:blk], acc[blk:])

        def trip(t, _, sub=sub):
            for u in range(unroll):
                sub(t * unroll + u)
            return 0

        lax.fori_loop(0, n_sub // unroll, trip, 0)

    chunk = 256

    def combine(c, _):
        rows = pl.ds(pl.multiple_of(c * chunk, chunk), chunk)
        m0, m1, m2 = m_scr[0, rows, :], m_scr[1, rows, :], m_scr[2, rows, :]
        mm = jnp.maximum(jnp.maximum(m0, m1), m2)
        w0, w1, w2 = jnp.exp(m0 - mm), jnp.exp(m1 - mm), jnp.exp(m2 - mm)
        num = w0 * a_scr[0, rows, :] + w1 * a_scr[1, rows, :] + w2 * a_scr[2, rows, :]
        den = w0 * l_scr[0, rows, :] + w1 * l_scr[1, rows, :] + w2 * l_scr[2, rows, :]
        o_ref[0, rows, :] = (num / den * _silu(z_ref[0, rows, :].astype(F32))).astype(BF16)
        return 0

    lax.fori_loop(0, G // chunk, combine, 0)


def _dilated(q, k, v, z):
    B, S, _ = q.shape
    G = min(DIL_GROUP, S)
    gspec = pl.BlockSpec((1, G, LANES), lambda b, p, g: (b, g, p))
    kspec = pl.BlockSpec((1, S, LANES), lambda b, p, g: (b, 0, p))
    scr = pltpu.VMEM((len(DIL_CONFIGS), G, LANES), F32)
    return pl.pallas_call(
        functools.partial(_dil_kernel, G=G),
        grid=(B, 2, S // G),
        in_specs=[gspec, kspec, kspec, gspec],
        out_specs=gspec,
        out_shape=jax.ShapeDtypeStruct((B, S, MIX_W), BF16),
        scratch_shapes=[scr, scr, scr],
        compiler_params=pltpu.CompilerParams(
            dimension_semantics=("parallel", "parallel", "arbitrary"), vmem_limit_bytes=VMEM_LIMIT),
        name="dilated_window",
    )(q, k, v, z)


def _compress_kernel(k_ref, v_ref, p_ref, w1_ref, b1_ref, w2k_ref, w2vt_ref, cos_ref, sa_ref, sb_ref,
                     kc_ref, vc_ref):
    n = kc_ref.shape[1]
    dh = k_ref.shape[-1]
    for s, (t_ref, o_ref) in enumerate(((k_ref, kc_ref), (v_ref, vc_ref))):
        lo = jnp.zeros((n, CMP_HID), F32)
        hi = jnp.zeros((n, CMP_HID), F32)
        for l in range(CMP_STRIDE):
            t = t_ref[0, pl.ds(l, n, stride=CMP_STRIDE), :]
            lo = lo + _dot((t + p_ref[s, l:l + 1, :]).astype(BF16),
                           w1_ref[s, l * dh:(l + 1) * dh, :])
            lh = l + CMP_STRIDE
            hi = hi + _dot((t + p_ref[s, lh:lh + 1, :]).astype(BF16),
                           w1_ref[s, lh * dh:(lh + 1) * dh, :])
        pre = lo + pltpu.roll(hi, n - 1, 0) + b1_ref[s]
        hid = 0.5 * pre * (1.0 + jnp.tanh(np.sqrt(2.0 / np.pi).astype(np.float32)
                                          * (pre + 0.044715 * (pre * pre * pre))))
        hid = hid.astype(BF16)
        if s == 0:
            out = _dot(hid, w2k_ref[...])
            o_ref[0] = _rope_lane_tile(out, cos_ref[...], sa_ref[...], sb_ref[...]).astype(BF16)
        else:
            o_ref[0] = _nt_dot(w2vt_ref[...], hid).astype(BF16)


def _compress(kcr, vcr, cmp_pos, cmp_w1, cmp_b1, cmp_w2, tables_c):
    B, S, dh = kcr.shape
    n = S // CMP_STRIDE
    w1 = cmp_w1.astype(BF16)
    w2k = jnp.concatenate([cmp_w2[0], cmp_w2[0]], axis=-1).astype(BF16)
    w2vt = cmp_w2[1].T.astype(BF16)
    b1 = cmp_b1[:, None, :]
    cos, sa, sb = tables_c
    const = lambda a: pl.BlockSpec(a.shape, lambda b: (0,) * a.ndim)
    tspec = pl.BlockSpec((1, S, dh), lambda b: (b, 0, 0))
    return pl.pallas_call(
        _compress_kernel,
        grid=(B,),
        in_specs=[tspec, tspec, const(cmp_pos), const(w1), const(b1), const(w2k), const(w2vt),
                  const(cos), const(sa), const(sb)],
        out_specs=[pl.BlockSpec((1, n, LANES), lambda b: (b, 0, 0)),
                   pl.BlockSpec((1, dh, n), lambda b: (b, 0, 0))],
        out_shape=[jax.ShapeDtypeStruct((B, n, LANES), BF16),
                   jax.ShapeDtypeStruct((B, dh, n), BF16)],
        compiler_params=pltpu.CompilerParams(
            dimension_semantics=("parallel",), vmem_limit_bytes=VMEM_LIMIT),
        name="compress_tokens",
    )(kcr, vcr, cmp_pos, w1, b1, w2k, w2vt, cos, sa, sb)


def _nsa_kernel(q_ref, kc_ref, vct_ref, ks_ref, vst_ref, kw_ref, vwt_ref, gc_ref, z_ref,
                ovt_ref, ext_ref, o_ref, tile_scr, *, tq, tk, n_sel):
    i = pl.program_id(1)
    Q = N_HEADS * tq
    is_a = _head_masks((tq, LANES))
    qs = (q_ref[0].astype(F32) * SCALE).astype(BF16)
    q4 = jnp.concatenate([_stack_pair(qs[:, :LANES], is_a), _stack_pair(qs[:, LANES:], is_a)], axis=0)
    heads = lambda x: jnp.concatenate([x] * N_HEADS, axis=1)
    qpos1 = i * tq + lax.broadcasted_iota(jnp.int32, (1, tq), 1)
    qpos = heads(qpos1)

    kpos0 = lax.broadcasted_iota(jnp.int32, (tk, Q), 0)
    J = (i * tq) // tk

    n_win = -(-WIN_LEN // tk) + 1
    s_w, vt_w = [], []
    for a in range(n_win):
        j_a = J - (n_win - 1) + a
        qrel = qpos - j_a * tk
        ok = jnp.broadcast_to(j_a >= 0, (tk, Q))
        if a == 0:
            ok = jnp.logical_and(ok, kpos0 > qrel - WIN_LEN)
        if a == n_win - 1:
            ok = kpos0 <= qrel
        j_c = jnp.maximum(j_a, 0)
        s_w.append(jnp.where(ok, _nt_dot(_tile_rows(kw_ref, j_c, tk), q4), NEG))
        vt_w.append(_tile_lanes(vwt_ref, j_c, tk))
    m_w = functools.reduce(jnp.maximum, [jnp.max(s, axis=0, keepdims=True) for s in s_w])
    p_w = [jnp.exp(s - m_w) for s in s_w]
    l_w = sum(jnp.sum(p, axis=0, keepdims=True) for p in p_w)
    o_win = sum(_dot(vt, p.astype(BF16)) for vt, p in zip(vt_w, p_w)) / l_w

    n_c = kc_ref.shape[1]
    c_end = lax.broadcasted_iota(jnp.int32, (n_c, Q), 0) * CMP_STRIDE + (CMP_LEN - 1)
    c_valid = c_end <= qpos
    sc = jnp.where(c_valid, _nt_dot(kc_ref[0], q4), NEG)
    pc = jnp.exp(sc - jnp.max(sc, axis=0, keepdims=True)) * jnp.where(c_valid, 1.0, 0.0)
    lc = jnp.sum(pc, axis=0, keepdims=True)
    pc = pc * jnp.where(lc > 0.0, 1.0 / lc, 0.0)
    o_cmp = _dot(vct_ref[0], pc.astype(BF16))

    pcs = pc[:, 0:tq] + pc[:, tq:2 * tq] + pc[:, 2 * tq:3 * tq] + pc[:, 3 * tq:]
    hi, lo = _split_bf16(pcs)
    imp = _dot(ovt_ref[...], hi) + _dot(ovt_ref[...], lo)
    nsp = imp.shape[0]
    blk = lax.broadcasted_iota(jnp.int32, (nsp, tq), 0)
    own = lax.shift_right_logical(qpos1, int(np.log2(SEL_LEN)))
    s_valid = blk <= own
    forced = jnp.logical_or(blk == 0, blk >= own - 1)
    imp = jnp.where(s_valid, jnp.where(forced, BIG, imp), NEG)
    rank = jnp.zeros((nsp, tq), F32)
    for s in range(n_sel):
        c = imp[s:s + 1, :]
        beats = jnp.logical_or(c > imp, jnp.logical_and(c == imp, blk > s))
        rank = rank + jnp.where(beats, 1.0, 0.0)
    chosen = jnp.logical_and(rank < float(min(SEL_TOPN, n_sel)), s_valid)
    sel = jnp.where(chosen, 1.0, 0.0).astype(BF16)

    def sel_scores(j, diag):
        bias = (_dot(_tile_rows_2d(ext_ref, j, tk), sel) - 1.0) * (-NEG)
        s = _nt_dot(_tile_rows(ks_ref, j, tk), q4) + heads(bias)
        if diag:
            s = jnp.where(j * tk + kpos0 <= qpos, s, NEG)
        return (j, jnp.max(s, axis=0, keepdims=True)), [s]

    def consume(blk_s, st):
        (j, s_max), (s,) = blk_s
        m, l, acc = st
        m_new = jnp.maximum(m, s_max)
        alpha = jnp.exp(m - m_new)
        p = jnp.exp(s - m_new)
        l = alpha * l + jnp.sum(p, axis=0, keepdims=True)
        acc = alpha * acc + _dot(_tile_lanes(vst_ref, j, tk), p.astype(BF16))
        return m_new, l, acc

    init = (jnp.full((1, Q), NEG, F32), jnp.zeros((1, Q), F32), jnp.zeros((HEAD_DIM, Q), F32))

    _, l_s, a_s = _pipelined(J, sel_scores(J, True), lambda t: sel_scores(t, False), consume, init,
                             tile_scr)
    o_sel = a_s / l_s

    gates = jax.nn.sigmoid(gc_ref[0]).T
    outs = []
    for h in range(N_HEADS):
        cols = slice(h * tq, (h + 1) * tq)
        outs.append(gates[3 * h:3 * h + 1, :] * o_cmp[:, cols]
                    + gates[3 * h + 1:3 * h + 2, :] * o_sel[:, cols]
                    + gates[3 * h + 2:3 * h + 3, :] * o_win[:, cols])
    out = jnp.concatenate(outs, axis=0).T
    o_ref[0] = (out * _silu(z_ref[0].astype(F32))).astype(BF16)


def _nsa(q, kc2, vct, ks2, vst, kw2, vwt, gc, z, tq=256, tk=256):
    B, S, _ = q.shape
    n_c = kc2.shape[1]
    n_sel = S // SEL_LEN
    nsp = -(-n_sel // SUBLANES) * SUBLANES
    n_cmp = (S - CMP_LEN) // CMP_STRIDE + 1
    c_start = np.arange(n_c) * CMP_STRIDE
    s_start = np.arange(nsp) * SEL_LEN
    overlap_t = np.clip(np.minimum(c_start[None, :] + CMP_LEN, s_start[:, None] + SEL_LEN)
                        - np.maximum(c_start[None, :], s_start[:, None]), 0, None) / CMP_LEN
    overlap_t[:, n_cmp:] = 0.0
    overlap_t[n_sel:, :] = 0.0
    expand_t = (np.arange(S)[:, None] // SEL_LEN == np.arange(nsp)[None, :]).astype(np.float32)
    row = lambda w: pl.BlockSpec((1, tq, w), lambda b, i: (b, i, 0))
    full = lambda a: pl.BlockSpec((1,) + a.shape[1:], lambda b, i: (b, 0, 0))
    const = lambda a: pl.BlockSpec(a.shape, lambda b, i: (0, 0))
    ovt = jnp.asarray(overlap_t, BF16)
    ext = jnp.asarray(expand_t, BF16)
    return pl.pallas_call(
        functools.partial(_nsa_kernel, tq=tq, tk=tk, n_sel=n_sel),
        grid=(B, S // tq),
        in_specs=[row(MIX_W), full(kc2), full(vct), full(ks2), full(vst), full(kw2), full(vwt),
                  row(LANES), row(MIX_W), const(ovt), const(ext)],
        out_specs=row(MIX_W),
        out_shape=jax.ShapeDtypeStruct((B, S, MIX_W), BF16),
        scratch_shapes=[pltpu.VMEM((2, 1, tk, N_HEADS * tq), F32)],
        compiler_params=pltpu.CompilerParams(
            dimension_semantics=("parallel", "parallel"), vmem_limit_bytes=VMEM_LIMIT),
        name="native_sparse",
    )(q, kc2, vct, ks2, vst, kw2, vwt, gc, z, ovt, ext)


def _moba_kernel(q_ref, k_ref, vt_ref, z_ref, o_ref, km_scr, bias_scr, tile_scr, *, n_blk):
    i = pl.program_id(1)
    tq = tk = MOBA_BLOCK
    nb_pad = km_scr.shape[0]
    is_a = _head_masks((tq, LANES))

    @pl.when(i == 0)
    def _():
        km_scr[...] = jnp.zeros_like(km_scr)
        for b in range(n_blk):
            kb = k_ref[0, b * tk:(b + 1) * tk, :].astype(F32)
            km_scr[b:b + 1, :] = jnp.mean(kb, axis=0, keepdims=True)

    q = q_ref[0]
    qs = (q.astype(F32) * SCALE).astype(BF16)
    blk = lax.broadcasted_iota(jnp.int32, (nb_pad, 2 * tq), 0)
    n_top = min(MOBA_TOPK, max(n_blk - 1, 1))
    qcat = []
    for p in range(N_PAIRS):
        qcat.append(_stack_pair(qs[:, _pair_lanes(p)], is_a))
        qgcat = _stack_pair(q[:, _pair_lanes(p)], is_a)
        km_hi, km_lo = _split_bf16(km_scr[:, _pair_lanes(p)])
        gsc = _nt_dot(km_hi, qgcat) + _nt_dot(km_lo, qgcat)
        gsc = jnp.where(blk < i, gsc, NEG)
        rank = jnp.zeros((nb_pad, 2 * tq), F32)
        for s in range(n_blk):
            c = gsc[s:s + 1, :]
            beats = jnp.logical_or(c > gsc, jnp.logical_and(c == gsc, blk > s))
            rank = rank + jnp.where(beats, 1.0, 0.0)
        attends = jnp.logical_or(jnp.logical_and(rank < float(n_top), blk < i), blk == i)
        bias_scr[p] = jnp.where(attends, 0.0, NEG)

    causal1 = lax.broadcasted_iota(jnp.int32, (tk, tq), 0) <= lax.broadcasted_iota(jnp.int32, (tk, tq), 1)
    causal = jnp.concatenate([causal1, causal1], axis=1)

    def scores(j, diag):
        ks = _tile_rows(k_ref, j, tk)
        ss = [_nt_dot(ks[:, _pair_lanes(p)], qcat[p]) for p in range(N_PAIRS)]
        if diag:
            ss = [jnp.where(causal, s, NEG) for s in ss]
        return (j, [jnp.max(s, axis=0, keepdims=True) for s in ss]), ss

    def consume(blk_s, st):
        (j, maxes), ss = blk_s
        vts = _tile_lanes(vt_ref, j, tk)
        pr = range(N_PAIRS)
        bias = [bias_scr[p, pl.ds(j, 1), :] for p in pr]
        m_new = [jnp.maximum(st[p][0], maxes[p] + bias[p]) for p in pr]
        ps = [jnp.exp(ss[p] - (m_new[p] - bias[p])) for p in pr]
        pvs = [_pv_pair(vts[_pair_lanes(p)], ps[p].astype(BF16), tq) for p in pr]
        out = []
        for p in pr:
            m, l, acc = st[p]
            alpha = jnp.exp(m - m_new[p])
            out.append((m_new[p], alpha * l + jnp.sum(ps[p], axis=0, keepdims=True),
                        alpha * acc + pvs[p]))
        return out

    init = [(jnp.full((1, 2 * tq), NEG, F32), jnp.zeros((1, 2 * tq), F32),
             jnp.zeros((HEAD_DIM, 2 * tq), F32))] * N_PAIRS
    st = _pipelined(i, scores(i, True), lambda t: scores(t, False), consume, init, tile_scr)
    _finish_pairs([acc / l for _, l, acc in st], tq, z_ref, o_ref)


def _moba(q, k, vt, z):
    B, S, _ = q.shape
    tq = MOBA_BLOCK
    n_blk = S // MOBA_BLOCK
    nb_pad = -(-n_blk // SUBLANES) * SUBLANES
    qspec, kspec, vtspec = _all_pair_specs(S, tq)
    return pl.pallas_call(
        functools.partial(_moba_kernel, n_blk=n_blk),
        grid=(B, S // tq),
        in_specs=[qspec, kspec, vtspec, qspec],
        out_specs=qspec,
        out_shape=jax.ShapeDtypeStruct((B, S, MIX_W), BF16),
        scratch_shapes=[pltpu.VMEM((nb_pad, MIX_W), F32),
                        pltpu.VMEM((N_PAIRS, nb_pad, 2 * tq), F32),
                        pltpu.VMEM((2, N_PAIRS, tq, 2 * tq), F32)],
        compiler_params=pltpu.CompilerParams(
            dimension_semantics=("parallel", "arbitrary"), vmem_limit_bytes=VMEM_LIMIT),
        name="moba",
    )(q, k, vt, z)


def _out_kernel(ma_ref, mb_ref, mc_ref, md_ref, w_ref, x_ref, mod_ref, g_ref, o_ref):
    D = x_ref.shape[-1]
    acc = _dot(ma_ref[0], w_ref[0:MIX_W, :])
    acc = acc + _dot(mb_ref[0], w_ref[MIX_W:2 * MIX_W, :])
    acc = acc + _dot(mc_ref[0], w_ref[2 * MIX_W:3 * MIX_W, :])
    acc = acc + _dot(md_ref[0], w_ref[3 * MIX_W:, :])
    ms = jnp.mean(acc * acc, axis=-1, keepdims=True)
    y = acc * lax.rsqrt(ms + EPS) * g_ref[...]
    o_ref[0] = x_ref[0] + mod_ref[0, :, 2 * D:] * y


def _out_projection(mixed, w_out, x, mod, g_post, tm):
    B, S, D = x.shape
    row = lambda w: pl.BlockSpec((1, tm, w), lambda b, i: (b, i, 0))
    return pl.pallas_call(
        _out_kernel,
        grid=(B, S // tm),
        in_specs=[row(MIX_W)] * 4 + [pl.BlockSpec(w_out.shape, lambda b, i: (0, 0)), row(D),
                                     pl.BlockSpec((1, 1, 3 * D), lambda b, i: (b, 0, 0)),
                                     pl.BlockSpec((1, D), lambda b, i: (0, 0))],
        out_specs=row(D),
        out_shape=jax.ShapeDtypeStruct((B, S, D), F32),
        compiler_params=pltpu.CompilerParams(
            dimension_semantics=("parallel", "parallel"), vmem_limit_bytes=VMEM_LIMIT),
        name="out_projection",
    )(*mixed, w_out.astype(BF16), x, mod[:, None, :], g_post[None])


def _layer(x, c, norm_pre, norm_post, w_mod, b_mod, w_in, w_out, cmp_pos, cmp_w1, cmp_b1, cmp_w2,
           tables, tables_c, tm):
    mod = _modulation(c, w_mod, b_mod)
    w_cat, w_t = _pack_w_in(w_in)
    (qa, ka, za, qb, kb, vb, zb, qc, kcr, vcr, gc, ks2, kw2, zc,
     qd, kd, zd, vat, vdt, vst, vwt) = _projection(x, mod, norm_pre, w_cat, w_t, tables, tm)
    oa = _stick_breaking(qa, ka, vat, za)
    ob = _dilated(qb, kb, vb, zb)
    kc2, vct = _compress(kcr, vcr, cmp_pos, cmp_w1, cmp_b1, cmp_w2, tables_c)
    oc = _nsa(qc, kc2, vct, ks2, vst, kw2, vwt, gc, zc)
    od = _moba(qd, kd, vdt, zd)
    return _out_projection((oa, ob, oc, od), w_out, x, mod, norm_post, tm)


def kernel(x, c, norm_pre, norm_post, w_mod, b_mod, w_in, w_out, cmp_pos, cmp_w1, cmp_b1, cmp_w2):
    S = x.shape[1]
    tables = _rope_lane_tables(np.arange(S))
    tables_c = _rope_lane_tables(np.arange(S // CMP_STRIDE) * CMP_STRIDE + CMP_LEN - 1)
    tm = min(512, S)
    for l in range(norm_pre.shape[0]):
        x = _layer(x, c, norm_pre[l], norm_post[l], w_mod[l], b_mod[l], w_in[l], w_out[l],
                   cmp_pos[l], cmp_w1[l], cmp_b1[l], cmp_w2[l], tables, tables_c, tm)
    return x
```

```python
import functools

import numpy as np
import jax
import jax.numpy as jnp
from jax import lax
from jax.experimental import pallas as pl
from jax.experimental.pallas import tpu as pltpu

F32 = jnp.float32
BF16 = jnp.bfloat16

N_HEADS = 4
HEAD_DIM = 64
MIX_W = N_HEADS * HEAD_DIM
ROPE_THETA = 500000.0
ROPE_DIM = HEAD_DIM // 4
ROPE_HALF = ROPE_DIM // 2
EPS = 1e-6
NEG = -1e30
BIG = 1e9
SCALE = HEAD_DIM ** -0.5
LOG2E = 1.4426950408889634
SCALE_LOG2 = SCALE * LOG2E
DIL_CONFIGS = ((128, 1), (512, 4), (2048, 16))
CMP_LEN = 32
CMP_STRIDE = 16
CMP_HID = 256
SEL_LEN = 64
SEL_TOPN = 16
WIN_LEN = 512
MOBA_BLOCK = 256
MOBA_TOPK = 3

N_PAIRS = N_HEADS // 2
LANES = 128
SUBLANES = 8
TILE_W = 2 * LANES
KEY_TILE = 256
DIL_GROUP = 2048
VMEM_LIMIT = 56 * 1024 * 1024


def _nt_dot(a, b):
    return lax.dot_general(a, b, (((1,), (1,)), ((), ())), preferred_element_type=F32)


def _dot(a, b):
    return jnp.dot(a, b, preferred_element_type=F32)


def _split_bf16(x):
    hi = x.astype(BF16)
    lo = (x - hi.astype(F32)).astype(BF16)
    return hi, lo


def _silu(x):
    return x * jax.nn.sigmoid(x)


def _head_masks(shape):
    lane = lax.broadcasted_iota(jnp.int32, shape, 1)
    return lane < HEAD_DIM


def _pv_pair(vt, p, tq):
    return jnp.concatenate([_dot(vt[:HEAD_DIM], p[:, :tq]), _dot(vt[HEAD_DIM:], p[:, tq:])], axis=1)


def _mod_kernel(c_ref, w_ref, b_ref, o_ref):
    c = c_ref[...]
    o_ref[...] = jnp.dot(_silu(c), w_ref[...], preferred_element_type=F32,
                         precision=lax.Precision.HIGHEST) + b_ref[...]


def _modulation(c, w_mod, b_mod):
    B, D = c.shape
    N = w_mod.shape[1]
    rows = SUBLANES
    cp = jnp.zeros((rows, D), F32).at[:B].set(c)
    tn = N // 4
    out = pl.pallas_call(
        _mod_kernel,
        grid=(N // tn,),
        in_specs=[pl.BlockSpec((rows, D), lambda j: (0, 0)),
                  pl.BlockSpec((D, tn), lambda j: (0, j)),
                  pl.BlockSpec((1, tn), lambda j: (0, j))],
        out_specs=pl.BlockSpec((rows, tn), lambda j: (0, j)),
        out_shape=jax.ShapeDtypeStruct((rows, N), F32),
        name="modulation",
    )(cp, w_mod, b_mod[None])
    return out[:B]


def _rope_lane_tile(y, cos, sa, sb):
    return y * cos + pltpu.roll(y, LANES - ROPE_HALF, 1) * sa + pltpu.roll(y, ROPE_HALF, 1) * sb


def _proj_kernel(x_ref, mod_ref, g_ref, w_ref, wt_ref, cos_ref, sa_ref, sb_ref,
                 qa_ref, ka_ref, za_ref,
                 qb_ref, kb_ref, vb_ref, zb_ref,
                 qc_ref, kcr_ref, vcr_ref, gc_ref, ks_ref, kw_ref, zc_ref,
                 qd_ref, kd_ref, zd_ref, vat_ref, vdt_ref, vst_ref, vwt_ref):
    D = x_ref.shape[-1]
    x = x_ref[0]
    ms = jnp.mean(x * x, axis=-1, keepdims=True)
    y = x * lax.rsqrt(ms + EPS) * g_ref[...]
    shift = mod_ref[0, :, 0:D]
    scale = mod_ref[0, :, D:2 * D]
    h = (y * (1.0 + scale) + shift).astype(BF16)
    cos, sa, sb = cos_ref[...], sa_ref[...], sb_ref[...]

    def tile(t):
        return _dot(h, w_ref[:, t * TILE_W:(t + 1) * TILE_W])

    def rope_lo(y):
        return _rope_lane_tile(y[:, :LANES], cos, sa, sb)

    def rope_all(y):
        return jnp.concatenate([rope_lo(y), _rope_lane_tile(y[:, LANES:], cos, sa, sb)], axis=1)

    T = PROJ_TILES
    qa_ref[0] = tile(T["qa"]).astype(BF16)
    ka_ref[0] = tile(T["ka"]).astype(BF16)
    za_ref[0] = tile(T["za"]).astype(BF16)
    qb_ref[0] = rope_all(tile(T["qb"]))
    kb_ref[0] = rope_all(tile(T["kb"]))
    vb_ref[0] = tile(T["vb"])
    zb_ref[0] = tile(T["zb"]).astype(BF16)
    qc_ref[0] = (rope_all(tile(T["qc"])) * SCALE_LOG2).astype(BF16)
    small = tile(T["c_small"])
    kcr_ref[0] = small[:, 0:HEAD_DIM]
    vcr_ref[0] = small[:, HEAD_DIM:LANES]
    gc_ref[0] = small[:, LANES:]
    keys = rope_all(tile(T["c_keys"])).astype(BF16)
    ks_ref[0] = keys[:, :LANES]
    kw_ref[0] = keys[:, LANES:]
    zc_ref[0] = tile(T["zc"]).astype(BF16)
    qd_ref[0] = (rope_all(tile(T["qd"])) * SCALE_LOG2).astype(BF16)
    kd_ref[0] = rope_all(tile(T["kd"])).astype(BF16)
    zd_ref[0] = tile(T["zd"]).astype(BF16)
    vt = _nt_dot(wt_ref[...], h).astype(BF16)
    vat_ref[0] = vt[0:MIX_W]
    vdt_ref[0] = vt[MIX_W:2 * MIX_W]
    vst_ref[0] = vt[2 * MIX_W:2 * MIX_W + HEAD_DIM]
    vwt_ref[0] = vt[2 * MIX_W + HEAD_DIM:]


PROJ_TILES = dict(qa=0, ka=1, za=3, qb=4, kb=5, vb=6, zb=7, qc=8, c_small=9, c_keys=10,
                  zc=11, qd=12, kd=13, zd=15)


def _pack_w_in(w_in):
    D = w_in.shape[0]
    w = w_in.astype(BF16)
    offs = np.cumsum([0] + [MIX_W] * 9 + [HEAD_DIM] * 6 + [3 * N_HEADS] + [MIX_W] * 5)
    seg = lambda i: w[:, offs[i]:offs[i + 1]]
    span = lambda i, j: w[:, offs[i]:offs[j]]
    w_cat = jnp.concatenate(
        [span(0, 9),
         span(9, 11), seg(15), jnp.zeros((D, LANES - 3 * N_HEADS), BF16),
         seg(11), seg(11), seg(13), seg(13),
         span(16, 21)], axis=1)
    w_t = jnp.concatenate([seg(2), seg(19), seg(12), seg(14)], axis=1).T
    return w_cat, w_t


def _rope_angles(pos):
    inv_freq = 1.0 / (ROPE_THETA ** (np.arange(0, ROPE_DIM, 2, dtype=np.float32) / ROPE_DIM))
    ang = jnp.asarray(pos, F32)[:, None] * jnp.asarray(inv_freq, F32)[None, :]
    return jnp.cos(ang), jnp.sin(ang)


def _rope_lane_tables(pos):
    cos, sin = _rope_angles(pos)
    n = cos.shape[0]
    rest = HEAD_DIM - ROPE_DIM
    cos_h = jnp.concatenate([cos, cos, jnp.ones((n, rest), F32)], axis=1)
    sa_h = jnp.concatenate([-sin, jnp.zeros((n, HEAD_DIM - ROPE_HALF), F32)], axis=1)
    sb_h = jnp.concatenate([jnp.zeros((n, ROPE_HALF), F32), sin, jnp.zeros((n, rest), F32)], axis=1)
    two = lambda t: jnp.concatenate([t, t], axis=1)
    return two(cos_h), two(sa_h), two(sb_h)


def _projection(x, mod, g_pre, w_cat, w_t, tables, tm):
    B, S, D = x.shape
    cos, sa, sb = tables
    row = lambda w: pl.BlockSpec((1, tm, w), lambda b, i: (b, i, 0))
    tab = pl.BlockSpec((tm, LANES), lambda b, i: (i, 0))
    const = lambda a: pl.BlockSpec(a.shape, lambda b, i: (0,) * a.ndim)
    widths = [(MIX_W, BF16)] * 3 + [(MIX_W, F32)] * 3 + [(MIX_W, BF16)] + \
             [(MIX_W, BF16), (HEAD_DIM, F32), (HEAD_DIM, F32), (LANES, F32),
              (LANES, BF16), (LANES, BF16), (MIX_W, BF16)] + \
             [(MIX_W, BF16)] * 3
    t_rows = (MIX_W, MIX_W, HEAD_DIM, HEAD_DIM)
    t_specs = [pl.BlockSpec((1, r, tm), lambda b, i: (b, 0, i)) for r in t_rows]
    t_shapes = [jax.ShapeDtypeStruct((B, r, S), BF16) for r in t_rows]
    return pl.pallas_call(
        _proj_kernel,
        grid=(B, S // tm),
        in_specs=[row(D),
                  pl.BlockSpec((1, 1, 3 * D), lambda b, i: (b, 0, 0)),
                  pl.BlockSpec((1, D), lambda b, i: (0, 0)),
                  const(w_cat), const(w_t), tab, tab, tab],
        out_specs=[row(w) for w, _ in widths] + t_specs,
        out_shape=[jax.ShapeDtypeStruct((B, S, w), dt) for w, dt in widths] + t_shapes,
        compiler_params=pltpu.CompilerParams(
            dimension_semantics=("parallel", "parallel"), vmem_limit_bytes=VMEM_LIMIT),
        name="in_projection",
    )(x, mod[:, None, :], g_pre[None], w_cat, w_t, cos, sa, sb)


def _all_pair_specs(S, tq):
    qspec = pl.BlockSpec((1, tq, MIX_W), lambda b, i: (b, i, 0))
    kspec = pl.BlockSpec((1, S, MIX_W), lambda b, i: (b, 0, 0))
    vtspec = pl.BlockSpec((1, MIX_W, S), lambda b, i: (b, 0, 0))
    return qspec, kspec, vtspec


def _pair_lanes(p):
    return slice(p * LANES, (p + 1) * LANES)


def _tile_rows(ref, j, tk):
    return ref[0, pl.ds(pl.multiple_of(j * tk, tk), tk), :]


def _tile_rows_2d(ref, j, tk):
    return ref[pl.ds(pl.multiple_of(j * tk, tk), tk), :]


def _tile_lanes(ref, j, tk):
    return ref[0, :, pl.ds(pl.multiple_of(j * tk, tk), tk)]


def _stack_pair(q, is_a):
    return jnp.concatenate([jnp.where(is_a, q, 0), jnp.where(is_a, 0, q)], axis=0)


def _pipelined(n, first, produce, consume, state, scr):
    def put(slot, tiles):
        for a, tile in enumerate(tiles):
            scr[slot, a] = tile

    def get(slot):
        return [scr[slot, a] for a in range(scr.shape[1])]

    def step(t, slot, carry):
        small, st = carry
        small_new, tiles = produce(t)
        put(1 - slot, tiles)
        return small_new, consume((small, get(slot)), st)

    put(0, first[1])
    carry = lax.fori_loop(0, lax.shift_right_logical(n, 1),
                          lambda u, c: step(2 * u + 1, 1, step(2 * u, 0, c)), (first[0], state))

    def odd_tail(c):
        small, st = step(n - 1, 0, c)
        return consume((small, get(1)), st)

    return lax.cond(jnp.bitwise_and(n, 1) == 1, odd_tail,
                    lambda c: consume((c[0], get(0)), c[1]), carry)


def _finish_pairs(accs, tq, z_ref, o_ref):
    heads = [a[:, h * tq:(h + 1) * tq] for a in accs for h in range(2)]
    out = jnp.concatenate(heads, axis=0).T
    o_ref[0] = (out * _silu(z_ref[0].astype(F32))).astype(BF16)


def _stick_kernel(q_ref, k_ref, vt_ref, z_ref, o_ref, tile_scr, *, tq):
    i = pl.program_id(1)
    tk = KEY_TILE
    n_diag = tq // tk
    is_a = _head_masks((tq, LANES))
    qs = (q_ref[0].astype(F32) * SCALE).astype(BF16)
    qcat = [_stack_pair(qs[:, _pair_lanes(p)], is_a) for p in range(N_PAIRS)]
    r_i = lax.broadcasted_iota(jnp.int32, (tk, tk), 0)
    c_i = lax.broadcasted_iota(jnp.int32, (tk, tk), 1)
    suffix = jnp.where(c_i >= r_i, 1.0, 0.0).astype(BF16)
    kloc = lax.broadcasted_iota(jnp.int32, (tk, 2 * tq), 0)
    qloc = jnp.bitwise_and(lax.broadcasted_iota(jnp.int32, (1, 2 * tq), 1), tq - 1)

    def log_weights(j, diag):
        ks = _tile_rows(k_ref, j, tk)
        zs = [_nt_dot(ks[:, _pair_lanes(p)], qcat[p]) for p in range(N_PAIRS)]
        sps = [jnp.maximum(z, 0.0) + jnp.log(1.0 + jnp.exp2(jnp.abs(z) * (-LOG2E))) for z in zs]
        if diag:
            past = kloc < qloc - (j * tk - i * tq)
            sps = [jnp.where(past, sp, 0.0) for sp in sps]
        css = [_dot(suffix, sp.astype(BF16)) for sp in sps]
        xs = [z - cs for z, cs in zip(zs, css)]
        if diag:
            xs = [jnp.where(past, x, NEG) for x in xs]
        return (j, [cs[0:1, :] for cs in css]), xs

    def accumulate(blk, st):
        (j, tots), xs = blk
        vts = _tile_lanes(vt_ref, j, tk)
        ws = [jnp.exp(xs[p] - st[p][1]).astype(BF16) for p in range(N_PAIRS)]
        return [(st[p][0] + _pv_pair(vts[_pair_lanes(p)], ws[p], tq), st[p][1] + tots[p])
                for p in range(N_PAIRS)]

    st = [(jnp.zeros((HEAD_DIM, 2 * tq), F32), jnp.zeros((1, 2 * tq), F32))] * N_PAIRS
    top = i * n_diag + n_diag - 1
    for d in range(n_diag - 1):
        st = accumulate(log_weights(top - d, True), st)
    st = _pipelined(i * n_diag, log_weights(i * n_diag, True),
                    lambda t: log_weights(i * n_diag - 1 - t, False), accumulate, st, tile_scr)
    _finish_pairs([acc for acc, _ in st], tq, z_ref, o_ref)


def _stick_breaking(q, k, vt, z, tq=512):
    B, S, _ = q.shape
    tq = min(tq, S)
    qspec, kspec, vtspec = _all_pair_specs(S, tq)
    return pl.pallas_call(
        functools.partial(_stick_kernel, tq=tq),
        grid=(B, S // tq),
        in_specs=[qspec, kspec, vtspec, qspec],
        out_specs=qspec,
        out_shape=jax.ShapeDtypeStruct((B, S, MIX_W), BF16),
        scratch_shapes=[pltpu.VMEM((2, N_PAIRS, KEY_TILE, 2 * tq), F32)],
        compiler_params=pltpu.CompilerParams(
            dimension_semantics=("parallel", "arbitrary"), vmem_limit_bytes=VMEM_LIMIT),
        name="stick_breaking",
    )(q, k, vt, z)


def _dil_kernel(q_ref, k_ref, v_ref, z_ref, o_ref, m_scr, l_scr, a_scr, *, G):
    g = pl.program_id(2)
    blk = LANES
    is_a = _head_masks((blk, LANES))
    qrow = jnp.bitwise_and(lax.broadcasted_iota(jnp.int32, (2 * blk, 2 * blk), 0), blk - 1)
    col = lax.broadcasted_iota(jnp.int32, (2 * blk, 2 * blk), 1)
    band = jnp.logical_and(col >= qrow, col <= qrow + blk)
    n_sub = G // blk
    unroll = 8

    for ci, (_, d) in enumerate(DIL_CONFIGS):
        nb = G // (blk * d)
        sh = int(np.log2(nb))

        def sub(n, ci=ci, d=d, nb=nb, sh=sh):
            r = lax.shift_right_logical(n, sh)
            ub = jnp.bitwise_and(n, nb - 1)
            loc = ub * (blk * d) + r
            glob = g * G + loc
            has_prev = glob >= blk * d
            pstart = jnp.maximum(glob - blk * d, r)
            qs = (q_ref[0, pl.ds(loc, blk, stride=d), :] * SCALE_LOG2).astype(BF16)
            kk = jnp.concatenate([k_ref[0, pl.ds(pstart, blk, stride=d), :],
                                  k_ref[0, pl.ds(glob, blk, stride=d), :]], axis=0).astype(BF16)
            vv = jnp.concatenate([v_ref[0, pl.ds(pstart, blk, stride=d), :],
                                  v_ref[0, pl.ds(glob, blk, stride=d), :]], axis=0).astype(BF16)
            ok = jnp.logical_and(band, jnp.logical_or(col >= blk, has_prev))
            s = jnp.where(ok, _nt_dot(_stack_pair(qs, is_a), kk), NEG)
            m = jnp.max(s, axis=-1, keepdims=True)
            p = jnp.exp2(s - m)
            l = jnp.sum(p, axis=-1, keepdims=True)
            acc = _dot(p.astype(BF16), vv)
            rows = pl.ds(loc, blk, stride=d)
            m_scr[ci, rows, :] = jnp.where(is_a, m[:blk], m[blk:])
            l_scr[ci, rows, :] = jnp.where(is_a, l[:blk], l[blk:])
            a_scr[ci, rows, :] = jnp.where(is_a, acc[:blk], acc[blk:])

        def trip(t, _, sub=sub):
            for u in range(unroll):
                sub(t * unroll + u)
            return 0

        lax.fori_loop(0, n_sub // unroll, trip, 0)

    chunk = 256

    def combine(c, _):
        rows = pl.ds(pl.multiple_of(c * chunk, chunk), chunk)
        m0, m1, m2 = m_scr[0, rows, :], m_scr[1, rows, :], m_scr[2, rows, :]
        mm = jnp.maximum(jnp.maximum(m0, m1), m2)
        w0, w1, w2 = jnp.exp2(m0 - mm), jnp.exp2(m1 - mm), jnp.exp2(m2 - mm)
        num = w0 * a_scr[0, rows, :] + w1 * a_scr[1, rows, :] + w2 * a_scr[2, rows, :]
        den = w0 * l_scr[0, rows, :] + w1 * l_scr[1, rows, :] + w2 * l_scr[2, rows, :]
        o_ref[0, rows, :] = (num / den * _silu(z_ref[0, rows, :].astype(F32))).astype(BF16)
        return 0

    lax.fori_loop(0, G // chunk, combine, 0)


def _dilated(q, k, v, z):
    B, S, _ = q.shape
    G = min(DIL_GROUP, S)
    gspec = pl.BlockSpec((1, G, LANES), lambda b, p, g: (b, g, p))
    kspec = pl.BlockSpec((1, S, LANES), lambda b, p, g: (b, 0, p))
    scr = pltpu.VMEM((len(DIL_CONFIGS), G, LANES), F32)
    return pl.pallas_call(
        functools.partial(_dil_kernel, G=G),
        grid=(B, 2, S // G),
        in_specs=[gspec, kspec, kspec, gspec],
        out_specs=gspec,
        out_shape=jax.ShapeDtypeStruct((B, S, MIX_W), BF16),
        scratch_shapes=[scr, scr, scr],
        compiler_params=pltpu.CompilerParams(
            dimension_semantics=("parallel", "parallel", "arbitrary"), vmem_limit_bytes=VMEM_LIMIT),
        name="dilated_window",
    )(q, k, v, z)


def _compress_kernel(k_ref, v_ref, p_ref, w1_ref, b1_ref, w2k_ref, w2vt_ref, cos_ref, sa_ref, sb_ref,
                     kc_ref, vc_ref):
    n = kc_ref.shape[1]
    dh = k_ref.shape[-1]
    for s, (t_ref, o_ref) in enumerate(((k_ref, kc_ref), (v_ref, vc_ref))):
        lo = jnp.zeros((n, CMP_HID), F32)
        hi = jnp.zeros((n, CMP_HID), F32)
        for l in range(CMP_STRIDE):
            t = t_ref[0, pl.ds(l, n, stride=CMP_STRIDE), :]
            lo = lo + _dot((t + p_ref[s, l:l + 1, :]).astype(BF16),
                           w1_ref[s, l * dh:(l + 1) * dh, :])
            lh = l + CMP_STRIDE
            hi = hi + _dot((t + p_ref[s, lh:lh + 1, :]).astype(BF16),
                           w1_ref[s, lh * dh:(lh + 1) * dh, :])
        pre = lo + pltpu.roll(hi, n - 1, 0) + b1_ref[s]
        hid = 0.5 * pre * (1.0 + jnp.tanh(np.sqrt(2.0 / np.pi).astype(np.float32)
                                          * (pre + 0.044715 * (pre * pre * pre))))
        hid = hid.astype(BF16)
        if s == 0:
            out = _dot(hid, w2k_ref[...])
            o_ref[0] = _rope_lane_tile(out, cos_ref[...], sa_ref[...], sb_ref[...]).astype(BF16)
        else:
            o_ref[0] = _nt_dot(w2vt_ref[...], hid).astype(BF16)


def _compress(kcr, vcr, cmp_pos, cmp_w1, cmp_b1, cmp_w2, tables_c):
    B, S, dh = kcr.shape
    n = S // CMP_STRIDE
    w1 = cmp_w1.astype(BF16)
    w2k = jnp.concatenate([cmp_w2[0], cmp_w2[0]], axis=-1).astype(BF16)
    w2vt = cmp_w2[1].T.astype(BF16)
    b1 = cmp_b1[:, None, :]
    cos, sa, sb = tables_c
    const = lambda a: pl.BlockSpec(a.shape, lambda b: (0,) * a.ndim)
    tspec = pl.BlockSpec((1, S, dh), lambda b: (b, 0, 0))
    return pl.pallas_call(
        _compress_kernel,
        grid=(B,),
        in_specs=[tspec, tspec, const(cmp_pos), const(w1), const(b1), const(w2k), const(w2vt),
                  const(cos), const(sa), const(sb)],
        out_specs=[pl.BlockSpec((1, n, LANES), lambda b: (b, 0, 0)),
                   pl.BlockSpec((1, dh, n), lambda b: (b, 0, 0))],
        out_shape=[jax.ShapeDtypeStruct((B, n, LANES), BF16),
                   jax.ShapeDtypeStruct((B, dh, n), BF16)],
        compiler_params=pltpu.CompilerParams(
            dimension_semantics=("parallel",), vmem_limit_bytes=VMEM_LIMIT),
        name="compress_tokens",
    )(kcr, vcr, cmp_pos, w1, b1, w2k, w2vt, cos, sa, sb)


def _nsa_kernel(q_ref, kc_ref, vct_ref, ks_ref, vst_ref, kw_ref, vwt_ref, gc_ref, z_ref,
                ovt_ref, ext_ref, o_ref, tile_scr, *, tq, tk, n_sel):
    i = pl.program_id(1)
    Q = N_HEADS * tq
    is_a = _head_masks((tq, LANES))
    qs = q_ref[0]
    q4 = jnp.concatenate([_stack_pair(qs[:, :LANES], is_a), _stack_pair(qs[:, LANES:], is_a)], axis=0)
    heads = lambda x: jnp.concatenate([x] * N_HEADS, axis=1)
    qpos1 = i * tq + lax.broadcasted_iota(jnp.int32, (1, tq), 1)
    qpos = heads(qpos1)

    kpos0 = lax.broadcasted_iota(jnp.int32, (tk, Q), 0)
    J = (i * tq) // tk

    n_win = -(-WIN_LEN // tk) + 1
    s_w, vt_w = [], []
    for a in range(n_win):
        j_a = J - (n_win - 1) + a
        qrel = qpos - j_a * tk
        ok = jnp.broadcast_to(j_a >= 0, (tk, Q))
        if a == 0:
            ok = jnp.logical_and(ok, kpos0 > qrel - WIN_LEN)
        if a == n_win - 1:
            ok = kpos0 <= qrel
        j_c = jnp.maximum(j_a, 0)
        s_w.append(jnp.where(ok, _nt_dot(_tile_rows(kw_ref, j_c, tk), q4), NEG))
        vt_w.append(_tile_lanes(vwt_ref, j_c, tk))
    m_w = functools.reduce(jnp.maximum, [jnp.max(s, axis=0, keepdims=True) for s in s_w])
    p_w = [jnp.exp2(s - m_w) for s in s_w]
    l_w = sum(jnp.sum(p, axis=0, keepdims=True) for p in p_w)
    o_win = sum(_dot(vt, p.astype(BF16)) for vt, p in zip(vt_w, p_w)) / l_w

    n_c = kc_ref.shape[1]
    c_end = lax.broadcasted_iota(jnp.int32, (n_c, Q), 0) * CMP_STRIDE + (CMP_LEN - 1)
    c_valid = c_end <= qpos
    sc = jnp.where(c_valid, _nt_dot(kc_ref[0], q4), NEG)
    pc = jnp.exp2(sc - jnp.max(sc, axis=0, keepdims=True)) * jnp.where(c_valid, 1.0, 0.0)
    lc = jnp.sum(pc, axis=0, keepdims=True)
    pc = pc * jnp.where(lc > 0.0, 1.0 / lc, 0.0)
    o_cmp = _dot(vct_ref[0], pc.astype(BF16))

    pcs = pc[:, 0:tq] + pc[:, tq:2 * tq] + pc[:, 2 * tq:3 * tq] + pc[:, 3 * tq:]
    hi, lo = _split_bf16(pcs)
    imp = _dot(ovt_ref[...], hi) + _dot(ovt_ref[...], lo)
    nsp = imp.shape[0]
    blk = lax.broadcasted_iota(jnp.int32, (nsp, tq), 0)
    own = lax.shift_right_logical(qpos1, int(np.log2(SEL_LEN)))
    s_valid = blk <= own
    forced = jnp.logical_or(blk == 0, blk >= own - 1)
    imp = jnp.where(s_valid, jnp.where(forced, BIG, imp), NEG)
    rank = jnp.zeros((nsp, tq), F32)
    for s in range(n_sel):
        c = imp[s:s + 1, :]
        beats = jnp.logical_or(c > imp, jnp.logical_and(c == imp, blk > s))
        rank = rank + jnp.where(beats, 1.0, 0.0)
    chosen = jnp.logical_and(rank < float(min(SEL_TOPN, n_sel)), s_valid)
    sel = jnp.where(chosen, 1.0, 0.0).astype(BF16)

    def sel_scores(j, diag):
        bias = (_dot(_tile_rows_2d(ext_ref, j, tk), sel) - 1.0) * (-NEG)
        s = _nt_dot(_tile_rows(ks_ref, j, tk), q4) + heads(bias)
        if diag:
            s = jnp.where(j * tk + kpos0 <= qpos, s, NEG)
        return (j, jnp.max(s, axis=0, keepdims=True)), [s]

    def consume(blk_s, st):
        (j, s_max), (s,) = blk_s
        m, l, acc = st
        m_new = jnp.maximum(m, s_max)
        alpha = jnp.exp2(m - m_new)
        p = jnp.exp2(s - m_new)
        l = alpha * l + jnp.sum(p, axis=0, keepdims=True)
        acc = alpha * acc + _dot(_tile_lanes(vst_ref, j, tk), p.astype(BF16))
        return m_new, l, acc

    init = (jnp.full((1, Q), NEG, F32), jnp.zeros((1, Q), F32), jnp.zeros((HEAD_DIM, Q), F32))

    _, l_s, a_s = _pipelined(J, sel_scores(J, True), lambda t: sel_scores(t, False), consume, init,
                             tile_scr)
    o_sel = a_s / l_s

    gates = jax.nn.sigmoid(gc_ref[0]).T
    outs = []
    for h in range(N_HEADS):
        cols = slice(h * tq, (h + 1) * tq)
        outs.append(gates[3 * h:3 * h + 1, :] * o_cmp[:, cols]
                    + gates[3 * h + 1:3 * h + 2, :] * o_sel[:, cols]
                    + gates[3 * h + 2:3 * h + 3, :] * o_win[:, cols])
    out = jnp.concatenate(outs, axis=0).T
    o_ref[0] = (out * _silu(z_ref[0].astype(F32))).astype(BF16)


def _nsa(q, kc2, vct, ks2, vst, kw2, vwt, gc, z, tq=256, tk=256):
    B, S, _ = q.shape
    n_c = kc2.shape[1]
    n_sel = S // SEL_LEN
    nsp = -(-n_sel // SUBLANES) * SUBLANES
    n_cmp = (S - CMP_LEN) // CMP_STRIDE + 1
    c_start = np.arange(n_c) * CMP_STRIDE
    s_start = np.arange(nsp) * SEL_LEN
    overlap_t = np.clip(np.minimum(c_start[None, :] + CMP_LEN, s_start[:, None] + SEL_LEN)
                        - np.maximum(c_start[None, :], s_start[:, None]), 0, None) / CMP_LEN
    overlap_t[:, n_cmp:] = 0.0
    overlap_t[n_sel:, :] = 0.0
    expand_t = (np.arange(S)[:, None] // SEL_LEN == np.arange(nsp)[None, :]).astype(np.float32)
    row = lambda w: pl.BlockSpec((1, tq, w), lambda b, i: (b, i, 0))
    full = lambda a: pl.BlockSpec((1,) + a.shape[1:], lambda b, i: (b, 0, 0))
    const = lambda a: pl.BlockSpec(a.shape, lambda b, i: (0, 0))
    ovt = jnp.asarray(overlap_t, BF16)
    ext = jnp.asarray(expand_t, BF16)
    return pl.pallas_call(
        functools.partial(_nsa_kernel, tq=tq, tk=tk, n_sel=n_sel),
        grid=(B, S // tq),
        in_specs=[row(MIX_W), full(kc2), full(vct), full(ks2), full(vst), full(kw2), full(vwt),
                  row(LANES), row(MIX_W), const(ovt), const(ext)],
        out_specs=row(MIX_W),
        out_shape=jax.ShapeDtypeStruct((B, S, MIX_W), BF16),
        scratch_shapes=[pltpu.VMEM((2, 1, tk, N_HEADS * tq), F32)],
        compiler_params=pltpu.CompilerParams(
            dimension_semantics=("parallel", "parallel"), vmem_limit_bytes=VMEM_LIMIT),
        name="native_sparse",
    )(q, kc2, vct, ks2, vst, kw2, vwt, gc, z, ovt, ext)


def _moba_kernel(q_ref, k_ref, vt_ref, z_ref, o_ref, km_scr, bias_scr, tile_scr, *, n_blk, tq):
    i = pl.program_id(1)
    tk = MOBA_BLOCK
    n_diag = tq // tk
    nb_pad = km_scr.shape[0]
    is_a = _head_masks((tq, LANES))

    @pl.when(i == 0)
    def _():
        km_scr[...] = jnp.zeros_like(km_scr)
        for b in range(n_blk):
            kb = k_ref[0, b * tk:(b + 1) * tk, :].astype(F32)
            km_scr[b:b + 1, :] = jnp.mean(kb, axis=0, keepdims=True)

    q = q_ref[0]
    blk = lax.broadcasted_iota(jnp.int32, (nb_pad, 2 * tq), 0)
    qloc = jnp.bitwise_and(lax.broadcasted_iota(jnp.int32, (1, 2 * tq), 1), tq - 1)
    own = i * n_diag + lax.shift_right_logical(qloc, int(np.log2(tk)))
    n_top = min(MOBA_TOPK, max(n_blk - 1, 1))
    qcat = []
    for p in range(N_PAIRS):
        qcat.append(_stack_pair(q[:, _pair_lanes(p)], is_a))
        km_hi, km_lo = _split_bf16(km_scr[:, _pair_lanes(p)])
        gsc = _nt_dot(km_hi, qcat[p]) + _nt_dot(km_lo, qcat[p])
        gsc = jnp.where(blk < own, gsc, NEG)
        rank = jnp.zeros((nb_pad, 2 * tq), F32)
        for s in range(n_blk):
            c = gsc[s:s + 1, :]
            beats = jnp.logical_or(c > gsc, jnp.logical_and(c == gsc, blk > s))
            rank = rank + jnp.where(beats, 1.0, 0.0)
        attends = jnp.logical_or(jnp.logical_and(rank < float(n_top), blk < own), blk == own)
        bias_scr[p] = jnp.where(attends, 0.0, NEG)

    kloc = lax.broadcasted_iota(jnp.int32, (tk, 2 * tq), 0)

    def scores(j, diag):
        ks = _tile_rows(k_ref, j, tk)
        ss = [_nt_dot(ks[:, _pair_lanes(p)], qcat[p]) for p in range(N_PAIRS)]
        if diag:
            causal = kloc <= qloc - (j * tk - i * tq)
            ss = [jnp.where(causal, s, NEG) for s in ss]
        return (j, [jnp.max(s, axis=0, keepdims=True) for s in ss]), ss

    def consume(blk_s, st):
        (j, maxes), ss = blk_s
        vts = _tile_lanes(vt_ref, j, tk)
        pr = range(N_PAIRS)
        bias = [bias_scr[p, pl.ds(j, 1), :] for p in pr]
        m_new = [jnp.maximum(st[p][0], maxes[p] + bias[p]) for p in pr]
        ps = [jnp.exp2(ss[p] - (m_new[p] - bias[p])) for p in pr]
        pvs = [_pv_pair(vts[_pair_lanes(p)], ps[p].astype(BF16), tq) for p in pr]
        out = []
        for p in pr:
            m, l, acc = st[p]
            alpha = jnp.exp2(m - m_new[p])
            out.append((m_new[p], alpha * l + jnp.sum(ps[p], axis=0, keepdims=True),
                        alpha * acc + pvs[p]))
        return out

    st = [(jnp.full((1, 2 * tq), NEG, F32), jnp.zeros((1, 2 * tq), F32),
           jnp.zeros((HEAD_DIM, 2 * tq), F32))] * N_PAIRS
    for d in range(n_diag - 1, 0, -1):
        st = consume(scores(i * n_diag + d, True), st)
    st = _pipelined(i * n_diag, scores(i * n_diag, True), lambda t: scores(t, False), consume, st,
                    tile_scr)
    _finish_pairs([acc / l for _, l, acc in st], tq, z_ref, o_ref)


def _moba(q, k, vt, z, tq=512):
    B, S, _ = q.shape
    tq = min(tq, S)
    n_blk = S // MOBA_BLOCK
    nb_pad = -(-n_blk // SUBLANES) * SUBLANES
    qspec, kspec, vtspec = _all_pair_specs(S, tq)
    return pl.pallas_call(
        functools.partial(_moba_kernel, n_blk=n_blk, tq=tq),
        grid=(B, S // tq),
        in_specs=[qspec, kspec, vtspec, qspec],
        out_specs=qspec,
        out_shape=jax.ShapeDtypeStruct((B, S, MIX_W), BF16),
        scratch_shapes=[pltpu.VMEM((nb_pad, MIX_W), F32),
                        pltpu.VMEM((N_PAIRS, nb_pad, 2 * tq), F32),
                        pltpu.VMEM((2, N_PAIRS, MOBA_BLOCK, 2 * tq), F32)],
        compiler_params=pltpu.CompilerParams(
            dimension_semantics=("parallel", "arbitrary"), vmem_limit_bytes=VMEM_LIMIT),
        name="moba",
    )(q, k, vt, z)


def _out_kernel(ma_ref, mb_ref, mc_ref, md_ref, w_ref, x_ref, mod_ref, g_ref, o_ref):
    D = x_ref.shape[-1]
    acc = _dot(ma_ref[0], w_ref[0:MIX_W, :])
    acc = acc + _dot(mb_ref[0], w_ref[MIX_W:2 * MIX_W, :])
    acc = acc + _dot(mc_ref[0], w_ref[2 * MIX_W:3 * MIX_W, :])
    acc = acc + _dot(md_ref[0], w_ref[3 * MIX_W:, :])
    ms = jnp.mean(acc * acc, axis=-1, keepdims=True)
    y = acc * lax.rsqrt(ms + EPS) * g_ref[...]
    o_ref[0] = x_ref[0] + mod_ref[0, :, 2 * D:] * y


def _out_projection(mixed, w_out, x, mod, g_post, tm):
    B, S, D = x.shape
    row = lambda w: pl.BlockSpec((1, tm, w), lambda b, i: (b, i, 0))
    return pl.pallas_call(
        _out_kernel,
        grid=(B, S // tm),
        in_specs=[row(MIX_W)] * 4 + [pl.BlockSpec(w_out.shape, lambda b, i: (0, 0)), row(D),
                                     pl.BlockSpec((1, 1, 3 * D), lambda b, i: (b, 0, 0)),
                                     pl.BlockSpec((1, D), lambda b, i: (0, 0))],
        out_specs=row(D),
        out_shape=jax.ShapeDtypeStruct((B, S, D), F32),
        compiler_params=pltpu.CompilerParams(
            dimension_semantics=("parallel", "parallel"), vmem_limit_bytes=VMEM_LIMIT),
        name="out_projection",
    )(*mixed, w_out.astype(BF16), x, mod[:, None, :], g_post[None])


def _layer(x, c, norm_pre, norm_post, w_mod, b_mod, w_in, w_out, cmp_pos, cmp_w1, cmp_b1, cmp_w2,
           tables, tables_c, tm):
    mod = _modulation(c, w_mod, b_mod)
    w_cat, w_t = _pack_w_in(w_in)
    (qa, ka, za, qb, kb, vb, zb, qc, kcr, vcr, gc, ks2, kw2, zc,
     qd, kd, zd, vat, vdt, vst, vwt) = _projection(x, mod, norm_pre, w_cat, w_t, tables, tm)
    oa = _stick_breaking(qa, ka, vat, za)
    ob = _dilated(qb, kb, vb, zb)
    kc2, vct = _compress(kcr, vcr, cmp_pos, cmp_w1, cmp_b1, cmp_w2, tables_c)
    oc = _nsa(qc, kc2, vct, ks2, vst, kw2, vwt, gc, zc)
    od = _moba(qd, kd, vdt, zd)
    return _out_projection((oa, ob, oc, od), w_out, x, mod, norm_post, tm)


def kernel(x, c, norm_pre, norm_post, w_mod, b_mod, w_in, w_out, cmp_pos, cmp_w1, cmp_b1, cmp_w2):
    S = x.shape[1]
    tables = _rope_lane_tables(np.arange(S))
    tables_c = _rope_lane_tables(np.arange(S // CMP_STRIDE) * CMP_STRIDE + CMP_LEN - 1)
    tm = min(512, S)
    for l in range(norm_pre.shape[0]):
        x = _layer(x, c, norm_pre[l], norm_post[l], w_mod[l], b_mod[l], w_in[l], w_out[l],
                   cmp_pos[l], cmp_w1[l], cmp_b1[l], cmp_w2[l], tables, tables_c, tm)
    return x
```

```python
import functools

import numpy as np
import jax
import jax.numpy as jnp
from jax import lax
from jax.experimental import pallas as pl
from jax.experimental.pallas import tpu as pltpu

F32 = jnp.float32
BF16 = jnp.bfloat16

N_HEADS = 4
HEAD_DIM = 64
MIX_W = N_HEADS * HEAD_DIM
ROPE_THETA = 500000.0
ROPE_DIM = HEAD_DIM // 4
ROPE_HALF = ROPE_DIM // 2
EPS = 1e-6
NEG = -1e30
BIG = 1e9
SCALE = HEAD_DIM ** -0.5
LOG2E = 1.4426950408889634
SCALE_LOG2 = SCALE * LOG2E
DIL_CONFIGS = ((128, 1), (512, 4), (2048, 16))
CMP_LEN = 32
CMP_STRIDE = 16
CMP_HID = 256
SEL_LEN = 64
SEL_TOPN = 16
WIN_LEN = 512
MOBA_BLOCK = 256
MOBA_TOPK = 3

N_PAIRS = N_HEADS // 2
LANES = 128
SUBLANES = 8
TILE_W = 2 * LANES
KEY_TILE = 256
DIL_GROUP = 2048
VMEM_LIMIT = 56 * 1024 * 1024


def _nt_dot(a, b):
    return lax.dot_general(a, b, (((1,), (1,)), ((), ())), preferred_element_type=F32)


def _dot(a, b):
    return jnp.dot(a, b, preferred_element_type=F32)


def _split_bf16(x):
    hi = x.astype(BF16)
    lo = (x - hi.astype(F32)).astype(BF16)
    return hi, lo


def _silu(x):
    return x * jax.nn.sigmoid(x)


def _head_masks(shape):
    lane = lax.broadcasted_iota(jnp.int32, shape, 1)
    return lane < HEAD_DIM


def _pv_pair(vt, p, tq):
    return jnp.concatenate([_dot(vt[:HEAD_DIM], p[:, :tq]), _dot(vt[HEAD_DIM:], p[:, tq:])], axis=1)


def _mod_kernel(c_ref, w_ref, b_ref, o_ref):
    c = c_ref[...]
    o_ref[...] = jnp.dot(_silu(c), w_ref[...], preferred_element_type=F32,
                         precision=lax.Precision.HIGHEST) + b_ref[...]


def _modulation(c, w_mod, b_mod):
    B, D = c.shape
    N = w_mod.shape[1]
    rows = SUBLANES
    cp = jnp.zeros((rows, D), F32).at[:B].set(c)
    tn = N // 4
    out = pl.pallas_call(
        _mod_kernel,
        grid=(N // tn,),
        in_specs=[pl.BlockSpec((rows, D), lambda j: (0, 0)),
                  pl.BlockSpec((D, tn), lambda j: (0, j)),
                  pl.BlockSpec((1, tn), lambda j: (0, j))],
        out_specs=pl.BlockSpec((rows, tn), lambda j: (0, j)),
        out_shape=jax.ShapeDtypeStruct((rows, N), F32),
        name="modulation",
    )(cp, w_mod, b_mod[None])
    return out[:B]


def _rope_lane_tile(y, cos, sa, sb):
    return y * cos + pltpu.roll(y, LANES - ROPE_HALF, 1) * sa + pltpu.roll(y, ROPE_HALF, 1) * sb


def _pack_weights(win_ref, w_ref, wt_ref):
    D = win_ref.shape[0]
    o = IN_OFFS
    chunk = 256
    for r0 in range(0, D, chunk):
        rs = slice(r0, r0 + chunk)

        def cp(dst, a, b):
            w_ref[rs, dst:dst + (b - a)] = win_ref[rs, a:b].astype(BF16)

        cp(0, o[0], o[11])
        small = PROJ_TILES["c_small"] * TILE_W
        cp(small + LANES, o[15], o[16])
        pad = small + LANES + 3 * N_HEADS
        w_ref[rs, pad:small + TILE_W] = jnp.zeros((chunk, small + TILE_W - pad), BF16)
        keys = PROJ_TILES["c_keys"] * TILE_W
        for rep in range(2):
            cp(keys + rep * HEAD_DIM, o[11], o[12])
            cp(keys + LANES + rep * HEAD_DIM, o[13], o[14])
        cp(PROJ_TILES["zc"] * TILE_W, o[16], o[21])

    def put_t(dst, a, lo=0, n=LANES):
        wt_ref[dst:dst + n, :] = win_ref[:, a:a + LANES].astype(F32).T[lo:lo + n].astype(BF16)

    for h in range(2):
        put_t(h * LANES, o[2] + h * LANES)
        put_t(MIX_W + h * LANES, o[19] + h * LANES)
    put_t(2 * MIX_W, o[11], HEAD_DIM, HEAD_DIM)
    put_t(2 * MIX_W + HEAD_DIM, o[13], HEAD_DIM, HEAD_DIM)


def _proj_kernel(x_ref, mod_ref, g_ref, win_ref, cos_ref, sa_ref, sb_ref,
                 qa_ref, ka_ref, za_ref,
                 qb_ref, kb_ref, vb_ref, zb_ref,
                 qc_ref, kcr_ref, vcr_ref, gc_ref, ks_ref, kw_ref, zc_ref,
                 qd_ref, kd_ref, zd_ref, vat_ref, vdt_ref, vst_ref, vwt_ref, w_ref, wt_ref):
    @pl.when(jnp.logical_and(pl.program_id(0) == 0, pl.program_id(1) == 0))
    def _():
        _pack_weights(win_ref, w_ref, wt_ref)

    D = x_ref.shape[-1]
    x = x_ref[0]
    ms = jnp.mean(x * x, axis=-1, keepdims=True)
    y = x * lax.rsqrt(ms + EPS) * g_ref[...]
    shift = mod_ref[0, :, 0:D]
    scale = mod_ref[0, :, D:2 * D]
    h = (y * (1.0 + scale) + shift).astype(BF16)
    cos, sa, sb = cos_ref[...], sa_ref[...], sb_ref[...]

    def tile(t):
        return _dot(h, w_ref[:, t * TILE_W:(t + 1) * TILE_W])

    def rope_lo(y):
        return _rope_lane_tile(y[:, :LANES], cos, sa, sb)

    def rope_all(y):
        return jnp.concatenate([rope_lo(y), _rope_lane_tile(y[:, LANES:], cos, sa, sb)], axis=1)

    T = PROJ_TILES
    qa_ref[0] = tile(T["qa"]).astype(BF16)
    ka_ref[0] = tile(T["ka"]).astype(BF16)
    za_ref[0] = tile(T["za"]).astype(BF16)
    qb_ref[0] = rope_all(tile(T["qb"]))
    kb_ref[0] = rope_all(tile(T["kb"]))
    vb_ref[0] = tile(T["vb"])
    zb_ref[0] = tile(T["zb"]).astype(BF16)
    qc_ref[0] = (rope_all(tile(T["qc"])) * SCALE_LOG2).astype(BF16)
    small = tile(T["c_small"])
    kcr_ref[0] = small[:, 0:HEAD_DIM]
    vcr_ref[0] = small[:, HEAD_DIM:LANES]
    gc_ref[0] = small[:, LANES:]
    keys = rope_all(tile(T["c_keys"])).astype(BF16)
    ks_ref[0] = keys[:, :LANES]
    kw_ref[0] = keys[:, LANES:]
    zc_ref[0] = tile(T["zc"]).astype(BF16)
    qd_ref[0] = (rope_all(tile(T["qd"])) * SCALE_LOG2).astype(BF16)
    kd_ref[0] = rope_all(tile(T["kd"])).astype(BF16)
    zd_ref[0] = tile(T["zd"]).astype(BF16)
    vt = _nt_dot(wt_ref[...], h).astype(BF16)
    vat_ref[0] = vt[0:MIX_W]
    vdt_ref[0] = vt[MIX_W:2 * MIX_W]
    vst_ref[0] = vt[2 * MIX_W:2 * MIX_W + HEAD_DIM]
    vwt_ref[0] = vt[2 * MIX_W + HEAD_DIM:]


PROJ_TILES = dict(qa=0, ka=1, za=3, qb=4, kb=5, vb=6, zb=7, qc=8, c_small=9, c_keys=10,
                  zc=11, qd=12, kd=13, zd=15)
N_PROJ_TILES = 16
N_VT_ROWS = 2 * MIX_W + 2 * HEAD_DIM
IN_OFFS = [int(v) for v in np.cumsum([0] + [MIX_W] * 9 + [HEAD_DIM] * 6 + [3 * N_HEADS] + [MIX_W] * 5)]


def _rope_angles(pos):
    inv_freq = 1.0 / (ROPE_THETA ** (np.arange(0, ROPE_DIM, 2, dtype=np.float32) / ROPE_DIM))
    ang = jnp.asarray(pos, F32)[:, None] * jnp.asarray(inv_freq, F32)[None, :]
    return jnp.cos(ang), jnp.sin(ang)


def _rope_lane_tables(pos):
    cos, sin = _rope_angles(pos)
    n = cos.shape[0]
    rest = HEAD_DIM - ROPE_DIM
    cos_h = jnp.concatenate([cos, cos, jnp.ones((n, rest), F32)], axis=1)
    sa_h = jnp.concatenate([-sin, jnp.zeros((n, HEAD_DIM - ROPE_HALF), F32)], axis=1)
    sb_h = jnp.concatenate([jnp.zeros((n, ROPE_HALF), F32), sin, jnp.zeros((n, rest), F32)], axis=1)
    two = lambda t: jnp.concatenate([t, t], axis=1)
    return two(cos_h), two(sa_h), two(sb_h)


def _projection(x, mod, g_pre, w_in, tables, tm):
    B, S, D = x.shape
    cos, sa, sb = tables
    w_in = jnp.pad(w_in.astype(BF16), ((0, 0), (0, -w_in.shape[1] % LANES)))
    row = lambda w: pl.BlockSpec((1, tm, w), lambda b, i: (b, i, 0))
    tab = pl.BlockSpec((tm, LANES), lambda b, i: (i, 0))
    widths = [(MIX_W, BF16)] * 3 + [(MIX_W, F32)] * 3 + [(MIX_W, BF16)] + \
             [(MIX_W, BF16), (HEAD_DIM, F32), (HEAD_DIM, F32), (LANES, F32),
              (LANES, BF16), (LANES, BF16), (MIX_W, BF16)] + \
             [(MIX_W, BF16)] * 3
    t_rows = (MIX_W, MIX_W, HEAD_DIM, HEAD_DIM)
    t_specs = [pl.BlockSpec((1, r, tm), lambda b, i: (b, 0, i)) for r in t_rows]
    t_shapes = [jax.ShapeDtypeStruct((B, r, S), BF16) for r in t_rows]
    return pl.pallas_call(
        _proj_kernel,
        grid=(B, S // tm),
        in_specs=[row(D),
                  pl.BlockSpec((1, 1, 3 * D), lambda b, i: (b, 0, 0)),
                  pl.BlockSpec((1, D), lambda b, i: (0, 0)),
                  pl.BlockSpec(w_in.shape, lambda b, i: (0, 0), pipeline_mode=pl.Buffered(1)),
                  tab, tab, tab],
        out_specs=[row(w) for w, _ in widths] + t_specs,
        out_shape=[jax.ShapeDtypeStruct((B, S, w), dt) for w, dt in widths] + t_shapes,
        scratch_shapes=[pltpu.VMEM((D, N_PROJ_TILES * TILE_W), BF16),
                        pltpu.VMEM((N_VT_ROWS, D), BF16)],
        compiler_params=pltpu.CompilerParams(
            dimension_semantics=("arbitrary", "arbitrary"), vmem_limit_bytes=VMEM_LIMIT),
        name="in_projection",
    )(x, mod[:, None, :], g_pre[None], w_in, cos, sa, sb)


def _all_pair_specs(S, tq):
    qspec = pl.BlockSpec((1, tq, MIX_W), lambda b, i: (b, i, 0))
    kspec = pl.BlockSpec((1, S, MIX_W), lambda b, i: (b, 0, 0))
    vtspec = pl.BlockSpec((1, MIX_W, S), lambda b, i: (b, 0, 0))
    return qspec, kspec, vtspec


def _pair_lanes(p):
    return slice(p * LANES, (p + 1) * LANES)


def _tile_rows(ref, j, tk):
    return ref[0, pl.ds(pl.multiple_of(j * tk, tk), tk), :]


def _tile_rows_2d(ref, j, tk):
    return ref[pl.ds(pl.multiple_of(j * tk, tk), tk), :]


def _tile_lanes(ref, j, tk):
    return ref[0, :, pl.ds(pl.multiple_of(j * tk, tk), tk)]


def _stack_pair(q, is_a):
    return jnp.concatenate([jnp.where(is_a, q, 0), jnp.where(is_a, 0, q)], axis=0)


def _pipelined(n, first, produce, consume, state, scr, quads=False):
    def put(slot, tiles):
        for a, tile in enumerate(tiles):
            scr[slot, a] = tile

    def get(slot):
        return [scr[slot, a] for a in range(scr.shape[1])]

    def step(t, slot, carry):
        small, st = carry
        small_new, tiles = produce(t)
        put(1 - slot, tiles)
        return small_new, consume((small, get(slot)), st)

    def pair(t, c):
        return step(t + 1, 1, step(t, 0, c))

    put(0, first[1])
    carry = (first[0], state)
    done = 0
    if quads:
        carry = lax.fori_loop(0, lax.shift_right_logical(n, 2),
                              lambda u, c: pair(4 * u + 2, pair(4 * u, c)), carry)
        done = lax.shift_left(lax.shift_right_logical(n, 2), 2)
    carry = lax.fori_loop(0, lax.shift_right_logical(n - done, 1),
                          lambda u, c: pair(done + 2 * u, c), carry)

    def odd_tail(c):
        small, st = step(n - 1, 0, c)
        return consume((small, get(1)), st)

    return lax.cond(jnp.bitwise_and(n, 1) == 1, odd_tail,
                    lambda c: consume((c[0], get(0)), c[1]), carry)


def _finish_pairs(accs, tq, z_ref, o_ref):
    heads = [a[:, h * tq:(h + 1) * tq] for a in accs for h in range(2)]
    out = jnp.concatenate(heads, axis=0).T
    o_ref[0] = (out * _silu(z_ref[0].astype(F32))).astype(BF16)


def _stick_kernel(q_ref, k_ref, vt_ref, z_ref, o_ref, tile_scr, *, tq):
    i = pl.program_id(1)
    tk = KEY_TILE
    n_diag = tq // tk
    is_a = _head_masks((tq, LANES))
    qs = (q_ref[0].astype(F32) * SCALE).astype(BF16)
    qcat = [_stack_pair(qs[:, _pair_lanes(p)], is_a) for p in range(N_PAIRS)]
    r_i = lax.broadcasted_iota(jnp.int32, (tk, tk), 0)
    c_i = lax.broadcasted_iota(jnp.int32, (tk, tk), 1)
    suffix = jnp.where(c_i >= r_i, 1.0, 0.0).astype(BF16)
    kloc = lax.broadcasted_iota(jnp.int32, (tk, 2 * tq), 0)
    qloc = jnp.bitwise_and(lax.broadcasted_iota(jnp.int32, (1, 2 * tq), 1), tq - 1)

    def log_weights(j, diag):
        ks = _tile_rows(k_ref, j, tk)
        zs = [_nt_dot(ks[:, _pair_lanes(p)], qcat[p]) for p in range(N_PAIRS)]
        sps = [jnp.maximum(z, 0.0) + jnp.log(1.0 + jnp.exp2(jnp.abs(z) * (-LOG2E))) for z in zs]
        if diag:
            past = kloc < qloc - (j * tk - i * tq)
            sps = [jnp.where(past, sp, 0.0) for sp in sps]
        css = [_dot(suffix, sp.astype(BF16)) for sp in sps]
        xs = [z - cs for z, cs in zip(zs, css)]
        if diag:
            xs = [jnp.where(past, x, NEG) for x in xs]
        return (j, [cs[0:1, :] for cs in css]), xs

    def accumulate(blk, st):
        (j, tots), xs = blk
        vts = _tile_lanes(vt_ref, j, tk)
        ws = [jnp.exp(xs[p] - st[p][1]).astype(BF16) for p in range(N_PAIRS)]
        return [(st[p][0] + _pv_pair(vts[_pair_lanes(p)], ws[p], tq), st[p][1] + tots[p])
                for p in range(N_PAIRS)]

    st = [(jnp.zeros((HEAD_DIM, 2 * tq), F32), jnp.zeros((1, 2 * tq), F32))] * N_PAIRS
    top = i * n_diag + n_diag - 1
    for d in range(n_diag - 1):
        st = accumulate(log_weights(top - d, True), st)
    st = _pipelined(i * n_diag, log_weights(i * n_diag, True),
                    lambda t: log_weights(i * n_diag - 1 - t, False), accumulate, st, tile_scr)
    _finish_pairs([acc for acc, _ in st], tq, z_ref, o_ref)


def _stick_breaking(q, k, vt, z, tq=512):
    B, S, _ = q.shape
    tq = min(tq, S)
    qspec, kspec, vtspec = _all_pair_specs(S, tq)
    return pl.pallas_call(
        functools.partial(_stick_kernel, tq=tq),
        grid=(B, S // tq),
        in_specs=[qspec, kspec, vtspec, qspec],
        out_specs=qspec,
        out_shape=jax.ShapeDtypeStruct((B, S, MIX_W), BF16),
        scratch_shapes=[pltpu.VMEM((2, N_PAIRS, KEY_TILE, 2 * tq), F32)],
        compiler_params=pltpu.CompilerParams(
            dimension_semantics=("parallel", "arbitrary"), vmem_limit_bytes=VMEM_LIMIT),
        name="stick_breaking",
    )(q, k, vt, z)


def _dil_kernel(q_ref, k_ref, v_ref, z_ref, o_ref, m_scr, l_scr, a_scr, *, G):
    g = pl.program_id(2)
    blk = LANES
    is_a = _head_masks((blk, LANES))
    qrow = jnp.bitwise_and(lax.broadcasted_iota(jnp.int32, (2 * blk, 2 * blk), 0), blk - 1)
    col = lax.broadcasted_iota(jnp.int32, (2 * blk, 2 * blk), 1)
    band = jnp.logical_and(col >= qrow, col <= qrow + blk)
    n_sub = G // blk
    unroll = 8

    for ci, (_, d) in enumerate(DIL_CONFIGS):
        nb = G // (blk * d)
        sh = int(np.log2(nb))

        def sub(n, ci=ci, d=d, nb=nb, sh=sh):
            r = lax.shift_right_logical(n, sh)
            ub = jnp.bitwise_and(n, nb - 1)
            loc = ub * (blk * d) + r
            glob = g * G + loc
            has_prev = glob >= blk * d
            pstart = jnp.maximum(glob - blk * d, r)
            qs = (q_ref[0, pl.ds(loc, blk, stride=d), :] * SCALE_LOG2).astype(BF16)
            kk = jnp.concatenate([k_ref[0, pl.ds(pstart, blk, stride=d), :],
                                  k_ref[0, pl.ds(glob, blk, stride=d), :]], axis=0).astype(BF16)
            vv = jnp.concatenate([v_ref[0, pl.ds(pstart, blk, stride=d), :],
                                  v_ref[0, pl.ds(glob, blk, stride=d), :]], axis=0).astype(BF16)
            ok = jnp.logical_and(band, jnp.logical_or(col >= blk, has_prev))
            s = jnp.where(ok, _nt_dot(_stack_pair(qs, is_a), kk), NEG)
            m = jnp.max(s, axis=-1, keepdims=True)
            p = jnp.exp2(s - m)
            l = jnp.sum(p, axis=-1, keepdims=True)
            acc = _dot(p.astype(BF16), vv)
            rows = pl.ds(loc, blk, stride=d)
            m_scr[ci, rows, :] = jnp.where(is_a, m[:blk], m[blk:])
            l_scr[ci, rows, :] = jnp.where(is_a, l[:blk], l[blk:])
            a_scr[ci, rows, :] = jnp.where(is_a, acc[:blk], acc[blk:])

        def trip(t, _, sub=sub):
            for u in range(unroll):
                sub(t * unroll + u)
            return 0

        lax.fori_loop(0, n_sub // unroll, trip, 0)

    chunk = 256

    def combine(c, _):
        rows = pl.ds(pl.multiple_of(c * chunk, chunk), chunk)
        m0, m1, m2 = m_scr[0, rows, :], m_scr[1, rows, :], m_scr[2, rows, :]
        mm = jnp.maximum(jnp.maximum(m0, m1), m2)
        w0, w1, w2 = jnp.exp2(m0 - mm), jnp.exp2(m1 - mm), jnp.exp2(m2 - mm)
        num = w0 * a_scr[0, rows, :] + w1 * a_scr[1, rows, :] + w2 * a_scr[2, rows, :]
        den = w0 * l_scr[0, rows, :] + w1 * l_scr[1, rows, :] + w2 * l_scr[2, rows, :]
        o_ref[0, rows, :] = (num / den * _silu(z_ref[0, rows, :].astype(F32))).astype(BF16)
        return 0

    lax.fori_loop(0, G // chunk, combine, 0)


def _dilated(q, k, v, z):
    B, S, _ = q.shape
    G = min(DIL_GROUP, S)
    gspec = pl.BlockSpec((1, G, LANES), lambda b, p, g: (b, g, p))
    kspec = pl.BlockSpec((1, S, LANES), lambda b, p, g: (b, 0, p))
    scr = pltpu.VMEM((len(DIL_CONFIGS), G, LANES), F32)
    return pl.pallas_call(
        functools.partial(_dil_kernel, G=G),
        grid=(B, 2, S // G),
        in_specs=[gspec, kspec, kspec, gspec],
        out_specs=gspec,
        out_shape=jax.ShapeDtypeStruct((B, S, MIX_W), BF16),
        scratch_shapes=[scr, scr, scr],
        compiler_params=pltpu.CompilerParams(
            dimension_semantics=("parallel", "parallel", "arbitrary"), vmem_limit_bytes=VMEM_LIMIT),
        name="dilated_window",
    )(q, k, v, z)


def _compress_kernel(k_ref, v_ref, p_ref, w1_ref, b1_ref, w2k_ref, w2vt_ref, cos_ref, sa_ref, sb_ref,
                     kc_ref, vc_ref):
    n = kc_ref.shape[1]
    dh = k_ref.shape[-1]
    for s, (t_ref, o_ref) in enumerate(((k_ref, kc_ref), (v_ref, vc_ref))):
        lo = jnp.zeros((n, CMP_HID), F32)
        hi = jnp.zeros((n, CMP_HID), F32)
        for l in range(CMP_STRIDE):
            t = t_ref[0, pl.ds(l, n, stride=CMP_STRIDE), :]
            lo = lo + _dot((t + p_ref[s, l:l + 1, :]).astype(BF16),
                           w1_ref[s, l * dh:(l + 1) * dh, :])
            lh = l + CMP_STRIDE
            hi = hi + _dot((t + p_ref[s, lh:lh + 1, :]).astype(BF16),
                           w1_ref[s, lh * dh:(lh + 1) * dh, :])
        pre = lo + pltpu.roll(hi, n - 1, 0) + b1_ref[s]
        hid = 0.5 * pre * (1.0 + jnp.tanh(np.sqrt(2.0 / np.pi).astype(np.float32)
                                          * (pre + 0.044715 * (pre * pre * pre))))
        hid = hid.astype(BF16)
        if s == 0:
            out = _dot(hid, w2k_ref[...])
            o_ref[0] = _rope_lane_tile(out, cos_ref[...], sa_ref[...], sb_ref[...]).astype(BF16)
        else:
            o_ref[0] = _nt_dot(w2vt_ref[...], hid).astype(BF16)


def _compress(kcr, vcr, cmp_pos, cmp_w1, cmp_b1, cmp_w2, tables_c):
    B, S, dh = kcr.shape
    n = S // CMP_STRIDE
    w1 = cmp_w1.astype(BF16)
    w2k = jnp.concatenate([cmp_w2[0], cmp_w2[0]], axis=-1).astype(BF16)
    w2vt = cmp_w2[1].T.astype(BF16)
    b1 = cmp_b1[:, None, :]
    cos, sa, sb = tables_c
    const = lambda a: pl.BlockSpec(a.shape, lambda b: (0,) * a.ndim)
    tspec = pl.BlockSpec((1, S, dh), lambda b: (b, 0, 0))
    return pl.pallas_call(
        _compress_kernel,
        grid=(B,),
        in_specs=[tspec, tspec, const(cmp_pos), const(w1), const(b1), const(w2k), const(w2vt),
                  const(cos), const(sa), const(sb)],
        out_specs=[pl.BlockSpec((1, n, LANES), lambda b: (b, 0, 0)),
                   pl.BlockSpec((1, dh, n), lambda b: (b, 0, 0))],
        out_shape=[jax.ShapeDtypeStruct((B, n, LANES), BF16),
                   jax.ShapeDtypeStruct((B, dh, n), BF16)],
        compiler_params=pltpu.CompilerParams(
            dimension_semantics=("parallel",), vmem_limit_bytes=VMEM_LIMIT),
        name="compress_tokens",
    )(kcr, vcr, cmp_pos, w1, b1, w2k, w2vt, cos, sa, sb)


def _nsa_kernel(q_ref, kc_ref, vct_ref, ks_ref, vst_ref, kw_ref, vwt_ref, gc_ref, z_ref,
                ovt_ref, ext_ref, o_ref, tile_scr, *, tq, tk, n_sel):
    i = pl.program_id(1)
    Q = N_HEADS * tq
    is_a = _head_masks((tq, LANES))
    qs = q_ref[0]
    q4 = jnp.concatenate([_stack_pair(qs[:, :LANES], is_a), _stack_pair(qs[:, LANES:], is_a)], axis=0)
    heads = lambda x: jnp.concatenate([x] * N_HEADS, axis=1)
    qpos1 = i * tq + lax.broadcasted_iota(jnp.int32, (1, tq), 1)
    qpos = heads(qpos1)

    kpos0 = lax.broadcasted_iota(jnp.int32, (tk, Q), 0)
    J = (i * tq) // tk

    n_win = -(-WIN_LEN // tk) + 1
    s_w, vt_w = [], []
    for a in range(n_win):
        j_a = J - (n_win - 1) + a
        qrel = qpos - j_a * tk
        ok = jnp.broadcast_to(j_a >= 0, (tk, Q))
        if a == 0:
            ok = jnp.logical_and(ok, kpos0 > qrel - WIN_LEN)
        if a == n_win - 1:
            ok = kpos0 <= qrel
        j_c = jnp.maximum(j_a, 0)
        s_w.append(jnp.where(ok, _nt_dot(_tile_rows(kw_ref, j_c, tk), q4), NEG))
        vt_w.append(_tile_lanes(vwt_ref, j_c, tk))
    m_w = functools.reduce(jnp.maximum, [jnp.max(s, axis=0, keepdims=True) for s in s_w])
    p_w = [jnp.exp2(s - m_w) for s in s_w]
    l_w = sum(jnp.sum(p, axis=0, keepdims=True) for p in p_w)
    o_win = sum(_dot(vt, p.astype(BF16)) for vt, p in zip(vt_w, p_w)) / l_w

    n_c = kc_ref.shape[1]
    c_end = lax.broadcasted_iota(jnp.int32, (n_c, Q), 0) * CMP_STRIDE + (CMP_LEN - 1)
    c_valid = c_end <= qpos
    sc = jnp.where(c_valid, _nt_dot(kc_ref[0], q4), NEG)
    pc = jnp.exp2(sc - jnp.max(sc, axis=0, keepdims=True)) * jnp.where(c_valid, 1.0, 0.0)
    lc = jnp.sum(pc, axis=0, keepdims=True)
    pc = pc * jnp.where(lc > 0.0, 1.0 / lc, 0.0)
    o_cmp = _dot(vct_ref[0], pc.astype(BF16))

    pcs = pc[:, 0:tq] + pc[:, tq:2 * tq] + pc[:, 2 * tq:3 * tq] + pc[:, 3 * tq:]
    hi, lo = _split_bf16(pcs)
    imp = _dot(ovt_ref[...], hi) + _dot(ovt_ref[...], lo)
    nsp = imp.shape[0]
    blk = lax.broadcasted_iota(jnp.int32, (nsp, tq), 0)
    own = lax.shift_right_logical(qpos1, int(np.log2(SEL_LEN)))
    s_valid = blk <= own
    forced = jnp.logical_or(blk == 0, blk >= own - 1)
    imp = jnp.where(s_valid, jnp.where(forced, BIG, imp), NEG)
    rank = jnp.zeros((nsp, tq), F32)
    for s in range(n_sel):
        c = imp[s:s + 1, :]
        beats = jnp.logical_or(c > imp, jnp.logical_and(c == imp, blk > s))
        rank = rank + jnp.where(beats, 1.0, 0.0)
    chosen = jnp.logical_and(rank < float(min(SEL_TOPN, n_sel)), s_valid)
    sel = jnp.where(chosen, 1.0, 0.0).astype(BF16)

    def sel_scores(j, diag):
        bias = (_dot(_tile_rows_2d(ext_ref, j, tk), sel) - 1.0) * (-NEG)
        s = _nt_dot(_tile_rows(ks_ref, j, tk), q4) + heads(bias)
        if diag:
            s = jnp.where(j * tk + kpos0 <= qpos, s, NEG)
        return (j, jnp.max(s, axis=0, keepdims=True)), [s]

    def consume(blk_s, st):
        (j, s_max), (s,) = blk_s
        m, l, acc = st
        m_new = jnp.maximum(m, s_max)
        alpha = jnp.exp2(m - m_new)
        p = jnp.exp2(s - m_new)
        l = alpha * l + jnp.sum(p, axis=0, keepdims=True)
        acc = alpha * acc + _dot(_tile_lanes(vst_ref, j, tk), p.astype(BF16))
        return m_new, l, acc

    init = (jnp.full((1, Q), NEG, F32), jnp.zeros((1, Q), F32), jnp.zeros((HEAD_DIM, Q), F32))

    _, l_s, a_s = _pipelined(J, sel_scores(J, True), lambda t: sel_scores(t, False), consume, init,
                             tile_scr, quads=True)
    o_sel = a_s / l_s

    gates = jax.nn.sigmoid(gc_ref[0]).T
    outs = []
    for h in range(N_HEADS):
        cols = slice(h * tq, (h + 1) * tq)
        outs.append(gates[3 * h:3 * h + 1, :] * o_cmp[:, cols]
                    + gates[3 * h + 1:3 * h + 2, :] * o_sel[:, cols]
                    + gates[3 * h + 2:3 * h + 3, :] * o_win[:, cols])
    out = jnp.concatenate(outs, axis=0).T
    o_ref[0] = (out * _silu(z_ref[0].astype(F32))).astype(BF16)


def _nsa(q, kc2, vct, ks2, vst, kw2, vwt, gc, z, tq=256, tk=256):
    B, S, _ = q.shape
    n_c = kc2.shape[1]
    n_sel = S // SEL_LEN
    nsp = -(-n_sel // SUBLANES) * SUBLANES
    n_cmp = (S - CMP_LEN) // CMP_STRIDE + 1
    c_start = np.arange(n_c) * CMP_STRIDE
    s_start = np.arange(nsp) * SEL_LEN
    overlap_t = np.clip(np.minimum(c_start[None, :] + CMP_LEN, s_start[:, None] + SEL_LEN)
                        - np.maximum(c_start[None, :], s_start[:, None]), 0, None) / CMP_LEN
    overlap_t[:, n_cmp:] = 0.0
    overlap_t[n_sel:, :] = 0.0
    expand_t = (np.arange(S)[:, None] // SEL_LEN == np.arange(nsp)[None, :]).astype(np.float32)
    row = lambda w: pl.BlockSpec((1, tq, w), lambda b, i: (b, i, 0))
    full = lambda a: pl.BlockSpec((1,) + a.shape[1:], lambda b, i: (b, 0, 0))
    const = lambda a: pl.BlockSpec(a.shape, lambda b, i: (0, 0))
    ovt = jnp.asarray(overlap_t, BF16)
    ext = jnp.asarray(expand_t, BF16)
    return pl.pallas_call(
        functools.partial(_nsa_kernel, tq=tq, tk=tk, n_sel=n_sel),
        grid=(B, S // tq),
        in_specs=[row(MIX_W), full(kc2), full(vct), full(ks2), full(vst), full(kw2), full(vwt),
                  row(LANES), row(MIX_W), const(ovt), const(ext)],
        out_specs=row(MIX_W),
        out_shape=jax.ShapeDtypeStruct((B, S, MIX_W), BF16),
        scratch_shapes=[pltpu.VMEM((2, 1, tk, N_HEADS * tq), F32)],
        compiler_params=pltpu.CompilerParams(
            dimension_semantics=("parallel", "parallel"), vmem_limit_bytes=VMEM_LIMIT),
        name="native_sparse",
    )(q, kc2, vct, ks2, vst, kw2, vwt, gc, z, ovt, ext)


def _moba_kernel(q_ref, k_ref, vt_ref, z_ref, o_ref, km_scr, bias_scr, tile_scr, *, n_blk, tq):
    i = pl.program_id(1)
    tk = MOBA_BLOCK
    n_diag = tq // tk
    nb_pad = km_scr.shape[0]
    is_a = _head_masks((tq, LANES))

    @pl.when(i == 0)
    def _():
        km_scr[...] = jnp.zeros_like(km_scr)
        for b in range(n_blk):
            kb = k_ref[0, b * tk:(b + 1) * tk, :].astype(F32)
            km_scr[b:b + 1, :] = jnp.mean(kb, axis=0, keepdims=True)

    q = q_ref[0]
    blk = lax.broadcasted_iota(jnp.int32, (nb_pad, 2 * tq), 0)
    qloc = jnp.bitwise_and(lax.broadcasted_iota(jnp.int32, (1, 2 * tq), 1), tq - 1)
    own = i * n_diag + lax.shift_right_logical(qloc, int(np.log2(tk)))
    n_top = min(MOBA_TOPK, max(n_blk - 1, 1))
    qcat = []
    for p in range(N_PAIRS):
        qcat.append(_stack_pair(q[:, _pair_lanes(p)], is_a))
        km_hi, km_lo = _split_bf16(km_scr[:, _pair_lanes(p)])
        gsc = _nt_dot(km_hi, qcat[p]) + _nt_dot(km_lo, qcat[p])
        gsc = jnp.where(blk < own, gsc, NEG)
        rank = jnp.zeros((nb_pad, 2 * tq), F32)
        for s in range(n_blk):
            c = gsc[s:s + 1, :]
            beats = jnp.logical_or(c > gsc, jnp.logical_and(c == gsc, blk > s))
            rank = rank + jnp.where(beats, 1.0, 0.0)
        attends = jnp.logical_or(jnp.logical_and(rank < float(n_top), blk < own), blk == own)
        bias_scr[p] = jnp.where(attends, 0.0, NEG)

    kloc = lax.broadcasted_iota(jnp.int32, (tk, 2 * tq), 0)

    def scores(j, diag):
        ks = _tile_rows(k_ref, j, tk)
        ss = [_nt_dot(ks[:, _pair_lanes(p)], qcat[p]) for p in range(N_PAIRS)]
        if diag:
            causal = kloc <= qloc - (j * tk - i * tq)
            ss = [jnp.where(causal, s, NEG) for s in ss]
        return (j, [jnp.max(s, axis=0, keepdims=True) for s in ss]), ss

    def consume(blk_s, st):
        (j, maxes), ss = blk_s
        vts = _tile_lanes(vt_ref, j, tk)
        pr = range(N_PAIRS)
        bias = [bias_scr[p, pl.ds(j, 1), :] for p in pr]
        m_new = [jnp.maximum(st[p][0], maxes[p] + bias[p]) for p in pr]
        ps = [jnp.exp2(ss[p] - (m_new[p] - bias[p])) for p in pr]
        pvs = [_pv_pair(vts[_pair_lanes(p)], ps[p].astype(BF16), tq) for p in pr]
        out = []
        for p in pr:
            m, l, acc = st[p]
            alpha = jnp.exp2(m - m_new[p])
            out.append((m_new[p], alpha * l + jnp.sum(ps[p], axis=0, keepdims=True),
                        alpha * acc + pvs[p]))
        return out

    st = [(jnp.full((1, 2 * tq), NEG, F32), jnp.zeros((1, 2 * tq), F32),
           jnp.zeros((HEAD_DIM, 2 * tq), F32))] * N_PAIRS
    for d in range(n_diag - 1, 0, -1):
        st = consume(scores(i * n_diag + d, True), st)
    st = _pipelined(i * n_diag, scores(i * n_diag, True), lambda t: scores(t, False), consume, st,
                    tile_scr, quads=True)
    _finish_pairs([acc / l for _, l, acc in st], tq, z_ref, o_ref)


def _moba(q, k, vt, z, tq=256):
    B, S, _ = q.shape
    tq = min(tq, S)
    n_blk = S // MOBA_BLOCK
    nb_pad = -(-n_blk // SUBLANES) * SUBLANES
    qspec, kspec, vtspec = _all_pair_specs(S, tq)
    return pl.pallas_call(
        functools.partial(_moba_kernel, n_blk=n_blk, tq=tq),
        grid=(B, S // tq),
        in_specs=[qspec, kspec, vtspec, qspec],
        out_specs=qspec,
        out_shape=jax.ShapeDtypeStruct((B, S, MIX_W), BF16),
        scratch_shapes=[pltpu.VMEM((nb_pad, MIX_W), F32),
                        pltpu.VMEM((N_PAIRS, nb_pad, 2 * tq), F32),
                        pltpu.VMEM((2, N_PAIRS, MOBA_BLOCK, 2 * tq), F32)],
        compiler_params=pltpu.CompilerParams(
            dimension_semantics=("parallel", "arbitrary"), vmem_limit_bytes=VMEM_LIMIT),
        name="moba",
    )(q, k, vt, z)


def _out_kernel(ma_ref, mb_ref, mc_ref, md_ref, w_ref, x_ref, mod_ref, g_ref, o_ref):
    D = x_ref.shape[-1]
    acc = _dot(ma_ref[0], w_ref[0:MIX_W, :])
    acc = acc + _dot(mb_ref[0], w_ref[MIX_W:2 * MIX_W, :])
    acc = acc + _dot(mc_ref[0], w_ref[2 * MIX_W:3 * MIX_W, :])
    acc = acc + _dot(md_ref[0], w_ref[3 * MIX_W:, :])
    ms = jnp.mean(acc * acc, axis=-1, keepdims=True)
    y = acc * lax.rsqrt(ms + EPS) * g_ref[...]
    o_ref[0] = x_ref[0] + mod_ref[0, :, 2 * D:] * y


def _out_projection(mixed, w_out, x, mod, g_post, tm):
    B, S, D = x.shape
    row = lambda w: pl.BlockSpec((1, tm, w), lambda b, i: (b, i, 0))
    return pl.pallas_call(
        _out_kernel,
        grid=(B, S // tm),
        in_specs=[row(MIX_W)] * 4 + [pl.BlockSpec(w_out.shape, lambda b, i: (0, 0)), row(D),
                                     pl.BlockSpec((1, 1, 3 * D), lambda b, i: (b, 0, 0)),
                                     pl.BlockSpec((1, D), lambda b, i: (0, 0))],
        out_specs=row(D),
        out_shape=jax.ShapeDtypeStruct((B, S, D), F32),
        compiler_params=pltpu.CompilerParams(
            dimension_semantics=("parallel", "parallel"), vmem_limit_bytes=VMEM_LIMIT),
        name="out_projection",
    )(*mixed, w_out.astype(BF16), x, mod[:, None, :], g_post[None])


def _layer(x, c, norm_pre, norm_post, w_mod, b_mod, w_in, w_out, cmp_pos, cmp_w1, cmp_b1, cmp_w2,
           tables, tables_c, tm):
    mod = _modulation(c, w_mod, b_mod)
    (qa, ka, za, qb, kb, vb, zb, qc, kcr, vcr, gc, ks2, kw2, zc,
     qd, kd, zd, vat, vdt, vst, vwt) = _projection(x, mod, norm_pre, w_in, tables, tm)
    oa = _stick_breaking(qa, ka, vat, za)
    ob = _dilated(qb, kb, vb, zb)
    kc2, vct = _compress(kcr, vcr, cmp_pos, cmp_w1, cmp_b1, cmp_w2, tables_c)
    oc = _nsa(qc, kc2, vct, ks2, vst, kw2, vwt, gc, zc)
    od = _moba(qd, kd, vdt, zd)
    return _out_projection((oa, ob, oc, od), w_out, x, mod, norm_post, tm)


def kernel(x, c, norm_pre, norm_post, w_mod, b_mod, w_in, w_out, cmp_pos, cmp_w1, cmp_b1, cmp_w2):
    S = x.shape[1]
    tables = _rope_lane_tables(np.arange(S))
    tables_c = _rope_lane_tables(np.arange(S // CMP_STRIDE) * CMP_STRIDE + CMP_LEN - 1)
    tm = min(512, S)
    for l in range(norm_pre.shape[0]):
        x = _layer(x, c, norm_pre[l], norm_post[l], w_mod[l], b_mod[l], w_in[l], w_out[l],
                   cmp_pos[l], cmp_w1[l], cmp_b1[l], cmp_w2[l], tables, tables_c, tm)
    return x
```

```python
import functools

import numpy as np
import jax
import jax.numpy as jnp
from jax import lax
from jax.experimental import pallas as pl
from jax.experimental.pallas import tpu as pltpu

F32 = jnp.float32
BF16 = jnp.bfloat16

N_HEADS = 4
HEAD_DIM = 64
MIX_W = N_HEADS * HEAD_DIM
ROPE_THETA = 500000.0
ROPE_DIM = HEAD_DIM // 4
ROPE_HALF = ROPE_DIM // 2
EPS = 1e-6
NEG = -1e30
BIG = 1e9
SCALE = HEAD_DIM ** -0.5
LOG2E = 1.4426950408889634
SCALE_LOG2 = SCALE * LOG2E
DIL_CONFIGS = ((128, 1), (512, 4), (2048, 16))
CMP_LEN = 32
CMP_STRIDE = 16
CMP_HID = 256
SEL_LEN = 64
SEL_TOPN = 16
WIN_LEN = 512
MOBA_BLOCK = 256
MOBA_TOPK = 3

N_PAIRS = N_HEADS // 2
LANES = 128
SUBLANES = 8
TILE_W = 2 * LANES
KEY_TILE = 256
DIL_GROUP = 2048
VMEM_LIMIT = 56 * 1024 * 1024


def _nt_dot(a, b):
    return lax.dot_general(a, b, (((1,), (1,)), ((), ())), preferred_element_type=F32)


def _dot(a, b):
    return jnp.dot(a, b, preferred_element_type=F32)


def _split_bf16(x):
    hi = x.astype(BF16)
    lo = (x - hi.astype(F32)).astype(BF16)
    return hi, lo


def _silu(x):
    return x * jax.nn.sigmoid(x)


def _head_masks(shape):
    lane = lax.broadcasted_iota(jnp.int32, shape, 1)
    return lane < HEAD_DIM


def _pv_pair(vt, p, tq):
    return jnp.concatenate([_dot(vt[:HEAD_DIM], p[:, :tq]), _dot(vt[HEAD_DIM:], p[:, tq:])], axis=1)


def _mod_kernel(c_ref, w_ref, b_ref, o_ref):
    c = c_ref[...]
    o_ref[...] = jnp.dot(_silu(c), w_ref[...], preferred_element_type=F32,
                         precision=lax.Precision.HIGHEST) + b_ref[...]


def _modulation(c, w_mod, b_mod):
    B, D = c.shape
    N = w_mod.shape[1]
    rows = SUBLANES
    cp = jnp.zeros((rows, D), F32).at[:B].set(c)
    tn = N // 4
    out = pl.pallas_call(
        _mod_kernel,
        grid=(N // tn,),
        in_specs=[pl.BlockSpec((rows, D), lambda j: (0, 0)),
                  pl.BlockSpec((D, tn), lambda j: (0, j)),
                  pl.BlockSpec((1, tn), lambda j: (0, j))],
        out_specs=pl.BlockSpec((rows, tn), lambda j: (0, j)),
        out_shape=jax.ShapeDtypeStruct((rows, N), F32),
        name="modulation",
    )(cp, w_mod, b_mod[None])
    return out[:B]


def _rope_lane_tile(y, cos, sa, sb):
    return y * cos + pltpu.roll(y, LANES - ROPE_HALF, 1) * sa + pltpu.roll(y, ROPE_HALF, 1) * sb


def _pack_weights(win_ref, w_ref, wt_ref):
    D = win_ref.shape[0]
    o = IN_OFFS
    chunk = 256
    for r0 in range(0, D, chunk):
        rs = slice(r0, r0 + chunk)

        def cp(dst, a, b):
            w_ref[rs, dst:dst + (b - a)] = win_ref[rs, a:b].astype(BF16)

        cp(0, o[0], o[11])
        small = PROJ_TILES["c_small"] * TILE_W
        cp(small + LANES, o[15], o[16])
        pad = small + LANES + 3 * N_HEADS
        w_ref[rs, pad:small + TILE_W] = jnp.zeros((chunk, small + TILE_W - pad), BF16)
        keys = PROJ_TILES["c_keys"] * TILE_W
        for rep in range(2):
            cp(keys + rep * HEAD_DIM, o[11], o[12])
            cp(keys + LANES + rep * HEAD_DIM, o[13], o[14])
        cp(PROJ_TILES["zc"] * TILE_W, o[16], o[21])

    def put_t(dst, a, lo=0, n=LANES):
        wt_ref[dst:dst + n, :] = win_ref[:, a:a + LANES].astype(F32).T[lo:lo + n].astype(BF16)

    for h in range(2):
        put_t(h * LANES, o[2] + h * LANES)
        put_t(MIX_W + h * LANES, o[19] + h * LANES)
    put_t(2 * MIX_W, o[11], HEAD_DIM, HEAD_DIM)
    put_t(2 * MIX_W + HEAD_DIM, o[13], HEAD_DIM, HEAD_DIM)


def _proj_kernel(x_ref, mod_ref, g_ref, win_ref, cos_ref, sa_ref, sb_ref,
                 qa_ref, ka_ref, za_ref,
                 qb_ref, kb_ref, vb_ref, zb_ref,
                 qc_ref, kcr_ref, vcr_ref, gc_ref, ks_ref, kw_ref, zc_ref,
                 qd_ref, kd_ref, zd_ref, vat_ref, vdt_ref, vst_ref, vwt_ref, w_ref, wt_ref):
    @pl.when(jnp.logical_and(pl.program_id(0) == 0, pl.program_id(1) == 0))
    def _():
        _pack_weights(win_ref, w_ref, wt_ref)

    D = x_ref.shape[-1]
    x = x_ref[0]
    ms = jnp.mean(x * x, axis=-1, keepdims=True)
    y = x * lax.rsqrt(ms + EPS) * g_ref[...]
    shift = mod_ref[0, :, 0:D]
    scale = mod_ref[0, :, D:2 * D]
    h = (y * (1.0 + scale) + shift).astype(BF16)
    cos, sa, sb = cos_ref[...], sa_ref[...], sb_ref[...]

    def tile(t):
        return _dot(h, w_ref[:, t * TILE_W:(t + 1) * TILE_W])

    def rope_lo(y):
        return _rope_lane_tile(y[:, :LANES], cos, sa, sb)

    def rope_all(y):
        return jnp.concatenate([rope_lo(y), _rope_lane_tile(y[:, LANES:], cos, sa, sb)], axis=1)

    T = PROJ_TILES
    qa_ref[0] = tile(T["qa"]).astype(BF16)
    ka_ref[0] = tile(T["ka"]).astype(BF16)
    za_ref[0] = tile(T["za"]).astype(BF16)
    qb_ref[0] = rope_all(tile(T["qb"]))
    kb_ref[0] = rope_all(tile(T["kb"]))
    vb_ref[0] = tile(T["vb"])
    zb_ref[0] = tile(T["zb"]).astype(BF16)
    qc_ref[0] = (rope_all(tile(T["qc"])) * SCALE_LOG2).astype(BF16)
    small = tile(T["c_small"])
    kcr_ref[0] = small[:, 0:HEAD_DIM]
    vcr_ref[0] = small[:, HEAD_DIM:LANES]
    gc_ref[0] = small[:, LANES:]
    keys = rope_all(tile(T["c_keys"])).astype(BF16)
    ks_ref[0] = keys[:, :LANES]
    kw_ref[0] = keys[:, LANES:]
    zc_ref[0] = tile(T["zc"]).astype(BF16)
    qd_ref[0] = (rope_all(tile(T["qd"])) * SCALE_LOG2).astype(BF16)
    kd_ref[0] = rope_all(tile(T["kd"])).astype(BF16)
    zd_ref[0] = tile(T["zd"]).astype(BF16)
    vt = _nt_dot(wt_ref[...], h).astype(BF16)
    vat_ref[0] = vt[0:MIX_W]
    vdt_ref[0] = vt[MIX_W:2 * MIX_W]
    vst_ref[0] = vt[2 * MIX_W:2 * MIX_W + HEAD_DIM]
    vwt_ref[0] = vt[2 * MIX_W + HEAD_DIM:]


PROJ_TILES = dict(qa=0, ka=1, za=3, qb=4, kb=5, vb=6, zb=7, qc=8, c_small=9, c_keys=10,
                  zc=11, qd=12, kd=13, zd=15)
N_PROJ_TILES = 16
N_VT_ROWS = 2 * MIX_W + 2 * HEAD_DIM
IN_OFFS = [int(v) for v in np.cumsum([0] + [MIX_W] * 9 + [HEAD_DIM] * 6 + [3 * N_HEADS] + [MIX_W] * 5)]


def _rope_angles(pos):
    inv_freq = 1.0 / (ROPE_THETA ** (np.arange(0, ROPE_DIM, 2, dtype=np.float32) / ROPE_DIM))
    ang = jnp.asarray(pos, F32)[:, None] * jnp.asarray(inv_freq, F32)[None, :]
    return jnp.cos(ang), jnp.sin(ang)


def _rope_lane_tables(pos):
    cos, sin = _rope_angles(pos)
    n = cos.shape[0]
    rest = HEAD_DIM - ROPE_DIM
    cos_h = jnp.concatenate([cos, cos, jnp.ones((n, rest), F32)], axis=1)
    sa_h = jnp.concatenate([-sin, jnp.zeros((n, HEAD_DIM - ROPE_HALF), F32)], axis=1)
    sb_h = jnp.concatenate([jnp.zeros((n, ROPE_HALF), F32), sin, jnp.zeros((n, rest), F32)], axis=1)
    two = lambda t: jnp.concatenate([t, t], axis=1)
    return two(cos_h), two(sa_h), two(sb_h)


def _projection(x, mod, g_pre, w_in, tables, tm):
    B, S, D = x.shape
    cos, sa, sb = tables
    w_in = jnp.pad(w_in.astype(BF16), ((0, 0), (0, -w_in.shape[1] % LANES)))
    row = lambda w: pl.BlockSpec((1, tm, w), lambda b, i: (b, i, 0))
    tab = pl.BlockSpec((tm, LANES), lambda b, i: (i, 0))
    widths = [(MIX_W, BF16)] * 3 + [(MIX_W, F32)] * 3 + [(MIX_W, BF16)] + \
             [(MIX_W, BF16), (HEAD_DIM, F32), (HEAD_DIM, F32), (LANES, F32),
              (LANES, BF16), (LANES, BF16), (MIX_W, BF16)] + \
             [(MIX_W, BF16)] * 3
    t_rows = (MIX_W, MIX_W, HEAD_DIM, HEAD_DIM)
    t_specs = [pl.BlockSpec((1, r, tm), lambda b, i: (b, 0, i)) for r in t_rows]
    t_shapes = [jax.ShapeDtypeStruct((B, r, S), BF16) for r in t_rows]
    return pl.pallas_call(
        _proj_kernel,
        grid=(B, S // tm),
        in_specs=[row(D),
                  pl.BlockSpec((1, 1, 3 * D), lambda b, i: (b, 0, 0)),
                  pl.BlockSpec((1, D), lambda b, i: (0, 0)),
                  pl.BlockSpec(w_in.shape, lambda b, i: (0, 0), pipeline_mode=pl.Buffered(1)),
                  tab, tab, tab],
        out_specs=[row(w) for w, _ in widths] + t_specs,
        out_shape=[jax.ShapeDtypeStruct((B, S, w), dt) for w, dt in widths] + t_shapes,
        scratch_shapes=[pltpu.VMEM((D, N_PROJ_TILES * TILE_W), BF16),
                        pltpu.VMEM((N_VT_ROWS, D), BF16)],
        compiler_params=pltpu.CompilerParams(
            dimension_semantics=("arbitrary", "arbitrary"), vmem_limit_bytes=VMEM_LIMIT),
        name="in_projection",
    )(x, mod[:, None, :], g_pre[None], w_in, cos, sa, sb)


def _all_pair_specs(S, tq):
    qspec = pl.BlockSpec((1, tq, MIX_W), lambda b, i: (b, i, 0))
    kspec = pl.BlockSpec((1, S, MIX_W), lambda b, i: (b, 0, 0))
    vtspec = pl.BlockSpec((1, MIX_W, S), lambda b, i: (b, 0, 0))
    return qspec, kspec, vtspec


def _pair_lanes(p):
    return slice(p * LANES, (p + 1) * LANES)


def _tile_rows(ref, j, tk):
    return ref[0, pl.ds(pl.multiple_of(j * tk, tk), tk), :]


def _tile_rows_2d(ref, j, tk):
    return ref[pl.ds(pl.multiple_of(j * tk, tk), tk), :]


def _tile_lanes(ref, j, tk):
    return ref[0, :, pl.ds(pl.multiple_of(j * tk, tk), tk)]


def _stack_pair(q, is_a):
    return jnp.concatenate([jnp.where(is_a, q, 0), jnp.where(is_a, 0, q)], axis=0)


def _rank_rows(v, n):
    rows = v.shape[0]
    rank = jnp.zeros(v.shape, F32)
    for s in range(n):
        c = v[s:s + 1, :]
        lo = s // SUBLANES * SUBLANES
        hi = lo + SUBLANES
        grp = lax.broadcasted_iota(jnp.int32, (SUBLANES, v.shape[1]), 0) + lo
        mixed = jnp.logical_or(c > v[lo:hi], jnp.logical_and(c == v[lo:hi], grp > s))
        parts = [jnp.where(c > v[:lo], 1.0, 0.0)] if lo else []
        parts.append(jnp.where(mixed, 1.0, 0.0))
        if hi < rows:
            parts.append(jnp.where(c >= v[hi:], 1.0, 0.0))
        rank = rank + jnp.concatenate(parts, axis=0)
    return rank


def _pipelined(n, first, produce, consume, state, scr, quads=False):
    def put(slot, tiles):
        for a, tile in enumerate(tiles):
            scr[slot, a] = tile

    def get(slot):
        return [scr[slot, a] for a in range(scr.shape[1])]

    def step(t, slot, carry):
        small, st = carry
        small_new, tiles = produce(t)
        put(1 - slot, tiles)
        return small_new, consume((small, get(slot)), st)

    def pair(t, c):
        return step(t + 1, 1, step(t, 0, c))

    put(0, first[1])
    carry = (first[0], state)
    done = 0
    if quads:
        carry = lax.fori_loop(0, lax.shift_right_logical(n, 2),
                              lambda u, c: pair(4 * u + 2, pair(4 * u, c)), carry)
        done = lax.shift_left(lax.shift_right_logical(n, 2), 2)
    carry = lax.fori_loop(0, lax.shift_right_logical(n - done, 1),
                          lambda u, c: pair(done + 2 * u, c), carry)

    def odd_tail(c):
        small, st = step(n - 1, 0, c)
        return consume((small, get(1)), st)

    return lax.cond(jnp.bitwise_and(n, 1) == 1, odd_tail,
                    lambda c: consume((c[0], get(0)), c[1]), carry)


def _finish_pairs(accs, tq, z_ref, o_ref):
    heads = [a[:, h * tq:(h + 1) * tq] for a in accs for h in range(2)]
    out = jnp.concatenate(heads, axis=0).T
    o_ref[0] = (out * _silu(z_ref[0].astype(F32))).astype(BF16)


def _stick_kernel(q_ref, k_ref, vt_ref, z_ref, o_ref, tile_scr, *, tq):
    i = pl.program_id(1)
    tk = KEY_TILE
    n_diag = tq // tk
    is_a = _head_masks((tq, LANES))
    qs = (q_ref[0].astype(F32) * SCALE).astype(BF16)
    qcat = [_stack_pair(qs[:, _pair_lanes(p)], is_a) for p in range(N_PAIRS)]
    r_i = lax.broadcasted_iota(jnp.int32, (tk, tk), 0)
    c_i = lax.broadcasted_iota(jnp.int32, (tk, tk), 1)
    suffix = jnp.where(c_i >= r_i, 1.0, 0.0).astype(BF16)
    kloc = lax.broadcasted_iota(jnp.int32, (tk, 2 * tq), 0)
    qloc = jnp.bitwise_and(lax.broadcasted_iota(jnp.int32, (1, 2 * tq), 1), tq - 1)

    def log_weights(j, diag):
        ks = _tile_rows(k_ref, j, tk)
        zs = [_nt_dot(ks[:, _pair_lanes(p)], qcat[p]) for p in range(N_PAIRS)]
        sps = [jnp.maximum(z, 0.0) + jnp.log(1.0 + jnp.exp2(jnp.abs(z) * (-LOG2E))) for z in zs]
        if diag:
            past = kloc < qloc - (j * tk - i * tq)
            sps = [jnp.where(past, sp, 0.0) for sp in sps]
        css = [_dot(suffix, sp.astype(BF16)) for sp in sps]
        xs = [z - cs for z, cs in zip(zs, css)]
        if diag:
            xs = [jnp.where(past, x, NEG) for x in xs]
        return (j, [cs[0:1, :] for cs in css]), xs

    def accumulate(blk, st):
        (j, tots), xs = blk
        vts = _tile_lanes(vt_ref, j, tk)
        ws = [jnp.exp(xs[p] - st[p][1]).astype(BF16) for p in range(N_PAIRS)]
        return [(st[p][0] + _pv_pair(vts[_pair_lanes(p)], ws[p], tq), st[p][1] + tots[p])
                for p in range(N_PAIRS)]

    st = [(jnp.zeros((HEAD_DIM, 2 * tq), F32), jnp.zeros((1, 2 * tq), F32))] * N_PAIRS
    top = i * n_diag + n_diag - 1
    for d in range(n_diag - 1):
        st = accumulate(log_weights(top - d, True), st)
    st = _pipelined(i * n_diag, log_weights(i * n_diag, True),
                    lambda t: log_weights(i * n_diag - 1 - t, False), accumulate, st, tile_scr)
    _finish_pairs([acc for acc, _ in st], tq, z_ref, o_ref)


def _stick_breaking(q, k, vt, z, tq=512):
    B, S, _ = q.shape
    tq = min(tq, S)
    qspec, kspec, vtspec = _all_pair_specs(S, tq)
    return pl.pallas_call(
        functools.partial(_stick_kernel, tq=tq),
        grid=(B, S // tq),
        in_specs=[qspec, kspec, vtspec, qspec],
        out_specs=qspec,
        out_shape=jax.ShapeDtypeStruct((B, S, MIX_W), BF16),
        scratch_shapes=[pltpu.VMEM((2, N_PAIRS, KEY_TILE, 2 * tq), F32)],
        compiler_params=pltpu.CompilerParams(
            dimension_semantics=("parallel", "arbitrary"), vmem_limit_bytes=VMEM_LIMIT),
        name="stick_breaking",
    )(q, k, vt, z)


def _dil_kernel(q_ref, k_ref, v_ref, z_ref, o_ref, m_scr, l_scr, a_scr, *, G):
    g = pl.program_id(2)
    blk = LANES
    is_a = _head_masks((blk, LANES))
    qrow = jnp.bitwise_and(lax.broadcasted_iota(jnp.int32, (2 * blk, 2 * blk), 0), blk - 1)
    col = lax.broadcasted_iota(jnp.int32, (2 * blk, 2 * blk), 1)
    band = jnp.logical_and(col >= qrow, col <= qrow + blk)
    n_sub = G // blk
    unroll = 16

    for ci, (_, d) in enumerate(DIL_CONFIGS):
        nb = G // (blk * d)
        sh = int(np.log2(nb))

        def sub(n, ci=ci, d=d, nb=nb, sh=sh):
            r = lax.shift_right_logical(n, sh)
            ub = jnp.bitwise_and(n, nb - 1)
            loc = ub * (blk * d) + r
            glob = g * G + loc
            has_prev = glob >= blk * d
            pstart = jnp.maximum(glob - blk * d, r)
            qs = (q_ref[0, pl.ds(loc, blk, stride=d), :] * SCALE_LOG2).astype(BF16)
            kk = jnp.concatenate([k_ref[0, pl.ds(pstart, blk, stride=d), :],
                                  k_ref[0, pl.ds(glob, blk, stride=d), :]], axis=0).astype(BF16)
            vv = jnp.concatenate([v_ref[0, pl.ds(pstart, blk, stride=d), :],
                                  v_ref[0, pl.ds(glob, blk, stride=d), :]], axis=0).astype(BF16)
            ok = jnp.logical_and(band, jnp.logical_or(col >= blk, has_prev))
            s = jnp.where(ok, _nt_dot(_stack_pair(qs, is_a), kk), NEG)
            m = jnp.max(s, axis=-1, keepdims=True)
            p = jnp.exp2(s - m)
            l = jnp.sum(p, axis=-1, keepdims=True)
            acc = _dot(p.astype(BF16), vv)
            rows = pl.ds(loc, blk, stride=d)
            m_scr[ci, rows, :] = jnp.where(is_a, m[:blk], m[blk:])
            l_scr[ci, rows, :] = jnp.where(is_a, l[:blk], l[blk:])
            a_scr[ci, rows, :] = jnp.where(is_a, acc[:blk], acc[blk:])

        def trip(t, _, sub=sub):
            for u in range(unroll):
                sub(t * unroll + u)
            return 0

        lax.fori_loop(0, n_sub // unroll, trip, 0)

    chunk = 256

    def combine(c, _):
        rows = pl.ds(pl.multiple_of(c * chunk, chunk), chunk)
        m0, m1, m2 = m_scr[0, rows, :], m_scr[1, rows, :], m_scr[2, rows, :]
        mm = jnp.maximum(jnp.maximum(m0, m1), m2)
        w0, w1, w2 = jnp.exp2(m0 - mm), jnp.exp2(m1 - mm), jnp.exp2(m2 - mm)
        num = w0 * a_scr[0, rows, :] + w1 * a_scr[1, rows, :] + w2 * a_scr[2, rows, :]
        den = w0 * l_scr[0, rows, :] + w1 * l_scr[1, rows, :] + w2 * l_scr[2, rows, :]
        o_ref[0, rows, :] = (num / den * _silu(z_ref[0, rows, :].astype(F32))).astype(BF16)
        return 0

    lax.fori_loop(0, G // chunk, combine, 0)


def _dilated(q, k, v, z):
    B, S, _ = q.shape
    G = min(DIL_GROUP, S)
    gspec = pl.BlockSpec((1, G, LANES), lambda b, p, g: (b, g, p))
    kspec = pl.BlockSpec((1, S, LANES), lambda b, p, g: (b, 0, p))
    scr = pltpu.VMEM((len(DIL_CONFIGS), G, LANES), F32)
    return pl.pallas_call(
        functools.partial(_dil_kernel, G=G),
        grid=(B, 2, S // G),
        in_specs=[gspec, kspec, kspec, gspec],
        out_specs=gspec,
        out_shape=jax.ShapeDtypeStruct((B, S, MIX_W), BF16),
        scratch_shapes=[scr, scr, scr],
        compiler_params=pltpu.CompilerParams(
            dimension_semantics=("parallel", "parallel", "arbitrary"), vmem_limit_bytes=VMEM_LIMIT),
        name="dilated_window",
    )(q, k, v, z)


def _compress_kernel(k_ref, v_ref, p_ref, w1_ref, b1_ref, w2k_ref, w2vt_ref, cos_ref, sa_ref, sb_ref,
                     kc_ref, vc_ref):
    n = kc_ref.shape[1]
    dh = k_ref.shape[-1]
    for s, (t_ref, o_ref) in enumerate(((k_ref, kc_ref), (v_ref, vc_ref))):
        lo = jnp.zeros((n, CMP_HID), F32)
        hi = jnp.zeros((n, CMP_HID), F32)
        for l in range(CMP_STRIDE):
            t = t_ref[0, pl.ds(l, n, stride=CMP_STRIDE), :]
            lo = lo + _dot((t + p_ref[s, l:l + 1, :]).astype(BF16),
                           w1_ref[s, l * dh:(l + 1) * dh, :])
            lh = l + CMP_STRIDE
            hi = hi + _dot((t + p_ref[s, lh:lh + 1, :]).astype(BF16),
                           w1_ref[s, lh * dh:(lh + 1) * dh, :])
        pre = lo + pltpu.roll(hi, n - 1, 0) + b1_ref[s]
        hid = 0.5 * pre * (1.0 + jnp.tanh(np.sqrt(2.0 / np.pi).astype(np.float32)
                                          * (pre + 0.044715 * (pre * pre * pre))))
        hid = hid.astype(BF16)
        if s == 0:
            out = _dot(hid, w2k_ref[...])
            o_ref[0] = _rope_lane_tile(out, cos_ref[...], sa_ref[...], sb_ref[...]).astype(BF16)
        else:
            o_ref[0] = _nt_dot(w2vt_ref[...], hid).astype(BF16)


def _compress(kcr, vcr, cmp_pos, cmp_w1, cmp_b1, cmp_w2, tables_c):
    B, S, dh = kcr.shape
    n = S // CMP_STRIDE
    w1 = cmp_w1.astype(BF16)
    w2k = jnp.concatenate([cmp_w2[0], cmp_w2[0]], axis=-1).astype(BF16)
    w2vt = cmp_w2[1].T.astype(BF16)
    b1 = cmp_b1[:, None, :]
    cos, sa, sb = tables_c
    const = lambda a: pl.BlockSpec(a.shape, lambda b: (0,) * a.ndim)
    tspec = pl.BlockSpec((1, S, dh), lambda b: (b, 0, 0))
    return pl.pallas_call(
        _compress_kernel,
        grid=(B,),
        in_specs=[tspec, tspec, const(cmp_pos), const(w1), const(b1), const(w2k), const(w2vt),
                  const(cos), const(sa), const(sb)],
        out_specs=[pl.BlockSpec((1, n, LANES), lambda b: (b, 0, 0)),
                   pl.BlockSpec((1, dh, n), lambda b: (b, 0, 0))],
        out_shape=[jax.ShapeDtypeStruct((B, n, LANES), BF16),
                   jax.ShapeDtypeStruct((B, dh, n), BF16)],
        compiler_params=pltpu.CompilerParams(
            dimension_semantics=("parallel",), vmem_limit_bytes=VMEM_LIMIT),
        name="compress_tokens",
    )(kcr, vcr, cmp_pos, w1, b1, w2k, w2vt, cos, sa, sb)


def _nsa_kernel(q_ref, kc_ref, vct_ref, ks_ref, vst_ref, kw_ref, vwt_ref, gc_ref, z_ref,
                ovt_ref, ext_ref, o_ref, tile_scr, *, tq, tk, n_sel):
    i = pl.program_id(1)
    Q = N_HEADS * tq
    is_a = _head_masks((tq, LANES))
    qs = q_ref[0]
    q4 = jnp.concatenate([_stack_pair(qs[:, :LANES], is_a), _stack_pair(qs[:, LANES:], is_a)], axis=0)
    heads = lambda x: jnp.concatenate([x] * N_HEADS, axis=1)
    qpos1 = i * tq + lax.broadcasted_iota(jnp.int32, (1, tq), 1)
    qpos = heads(qpos1)

    kpos0 = lax.broadcasted_iota(jnp.int32, (tk, Q), 0)
    J = (i * tq) // tk

    n_c = kc_ref.shape[1]
    c_end = lax.broadcasted_iota(jnp.int32, (n_c, Q), 0) * CMP_STRIDE + (CMP_LEN - 1)
    c_valid = c_end <= qpos
    sc = jnp.where(c_valid, _nt_dot(kc_ref[0], q4), NEG)
    pc = jnp.exp2(sc - jnp.max(sc, axis=0, keepdims=True)) * jnp.where(c_valid, 1.0, 0.0)
    lc = jnp.sum(pc, axis=0, keepdims=True)
    pc = pc * jnp.where(lc > 0.0, 1.0 / lc, 0.0)
    o_cmp = _dot(vct_ref[0], pc.astype(BF16))

    pcs = pc[:, 0:tq] + pc[:, tq:2 * tq] + pc[:, 2 * tq:3 * tq] + pc[:, 3 * tq:]
    hi, lo = _split_bf16(pcs)
    imp = _dot(ovt_ref[...], hi) + _dot(ovt_ref[...], lo)
    nsp = imp.shape[0]
    blk = lax.broadcasted_iota(jnp.int32, (nsp, tq), 0)
    own = lax.shift_right_logical(qpos1, int(np.log2(SEL_LEN)))
    s_valid = blk <= own
    forced = jnp.logical_or(blk == 0, blk >= own - 1)
    imp = jnp.where(s_valid, jnp.where(forced, BIG, imp), NEG)
    rank = _rank_rows(imp, n_sel)
    chosen = jnp.logical_and(rank < float(min(SEL_TOPN, n_sel)), s_valid)
    sel = jnp.where(chosen, 1.0, 0.0).astype(BF16)

    n_win = -(-WIN_LEN // tk) + 1
    s_w, vt_w = [], []
    for a in range(n_win):
        j_a = J - (n_win - 1) + a
        qrel = qpos - j_a * tk
        ok = jnp.broadcast_to(j_a >= 0, (tk, Q))
        if a == 0:
            ok = jnp.logical_and(ok, kpos0 > qrel - WIN_LEN)
        if a == n_win - 1:
            ok = kpos0 <= qrel
        j_c = jnp.maximum(j_a, 0)
        s_w.append(jnp.where(ok, _nt_dot(_tile_rows(kw_ref, j_c, tk), q4), NEG))
        vt_w.append(_tile_lanes(vwt_ref, j_c, tk))
    m_w = functools.reduce(jnp.maximum, [jnp.max(s, axis=0, keepdims=True) for s in s_w])
    p_w = [jnp.exp2(s - m_w) for s in s_w]
    l_w = sum(jnp.sum(p, axis=0, keepdims=True) for p in p_w)
    o_win = sum(_dot(vt, p.astype(BF16)) for vt, p in zip(vt_w, p_w)) / l_w

    def sel_scores(j, diag):
        bias = (_dot(_tile_rows_2d(ext_ref, j, tk), sel) - 1.0) * (-NEG)
        s = _nt_dot(_tile_rows(ks_ref, j, tk), q4) + heads(bias)
        if diag:
            s = jnp.where(j * tk + kpos0 <= qpos, s, NEG)
        return (j, jnp.max(s, axis=0, keepdims=True)), [s]

    def consume(blk_s, st):
        (j, s_max), (s,) = blk_s
        m, l, acc = st
        m_new = jnp.maximum(m, s_max)
        alpha = jnp.exp2(m - m_new)
        p = jnp.exp2(s - m_new)
        l = alpha * l + jnp.sum(p, axis=0, keepdims=True)
        acc = alpha * acc + _dot(_tile_lanes(vst_ref, j, tk), p.astype(BF16))
        return m_new, l, acc

    init = (jnp.full((1, Q), NEG, F32), jnp.zeros((1, Q), F32), jnp.zeros((HEAD_DIM, Q), F32))

    _, l_s, a_s = _pipelined(J, sel_scores(J, True), lambda t: sel_scores(t, False), consume, init,
                             tile_scr, quads=True)
    o_sel = a_s / l_s

    gates = jax.nn.sigmoid(gc_ref[0]).T
    outs = []
    for h in range(N_HEADS):
        cols = slice(h * tq, (h + 1) * tq)
        outs.append(gates[3 * h:3 * h + 1, :] * o_cmp[:, cols]
                    + gates[3 * h + 1:3 * h + 2, :] * o_sel[:, cols]
                    + gates[3 * h + 2:3 * h + 3, :] * o_win[:, cols])
    out = jnp.concatenate(outs, axis=0).T
    o_ref[0] = (out * _silu(z_ref[0].astype(F32))).astype(BF16)


def _nsa(q, kc2, vct, ks2, vst, kw2, vwt, gc, z, tq=256, tk=256):
    B, S, _ = q.shape
    n_c = kc2.shape[1]
    n_sel = S // SEL_LEN
    nsp = -(-n_sel // SUBLANES) * SUBLANES
    n_cmp = (S - CMP_LEN) // CMP_STRIDE + 1
    c_start = np.arange(n_c) * CMP_STRIDE
    s_start = np.arange(nsp) * SEL_LEN
    overlap_t = np.clip(np.minimum(c_start[None, :] + CMP_LEN, s_start[:, None] + SEL_LEN)
                        - np.maximum(c_start[None, :], s_start[:, None]), 0, None) / CMP_LEN
    overlap_t[:, n_cmp:] = 0.0
    overlap_t[n_sel:, :] = 0.0
    expand_t = (np.arange(S)[:, None] // SEL_LEN == np.arange(nsp)[None, :]).astype(np.float32)
    row = lambda w: pl.BlockSpec((1, tq, w), lambda b, i: (b, i, 0))
    full = lambda a: pl.BlockSpec((1,) + a.shape[1:], lambda b, i: (b, 0, 0))
    const = lambda a: pl.BlockSpec(a.shape, lambda b, i: (0, 0))
    ovt = jnp.asarray(overlap_t, BF16)
    ext = jnp.asarray(expand_t, BF16)
    return pl.pallas_call(
        functools.partial(_nsa_kernel, tq=tq, tk=tk, n_sel=n_sel),
        grid=(B, S // tq),
        in_specs=[row(MIX_W), full(kc2), full(vct), full(ks2), full(vst), full(kw2), full(vwt),
                  row(LANES), row(MIX_W), const(ovt), const(ext)],
        out_specs=row(MIX_W),
        out_shape=jax.ShapeDtypeStruct((B, S, MIX_W), BF16),
        scratch_shapes=[pltpu.VMEM((2, 1, tk, N_HEADS * tq), F32)],
        compiler_params=pltpu.CompilerParams(
            dimension_semantics=("parallel", "parallel"), vmem_limit_bytes=VMEM_LIMIT),
        name="native_sparse",
    )(q, kc2, vct, ks2, vst, kw2, vwt, gc, z, ovt, ext)


def _moba_kernel(q_ref, k_ref, vt_ref, z_ref, o_ref, km_scr, bias_scr, tile_scr, *, n_blk, tq):
    i = pl.program_id(1)
    tk = MOBA_BLOCK
    n_diag = tq // tk
    nb_pad = km_scr.shape[0]
    is_a = _head_masks((tq, LANES))

    @pl.when(i == 0)
    def _():
        km_scr[...] = jnp.zeros_like(km_scr)
        for b in range(n_blk):
            kb = k_ref[0, b * tk:(b + 1) * tk, :].astype(F32)
            km_scr[b:b + 1, :] = jnp.mean(kb, axis=0, keepdims=True)

    q = q_ref[0]
    blk = lax.broadcasted_iota(jnp.int32, (nb_pad, 2 * tq), 0)
    qloc = jnp.bitwise_and(lax.broadcasted_iota(jnp.int32, (1, 2 * tq), 1), tq - 1)
    own = i * n_diag + lax.shift_right_logical(qloc, int(np.log2(tk)))
    n_top = min(MOBA_TOPK, max(n_blk - 1, 1))
    qcat = []
    for p in range(N_PAIRS):
        qcat.append(_stack_pair(q[:, _pair_lanes(p)], is_a))
        km_hi, km_lo = _split_bf16(km_scr[:, _pair_lanes(p)])
        gsc = _nt_dot(km_hi, qcat[p]) + _nt_dot(km_lo, qcat[p])
        gsc = jnp.where(blk < own, gsc, NEG)
        rank = _rank_rows(gsc, n_blk)
        attends = jnp.logical_or(jnp.logical_and(rank < float(n_top), blk < own), blk == own)
        bias_scr[p] = jnp.where(attends, 0.0, NEG)

    kloc = lax.broadcasted_iota(jnp.int32, (tk, 2 * tq), 0)

    def scores(j, diag):
        ks = _tile_rows(k_ref, j, tk)
        ss = [_nt_dot(ks[:, _pair_lanes(p)], qcat[p]) for p in range(N_PAIRS)]
        if diag:
            causal = kloc <= qloc - (j * tk - i * tq)
            ss = [jnp.where(causal, s, NEG) for s in ss]
        return (j, [jnp.max(s, axis=0, keepdims=True) for s in ss]), ss

    def consume(blk_s, st):
        (j, maxes), ss = blk_s
        vts = _tile_lanes(vt_ref, j, tk)
        pr = range(N_PAIRS)
        bias = [bias_scr[p, pl.ds(j, 1), :] for p in pr]
        m_new = [jnp.maximum(st[p][0], maxes[p] + bias[p]) for p in pr]
        ps = [jnp.exp2(ss[p] - (m_new[p] - bias[p])) for p in pr]
        pvs = [_pv_pair(vts[_pair_lanes(p)], ps[p].astype(BF16), tq) for p in pr]
        out = []
        for p in pr:
            m, l, acc = st[p]
            alpha = jnp.exp2(m - m_new[p])
            out.append((m_new[p], alpha * l + jnp.sum(ps[p], axis=0, keepdims=True),
                        alpha * acc + pvs[p]))
        return out

    st = [(jnp.full((1, 2 * tq), NEG, F32), jnp.zeros((1, 2 * tq), F32),
           jnp.zeros((HEAD_DIM, 2 * tq), F32))] * N_PAIRS
    for d in range(n_diag - 1, 0, -1):
        st = consume(scores(i * n_diag + d, True), st)
    st = _pipelined(i * n_diag, scores(i * n_diag, True), lambda t: scores(t, False), consume, st,
                    tile_scr, quads=True)
    _finish_pairs([acc / l for _, l, acc in st], tq, z_ref, o_ref)


def _moba(q, k, vt, z, tq=256):
    B, S, _ = q.shape
    tq = min(tq, S)
    n_blk = S // MOBA_BLOCK
    nb_pad = -(-n_blk // SUBLANES) * SUBLANES
    qspec, kspec, vtspec = _all_pair_specs(S, tq)
    return pl.pallas_call(
        functools.partial(_moba_kernel, n_blk=n_blk, tq=tq),
        grid=(B, S // tq),
        in_specs=[qspec, kspec, vtspec, qspec],
        out_specs=qspec,
        out_shape=jax.ShapeDtypeStruct((B, S, MIX_W), BF16),
        scratch_shapes=[pltpu.VMEM((nb_pad, MIX_W), F32),
                        pltpu.VMEM((N_PAIRS, nb_pad, 2 * tq), F32),
                        pltpu.VMEM((2, N_PAIRS, MOBA_BLOCK, 2 * tq), F32)],
        compiler_params=pltpu.CompilerParams(
            dimension_semantics=("parallel", "arbitrary"), vmem_limit_bytes=VMEM_LIMIT),
        name="moba",
    )(q, k, vt, z)


def _out_kernel(ma_ref, mb_ref, mc_ref, md_ref, w_ref, x_ref, mod_ref, g_ref, o_ref):
    D = x_ref.shape[-1]
    acc = _dot(ma_ref[0], w_ref[0:MIX_W, :])
    acc = acc + _dot(mb_ref[0], w_ref[MIX_W:2 * MIX_W, :])
    acc = acc + _dot(mc_ref[0], w_ref[2 * MIX_W:3 * MIX_W, :])
    acc = acc + _dot(md_ref[0], w_ref[3 * MIX_W:, :])
    ms = jnp.mean(acc * acc, axis=-1, keepdims=True)
    y = acc * lax.rsqrt(ms + EPS) * g_ref[...]
    o_ref[0] = x_ref[0] + mod_ref[0, :, 2 * D:] * y


def _out_projection(mixed, w_out, x, mod, g_post, tm):
    B, S, D = x.shape
    row = lambda w: pl.BlockSpec((1, tm, w), lambda b, i: (b, i, 0))
    return pl.pallas_call(
        _out_kernel,
        grid=(B, S // tm),
        in_specs=[row(MIX_W)] * 4 + [pl.BlockSpec(w_out.shape, lambda b, i: (0, 0)), row(D),
                                     pl.BlockSpec((1, 1, 3 * D), lambda b, i: (b, 0, 0)),
                                     pl.BlockSpec((1, D), lambda b, i: (0, 0))],
        out_specs=row(D),
        out_shape=jax.ShapeDtypeStruct((B, S, D), F32),
        compiler_params=pltpu.CompilerParams(
            dimension_semantics=("parallel", "parallel"), vmem_limit_bytes=VMEM_LIMIT),
        name="out_projection",
    )(*mixed, w_out.astype(BF16), x, mod[:, None, :], g_post[None])


def _layer(x, c, norm_pre, norm_post, w_mod, b_mod, w_in, w_out, cmp_pos, cmp_w1, cmp_b1, cmp_w2,
           tables, tables_c, tm):
    mod = _modulation(c, w_mod, b_mod)
    (qa, ka, za, qb, kb, vb, zb, qc, kcr, vcr, gc, ks2, kw2, zc,
     qd, kd, zd, vat, vdt, vst, vwt) = _projection(x, mod, norm_pre, w_in, tables, tm)
    oa = _stick_breaking(qa, ka, vat, za)
    ob = _dilated(qb, kb, vb, zb)
    kc2, vct = _compress(kcr, vcr, cmp_pos, cmp_w1, cmp_b1, cmp_w2, tables_c)
    oc = _nsa(qc, kc2, vct, ks2, vst, kw2, vwt, gc, zc)
    od = _moba(qd, kd, vdt, zd)
    return _out_projection((oa, ob, oc, od), w_out, x, mod, norm_post, tm)


def kernel(x, c, norm_pre, norm_post, w_mod, b_mod, w_in, w_out, cmp_pos, cmp_w1, cmp_b1, cmp_w2):
    S = x.shape[1]
    tables = _rope_lane_tables(np.arange(S))
    tables_c = _rope_lane_tables(np.arange(S // CMP_STRIDE) * CMP_STRIDE + CMP_LEN - 1)
    tm = min(512, S)
    for l in range(norm_pre.shape[0]):
        x = _layer(x, c, norm_pre[l], norm_post[l], w_mod[l], b_mod[l], w_in[l], w_out[l],
                   cmp_pos[l], cmp_w1[l], cmp_b1[l], cmp_w2[l], tables, tables_c, tm)
    return x
```

```python
import functools

import numpy as np
import jax
import jax.numpy as jnp
from jax import lax
from jax.experimental import pallas as pl
from jax.experimental.pallas import tpu as pltpu

F32 = jnp.float32
BF16 = jnp.bfloat16

N_HEADS = 4
HEAD_DIM = 64
MIX_W = N_HEADS * HEAD_DIM
ROPE_THETA = 500000.0
ROPE_DIM = HEAD_DIM // 4
ROPE_HALF = ROPE_DIM // 2
EPS = 1e-6
NEG = -1e30
BIG = 1e9
SCALE = HEAD_DIM ** -0.5
LOG2E = 1.4426950408889634
SCALE_LOG2 = SCALE * LOG2E
DIL_CONFIGS = ((128, 1), (512, 4), (2048, 16))
CMP_LEN = 32
CMP_STRIDE = 16
CMP_HID = 256
SEL_LEN = 64
SEL_TOPN = 16
WIN_LEN = 512
MOBA_BLOCK = 256
MOBA_TOPK = 3

N_PAIRS = N_HEADS // 2
LANES = 128
SUBLANES = 8
TILE_W = 2 * LANES
KEY_TILE = 256
DIL_GROUP = 2048
VMEM_LIMIT = 56 * 1024 * 1024


def _nt_dot(a, b):
    return lax.dot_general(a, b, (((1,), (1,)), ((), ())), preferred_element_type=F32)


def _dot(a, b):
    return jnp.dot(a, b, preferred_element_type=F32)


def _split_bf16(x):
    hi = x.astype(BF16)
    lo = (x - hi.astype(F32)).astype(BF16)
    return hi, lo


def _silu(x):
    return x * jax.nn.sigmoid(x)


def _head_masks(shape):
    lane = lax.broadcasted_iota(jnp.int32, shape, 1)
    return lane < HEAD_DIM


def _pv_pair(vt, p, tq):
    return jnp.concatenate([_dot(vt[:HEAD_DIM], p[:, :tq]), _dot(vt[HEAD_DIM:], p[:, tq:])], axis=1)


def _mod_kernel(c_ref, w_ref, b_ref, o_ref):
    c = c_ref[...]
    o_ref[...] = jnp.dot(_silu(c), w_ref[...], preferred_element_type=F32,
                         precision=lax.Precision.HIGHEST) + b_ref[...]


def _modulation(c, w_mod, b_mod):
    B, D = c.shape
    N = w_mod.shape[1]
    rows = SUBLANES
    cp = jnp.zeros((rows, D), F32).at[:B].set(c)
    tn = N // 4
    out = pl.pallas_call(
        _mod_kernel,
        grid=(N // tn,),
        in_specs=[pl.BlockSpec((rows, D), lambda j: (0, 0)),
                  pl.BlockSpec((D, tn), lambda j: (0, j)),
                  pl.BlockSpec((1, tn), lambda j: (0, j))],
        out_specs=pl.BlockSpec((rows, tn), lambda j: (0, j)),
        out_shape=jax.ShapeDtypeStruct((rows, N), F32),
        name="modulation",
    )(cp, w_mod, b_mod[None])
    return out[:B]


def _rope_lane_tile(y, cos, sa, sb):
    return y * cos + pltpu.roll(y, LANES - ROPE_HALF, 1) * sa + pltpu.roll(y, ROPE_HALF, 1) * sb


def _pack_weights(win_ref, w_ref, wt_ref):
    D = win_ref.shape[0]
    o = IN_OFFS
    chunk = 256
    for r0 in range(0, D, chunk):
        rs = slice(r0, r0 + chunk)

        def cp(dst, a, b):
            w_ref[rs, dst:dst + (b - a)] = win_ref[rs, a:b].astype(BF16)

        cp(0, o[0], o[11])
        small = PROJ_TILES["c_small"] * TILE_W
        cp(small + LANES, o[15], o[16])
        pad = small + LANES + 3 * N_HEADS
        w_ref[rs, pad:small + TILE_W] = jnp.zeros((chunk, small + TILE_W - pad), BF16)
        keys = PROJ_TILES["c_keys"] * TILE_W
        for rep in range(2):
            cp(keys + rep * HEAD_DIM, o[11], o[12])
            cp(keys + LANES + rep * HEAD_DIM, o[13], o[14])
        cp(PROJ_TILES["zc"] * TILE_W, o[16], o[21])

    def put_t(dst, a, lo=0, n=LANES):
        wt_ref[dst:dst + n, :] = win_ref[:, a:a + LANES].astype(F32).T[lo:lo + n].astype(BF16)

    for h in range(2):
        put_t(h * LANES, o[2] + h * LANES)
        put_t(MIX_W + h * LANES, o[19] + h * LANES)
    put_t(2 * MIX_W, o[11], HEAD_DIM, HEAD_DIM)
    put_t(2 * MIX_W + HEAD_DIM, o[13], HEAD_DIM, HEAD_DIM)


def _proj_kernel(x_ref, mod_ref, g_ref, win_ref, cos_ref, sa_ref, sb_ref,
                 qa_ref, ka_ref, za_ref,
                 qb_ref, kb_ref, vb_ref, zb_ref,
                 qc_ref, kcr_ref, vcr_ref, gc_ref, ks_ref, kw_ref, zc_ref,
                 qd_ref, kd_ref, zd_ref, vat_ref, vdt_ref, vst_ref, vwt_ref, w_ref, wt_ref):
    @pl.when(jnp.logical_and(pl.program_id(0) == 0, pl.program_id(1) == 0))
    def _():
        _pack_weights(win_ref, w_ref, wt_ref)

    D = x_ref.shape[-1]
    x = x_ref[0]
    ms = jnp.mean(x * x, axis=-1, keepdims=True)
    y = x * lax.rsqrt(ms + EPS) * g_ref[...]
    shift = mod_ref[0, :, 0:D]
    scale = mod_ref[0, :, D:2 * D]
    h = (y * (1.0 + scale) + shift).astype(BF16)
    cos, sa, sb = cos_ref[...], sa_ref[...], sb_ref[...]

    def tile(t):
        return _dot(h, w_ref[:, t * TILE_W:(t + 1) * TILE_W])

    def rope_lo(y):
        return _rope_lane_tile(y[:, :LANES], cos, sa, sb)

    def rope_all(y):
        return jnp.concatenate([rope_lo(y), _rope_lane_tile(y[:, LANES:], cos, sa, sb)], axis=1)

    T = PROJ_TILES
    qa_ref[0] = tile(T["qa"]).astype(BF16)
    ka_ref[0] = tile(T["ka"]).astype(BF16)
    za_ref[0] = tile(T["za"]).astype(BF16)
    qb_ref[0] = rope_all(tile(T["qb"]))
    kb_ref[0] = rope_all(tile(T["kb"]))
    vb_ref[0] = tile(T["vb"])
    zb_ref[0] = tile(T["zb"]).astype(BF16)
    qc_ref[0] = (rope_all(tile(T["qc"])) * SCALE_LOG2).astype(BF16)
    small = tile(T["c_small"])
    kcr_ref[0] = small[:, 0:HEAD_DIM]
    vcr_ref[0] = small[:, HEAD_DIM:LANES]
    gc_ref[0] = small[:, LANES:]
    keys = rope_all(tile(T["c_keys"])).astype(BF16)
    ks_ref[0] = keys[:, :LANES]
    kw_ref[0] = keys[:, LANES:]
    zc_ref[0] = tile(T["zc"]).astype(BF16)
    qd_ref[0] = (rope_all(tile(T["qd"])) * SCALE_LOG2).astype(BF16)
    kd_ref[0] = rope_all(tile(T["kd"])).astype(BF16)
    zd_ref[0] = tile(T["zd"]).astype(BF16)
    vt = _nt_dot(wt_ref[...], h).astype(BF16)
    vat_ref[0] = vt[0:MIX_W]
    vdt_ref[0] = vt[MIX_W:2 * MIX_W]
    vst_ref[0] = vt[2 * MIX_W:2 * MIX_W + HEAD_DIM]
    vwt_ref[0] = vt[2 * MIX_W + HEAD_DIM:]


PROJ_TILES = dict(qa=0, ka=1, za=3, qb=4, kb=5, vb=6, zb=7, qc=8, c_small=9, c_keys=10,
                  zc=11, qd=12, kd=13, zd=15)
N_PROJ_TILES = 16
N_VT_ROWS = 2 * MIX_W + 2 * HEAD_DIM
IN_OFFS = [int(v) for v in np.cumsum([0] + [MIX_W] * 9 + [HEAD_DIM] * 6 + [3 * N_HEADS] + [MIX_W] * 5)]


def _rope_lane_tables(pos):
    f32 = np.float32
    inv_freq = (1.0 / (ROPE_THETA ** (np.arange(0, ROPE_DIM, 2, dtype=f32) / ROPE_DIM))).astype(f32)
    ang = np.asarray(pos, f32)[:, None] * inv_freq[None, :]
    cos, sin = np.cos(ang).astype(f32), np.sin(ang).astype(f32)
    n = cos.shape[0]
    rest = HEAD_DIM - ROPE_DIM
    cos_h = np.concatenate([cos, cos, np.ones((n, rest), f32)], axis=1)
    sa_h = np.concatenate([-sin, np.zeros((n, HEAD_DIM - ROPE_HALF), f32)], axis=1)
    sb_h = np.concatenate([np.zeros((n, ROPE_HALF), f32), sin, np.zeros((n, rest), f32)], axis=1)
    two = lambda t: jnp.asarray(np.concatenate([t, t], axis=1))
    return two(cos_h), two(sa_h), two(sb_h)


def _projection(x, mod, g_pre, w_in, tables, tm):
    B, S, D = x.shape
    cos, sa, sb = tables
    w_in = jnp.pad(w_in.astype(BF16), ((0, 0), (0, -w_in.shape[1] % LANES)))
    row = lambda w: pl.BlockSpec((1, tm, w), lambda b, i: (b, i, 0))
    tab = pl.BlockSpec((tm, LANES), lambda b, i: (i, 0))
    widths = [(MIX_W, BF16)] * 3 + [(MIX_W, F32)] * 3 + [(MIX_W, BF16)] + \
             [(MIX_W, BF16), (HEAD_DIM, F32), (HEAD_DIM, F32), (LANES, F32),
              (LANES, BF16), (LANES, BF16), (MIX_W, BF16)] + \
             [(MIX_W, BF16)] * 3
    t_rows = (MIX_W, MIX_W, HEAD_DIM, HEAD_DIM)
    t_specs = [pl.BlockSpec((1, r, tm), lambda b, i: (b, 0, i)) for r in t_rows]
    t_shapes = [jax.ShapeDtypeStruct((B, r, S), BF16) for r in t_rows]
    return pl.pallas_call(
        _proj_kernel,
        grid=(B, S // tm),
        in_specs=[row(D),
                  pl.BlockSpec((1, 1, 3 * D), lambda b, i: (b, 0, 0)),
                  pl.BlockSpec((1, D), lambda b, i: (0, 0)),
                  pl.BlockSpec(w_in.shape, lambda b, i: (0, 0), pipeline_mode=pl.Buffered(1)),
                  tab, tab, tab],
        out_specs=[row(w) for w, _ in widths] + t_specs,
        out_shape=[jax.ShapeDtypeStruct((B, S, w), dt) for w, dt in widths] + t_shapes,
        scratch_shapes=[pltpu.VMEM((D, N_PROJ_TILES * TILE_W), BF16),
                        pltpu.VMEM((N_VT_ROWS, D), BF16)],
        compiler_params=pltpu.CompilerParams(
            dimension_semantics=("arbitrary", "arbitrary"), vmem_limit_bytes=VMEM_LIMIT),
        name="in_projection",
    )(x, mod[:, None, :], g_pre[None], w_in, cos, sa, sb)


def _all_pair_specs(S, tq):
    qspec = pl.BlockSpec((1, tq, MIX_W), lambda b, i: (b, i, 0))
    kspec = pl.BlockSpec((1, S, MIX_W), lambda b, i: (b, 0, 0))
    vtspec = pl.BlockSpec((1, MIX_W, S), lambda b, i: (b, 0, 0))
    return qspec, kspec, vtspec


def _pair_lanes(p):
    return slice(p * LANES, (p + 1) * LANES)


def _tile_rows(ref, j, tk):
    return ref[0, pl.ds(pl.multiple_of(j * tk, tk), tk), :]


def _tile_rows_2d(ref, j, tk):
    return ref[pl.ds(pl.multiple_of(j * tk, tk), tk), :]


def _tile_lanes(ref, j, tk):
    return ref[0, :, pl.ds(pl.multiple_of(j * tk, tk), tk)]


def _stack_pair(q, is_a):
    return jnp.concatenate([jnp.where(is_a, q, 0), jnp.where(is_a, 0, q)], axis=0)


def _rank_rows(v, n):
    rows = v.shape[0]
    rank = jnp.zeros(v.shape, F32)
    for s in range(n):
        c = v[s:s + 1, :]
        lo = s // SUBLANES * SUBLANES
        hi = lo + SUBLANES
        grp = lax.broadcasted_iota(jnp.int32, (SUBLANES, v.shape[1]), 0) + lo
        mixed = jnp.logical_or(c > v[lo:hi], jnp.logical_and(c == v[lo:hi], grp > s))
        parts = [jnp.where(c > v[:lo], 1.0, 0.0)] if lo else []
        parts.append(jnp.where(mixed, 1.0, 0.0))
        if hi < rows:
            parts.append(jnp.where(c >= v[hi:], 1.0, 0.0))
        rank = rank + jnp.concatenate(parts, axis=0)
    return rank


def _pipelined(n, first, produce, consume, state, scr, quads=False):
    def put(slot, tiles):
        for a, tile in enumerate(tiles):
            scr[slot, a] = tile

    def get(slot):
        return [scr[slot, a] for a in range(scr.shape[1])]

    def step(t, slot, carry):
        small, st = carry
        small_new, tiles = produce(t)
        put(1 - slot, tiles)
        return small_new, consume((small, get(slot)), st)

    def pair(t, c):
        return step(t + 1, 1, step(t, 0, c))

    put(0, first[1])
    carry = (first[0], state)
    done = 0
    if quads:
        carry = lax.fori_loop(0, lax.shift_right_logical(n, 2),
                              lambda u, c: pair(4 * u + 2, pair(4 * u, c)), carry)
        done = lax.shift_left(lax.shift_right_logical(n, 2), 2)
    carry = lax.fori_loop(0, lax.shift_right_logical(n - done, 1),
                          lambda u, c: pair(done + 2 * u, c), carry)

    def odd_tail(c):
        small, st = step(n - 1, 0, c)
        return consume((small, get(1)), st)

    return lax.cond(jnp.bitwise_and(n, 1) == 1, odd_tail,
                    lambda c: consume((c[0], get(0)), c[1]), carry)


def _finish_pairs(accs, tq, z_ref, o_ref):
    heads = [a[:, h * tq:(h + 1) * tq] for a in accs for h in range(2)]
    out = jnp.concatenate(heads, axis=0).T
    o_ref[0] = (out * _silu(z_ref[0].astype(F32))).astype(BF16)


def _stick_kernel(q_ref, k_ref, vt_ref, z_ref, o_ref, tile_scr, *, tq):
    i = pl.program_id(1)
    tk = KEY_TILE
    n_diag = tq // tk
    is_a = _head_masks((tq, LANES))
    qs = (q_ref[0].astype(F32) * SCALE).astype(BF16)
    qcat = [_stack_pair(qs[:, _pair_lanes(p)], is_a) for p in range(N_PAIRS)]
    r_i = lax.broadcasted_iota(jnp.int32, (tk, tk), 0)
    c_i = lax.broadcasted_iota(jnp.int32, (tk, tk), 1)
    suffix = jnp.where(c_i >= r_i, 1.0, 0.0).astype(BF16)
    kloc = lax.broadcasted_iota(jnp.int32, (tk, 2 * tq), 0)
    qloc = jnp.bitwise_and(lax.broadcasted_iota(jnp.int32, (1, 2 * tq), 1), tq - 1)

    def log_weights(j, diag):
        ks = _tile_rows(k_ref, j, tk)
        zs = [_nt_dot(ks[:, _pair_lanes(p)], qcat[p]) for p in range(N_PAIRS)]
        sps = [jnp.maximum(z, 0.0) + jnp.log(1.0 + jnp.exp2(jnp.abs(z) * (-LOG2E))) for z in zs]
        if diag:
            past = kloc < qloc - (j * tk - i * tq)
            sps = [jnp.where(past, sp, 0.0) for sp in sps]
        css = [_dot(suffix, sp.astype(BF16)) for sp in sps]
        xs = [z - cs for z, cs in zip(zs, css)]
        if diag:
            xs = [jnp.where(past, x, NEG) for x in xs]
        return (j, [cs[0:1, :] for cs in css]), xs

    def accumulate(blk, st):
        (j, tots), xs = blk
        vts = _tile_lanes(vt_ref, j, tk)
        ws = [jnp.exp(xs[p] - st[p][1]).astype(BF16) for p in range(N_PAIRS)]
        return [(st[p][0] + _pv_pair(vts[_pair_lanes(p)], ws[p], tq), st[p][1] + tots[p])
                for p in range(N_PAIRS)]

    st = [(jnp.zeros((HEAD_DIM, 2 * tq), F32), jnp.zeros((1, 2 * tq), F32))] * N_PAIRS
    top = i * n_diag + n_diag - 1
    for d in range(n_diag - 1):
        st = accumulate(log_weights(top - d, True), st)
    st = _pipelined(i * n_diag, log_weights(i * n_diag, True),
                    lambda t: log_weights(i * n_diag - 1 - t, False), accumulate, st, tile_scr)
    _finish_pairs([acc for acc, _ in st], tq, z_ref, o_ref)


def _stick_breaking(q, k, vt, z, tq=512):
    B, S, _ = q.shape
    tq = min(tq, S)
    qspec, kspec, vtspec = _all_pair_specs(S, tq)
    return pl.pallas_call(
        functools.partial(_stick_kernel, tq=tq),
        grid=(B, S // tq),
        in_specs=[qspec, kspec, vtspec, qspec],
        out_specs=qspec,
        out_shape=jax.ShapeDtypeStruct((B, S, MIX_W), BF16),
        scratch_shapes=[pltpu.VMEM((2, N_PAIRS, KEY_TILE, 2 * tq), F32)],
        compiler_params=pltpu.CompilerParams(
            dimension_semantics=("parallel", "arbitrary"), vmem_limit_bytes=VMEM_LIMIT),
        name="stick_breaking",
    )(q, k, vt, z)


def _dil_kernel(q_ref, k_ref, v_ref, z_ref, o_ref, m_scr, l_scr, a_scr, *, G):
    g = pl.program_id(2)
    blk = LANES
    is_a = _head_masks((blk, LANES))
    qrow = jnp.bitwise_and(lax.broadcasted_iota(jnp.int32, (2 * blk, 2 * blk), 0), blk - 1)
    col = lax.broadcasted_iota(jnp.int32, (2 * blk, 2 * blk), 1)
    band = jnp.logical_and(col >= qrow, col <= qrow + blk)
    n_sub = G // blk
    unroll = 16

    for ci, (_, d) in enumerate(DIL_CONFIGS):
        nb = G // (blk * d)
        sh = int(np.log2(nb))

        def sub(n, ci=ci, d=d, nb=nb, sh=sh):
            r = lax.shift_right_logical(n, sh)
            ub = jnp.bitwise_and(n, nb - 1)
            loc = ub * (blk * d) + r
            glob = g * G + loc
            has_prev = glob >= blk * d
            pstart = jnp.maximum(glob - blk * d, r)
            qs = (q_ref[0, pl.ds(loc, blk, stride=d), :] * SCALE_LOG2).astype(BF16)
            kk = jnp.concatenate([k_ref[0, pl.ds(pstart, blk, stride=d), :],
                                  k_ref[0, pl.ds(glob, blk, stride=d), :]], axis=0).astype(BF16)
            vv = jnp.concatenate([v_ref[0, pl.ds(pstart, blk, stride=d), :],
                                  v_ref[0, pl.ds(glob, blk, stride=d), :]], axis=0).astype(BF16)
            ok = jnp.logical_and(band, jnp.logical_or(col >= blk, has_prev))
            s = jnp.where(ok, _nt_dot(_stack_pair(qs, is_a), kk), NEG)
            m = jnp.max(s, axis=-1, keepdims=True)
            p = jnp.exp2(s - m)
            l = jnp.sum(p, axis=-1, keepdims=True)
            acc = _dot(p.astype(BF16), vv)
            rows = pl.ds(loc, blk, stride=d)
            m_scr[ci, rows, :] = jnp.where(is_a, m[:blk], m[blk:])
            l_scr[ci, rows, :] = jnp.where(is_a, l[:blk], l[blk:])
            a_scr[ci, rows, :] = jnp.where(is_a, acc[:blk], acc[blk:])

        def trip(t, _, sub=sub):
            for u in range(unroll):
                sub(t * unroll + u)
            return 0

        lax.fori_loop(0, n_sub // unroll, trip, 0)

    chunk = 256

    def combine(c, _):
        rows = pl.ds(pl.multiple_of(c * chunk, chunk), chunk)
        m0, m1, m2 = m_scr[0, rows, :], m_scr[1, rows, :], m_scr[2, rows, :]
        mm = jnp.maximum(jnp.maximum(m0, m1), m2)
        w0, w1, w2 = jnp.exp2(m0 - mm), jnp.exp2(m1 - mm), jnp.exp2(m2 - mm)
        num = w0 * a_scr[0, rows, :] + w1 * a_scr[1, rows, :] + w2 * a_scr[2, rows, :]
        den = w0 * l_scr[0, rows, :] + w1 * l_scr[1, rows, :] + w2 * l_scr[2, rows, :]
        o_ref[0, rows, :] = (num / den * _silu(z_ref[0, rows, :].astype(F32))).astype(BF16)
        return 0

    lax.fori_loop(0, G // chunk, combine, 0)


def _dilated(q, k, v, z):
    B, S, _ = q.shape
    G = min(DIL_GROUP, S)
    gspec = pl.BlockSpec((1, G, LANES), lambda b, p, g: (b, g, p))
    kspec = pl.BlockSpec((1, S, LANES), lambda b, p, g: (b, 0, p))
    scr = pltpu.VMEM((len(DIL_CONFIGS), G, LANES), F32)
    return pl.pallas_call(
        functools.partial(_dil_kernel, G=G),
        grid=(B, 2, S // G),
        in_specs=[gspec, kspec, kspec, gspec],
        out_specs=gspec,
        out_shape=jax.ShapeDtypeStruct((B, S, MIX_W), BF16),
        scratch_shapes=[scr, scr, scr],
        compiler_params=pltpu.CompilerParams(
            dimension_semantics=("parallel", "parallel", "arbitrary"), vmem_limit_bytes=VMEM_LIMIT),
        name="dilated_window",
    )(q, k, v, z)


def _compress_kernel(k_ref, v_ref, p_ref, w1_ref, b1_ref, w2k_ref, w2vt_ref, cos_ref, sa_ref, sb_ref,
                     kc_ref, vc_ref):
    n = kc_ref.shape[1]
    dh = k_ref.shape[-1]
    for s, (t_ref, o_ref) in enumerate(((k_ref, kc_ref), (v_ref, vc_ref))):
        lo = jnp.zeros((n, CMP_HID), F32)
        hi = jnp.zeros((n, CMP_HID), F32)
        for l in range(CMP_STRIDE):
            t = t_ref[0, pl.ds(l, n, stride=CMP_STRIDE), :]
            lo = lo + _dot((t + p_ref[s, l:l + 1, :]).astype(BF16),
                           w1_ref[s, l * dh:(l + 1) * dh, :])
            lh = l + CMP_STRIDE
            hi = hi + _dot((t + p_ref[s, lh:lh + 1, :]).astype(BF16),
                           w1_ref[s, lh * dh:(lh + 1) * dh, :])
        pre = lo + pltpu.roll(hi, n - 1, 0) + b1_ref[s]
        hid = 0.5 * pre * (1.0 + jnp.tanh(np.sqrt(2.0 / np.pi).astype(np.float32)
                                          * (pre + 0.044715 * (pre * pre * pre))))
        hid = hid.astype(BF16)
        if s == 0:
            out = _dot(hid, w2k_ref[...])
            o_ref[0] = _rope_lane_tile(out, cos_ref[...], sa_ref[...], sb_ref[...]).astype(BF16)
        else:
            o_ref[0] = _nt_dot(w2vt_ref[...], hid).astype(BF16)


def _compress(kcr, vcr, cmp_pos, cmp_w1, cmp_b1, cmp_w2, tables_c):
    B, S, dh = kcr.shape
    n = S // CMP_STRIDE
    w1 = cmp_w1.astype(BF16)
    w2k = jnp.concatenate([cmp_w2[0], cmp_w2[0]], axis=-1).astype(BF16)
    w2vt = cmp_w2[1].T.astype(BF16)
    b1 = cmp_b1[:, None, :]
    cos, sa, sb = tables_c
    const = lambda a: pl.BlockSpec(a.shape, lambda b: (0,) * a.ndim)
    tspec = pl.BlockSpec((1, S, dh), lambda b: (b, 0, 0))
    return pl.pallas_call(
        _compress_kernel,
        grid=(B,),
        in_specs=[tspec, tspec, const(cmp_pos), const(w1), const(b1), const(w2k), const(w2vt),
                  const(cos), const(sa), const(sb)],
        out_specs=[pl.BlockSpec((1, n, LANES), lambda b: (b, 0, 0)),
                   pl.BlockSpec((1, dh, n), lambda b: (b, 0, 0))],
        out_shape=[jax.ShapeDtypeStruct((B, n, LANES), BF16),
                   jax.ShapeDtypeStruct((B, dh, n), BF16)],
        compiler_params=pltpu.CompilerParams(
            dimension_semantics=("parallel",), vmem_limit_bytes=VMEM_LIMIT),
        name="compress_tokens",
    )(kcr, vcr, cmp_pos, w1, b1, w2k, w2vt, cos, sa, sb)


def _nsa_kernel(q_ref, kc_ref, vct_ref, ks_ref, vst_ref, kw_ref, vwt_ref, gc_ref, z_ref,
                ovt_ref, ext_ref, o_ref, tile_scr, *, tq, tk, n_sel):
    i = pl.program_id(1)
    Q = N_HEADS * tq
    is_a = _head_masks((tq, LANES))
    qs = q_ref[0]
    q4 = jnp.concatenate([_stack_pair(qs[:, :LANES], is_a), _stack_pair(qs[:, LANES:], is_a)], axis=0)
    heads = lambda x: jnp.concatenate([x] * N_HEADS, axis=1)
    qpos1 = i * tq + lax.broadcasted_iota(jnp.int32, (1, tq), 1)
    qpos = heads(qpos1)

    kpos0 = lax.broadcasted_iota(jnp.int32, (tk, Q), 0)
    J = (i * tq) // tk

    n_c = kc_ref.shape[1]
    c_end = lax.broadcasted_iota(jnp.int32, (n_c, Q), 0) * CMP_STRIDE + (CMP_LEN - 1)
    c_valid = c_end <= qpos
    sc = jnp.where(c_valid, _nt_dot(kc_ref[0], q4), NEG)
    pc = jnp.exp2(sc - jnp.max(sc, axis=0, keepdims=True)) * jnp.where(c_valid, 1.0, 0.0)
    lc = jnp.sum(pc, axis=0, keepdims=True)
    pc = pc * jnp.where(lc > 0.0, 1.0 / lc, 0.0)
    o_cmp = _dot(vct_ref[0], pc.astype(BF16))

    pcs = pc[:, 0:tq] + pc[:, tq:2 * tq] + pc[:, 2 * tq:3 * tq] + pc[:, 3 * tq:]
    hi, lo = _split_bf16(pcs)
    imp = _dot(ovt_ref[...], hi) + _dot(ovt_ref[...], lo)
    nsp = imp.shape[0]
    blk = lax.broadcasted_iota(jnp.int32, (nsp, tq), 0)
    own = lax.shift_right_logical(qpos1, int(np.log2(SEL_LEN)))
    s_valid = blk <= own
    forced = jnp.logical_or(blk == 0, blk >= own - 1)
    imp = jnp.where(s_valid, jnp.where(forced, BIG, imp), NEG)
    rank = _rank_rows(imp, n_sel)
    chosen = jnp.logical_and(rank < float(min(SEL_TOPN, n_sel)), s_valid)
    sel = jnp.where(chosen, 1.0, 0.0).astype(BF16)

    n_win = -(-WIN_LEN // tk) + 1
    s_w, vt_w = [], []
    for a in range(n_win):
        j_a = J - (n_win - 1) + a
        qrel = qpos - j_a * tk
        ok = jnp.broadcast_to(j_a >= 0, (tk, Q))
        if a == 0:
            ok = jnp.logical_and(ok, kpos0 > qrel - WIN_LEN)
        if a == n_win - 1:
            ok = kpos0 <= qrel
        j_c = jnp.maximum(j_a, 0)
        s_w.append(jnp.where(ok, _nt_dot(_tile_rows(kw_ref, j_c, tk), q4), NEG))
        vt_w.append(_tile_lanes(vwt_ref, j_c, tk))
    m_w = functools.reduce(jnp.maximum, [jnp.max(s, axis=0, keepdims=True) for s in s_w])
    p_w = [jnp.exp2(s - m_w) for s in s_w]
    l_w = sum(jnp.sum(p, axis=0, keepdims=True) for p in p_w)
    o_win = sum(_dot(vt, p.astype(BF16)) for vt, p in zip(vt_w, p_w)) / l_w

    def sel_scores(j, diag):
        bias = (_dot(_tile_rows_2d(ext_ref, j, tk), sel) - 1.0) * (-NEG)
        s = _nt_dot(_tile_rows(ks_ref, j, tk), q4) + heads(bias)
        if diag:
            s = jnp.where(j * tk + kpos0 <= qpos, s, NEG)
        return (j, jnp.max(s, axis=0, keepdims=True)), [s]

    def consume(blk_s, st):
        (j, s_max), (s,) = blk_s
        m, l, acc = st
        m_new = jnp.maximum(m, s_max)
        alpha = jnp.exp2(m - m_new)
        p = jnp.exp2(s - m_new)
        l = alpha * l + jnp.sum(p, axis=0, keepdims=True)
        acc = alpha * acc + _dot(_tile_lanes(vst_ref, j, tk), p.astype(BF16))
        return m_new, l, acc

    init = (jnp.full((1, Q), NEG, F32), jnp.zeros((1, Q), F32), jnp.zeros((HEAD_DIM, Q), F32))

    _, l_s, a_s = _pipelined(J, sel_scores(J, True), lambda t: sel_scores(t, False), consume, init,
                             tile_scr, quads=True)
    o_sel = a_s / l_s

    gates = jax.nn.sigmoid(gc_ref[0]).T
    outs = []
    for h in range(N_HEADS):
        cols = slice(h * tq, (h + 1) * tq)
        outs.append(gates[3 * h:3 * h + 1, :] * o_cmp[:, cols]
                    + gates[3 * h + 1:3 * h + 2, :] * o_sel[:, cols]
                    + gates[3 * h + 2:3 * h + 3, :] * o_win[:, cols])
    out = jnp.concatenate(outs, axis=0).T
    o_ref[0] = (out * _silu(z_ref[0].astype(F32))).astype(BF16)


def _nsa(q, kc2, vct, ks2, vst, kw2, vwt, gc, z, tq=256, tk=256):
    B, S, _ = q.shape
    n_c = kc2.shape[1]
    n_sel = S // SEL_LEN
    nsp = -(-n_sel // SUBLANES) * SUBLANES
    n_cmp = (S - CMP_LEN) // CMP_STRIDE + 1
    c_start = np.arange(n_c) * CMP_STRIDE
    s_start = np.arange(nsp) * SEL_LEN
    overlap_t = np.clip(np.minimum(c_start[None, :] + CMP_LEN, s_start[:, None] + SEL_LEN)
                        - np.maximum(c_start[None, :], s_start[:, None]), 0, None) / CMP_LEN
    overlap_t[:, n_cmp:] = 0.0
    overlap_t[n_sel:, :] = 0.0
    expand_t = (np.arange(S)[:, None] // SEL_LEN == np.arange(nsp)[None, :]).astype(np.float32)
    row = lambda w: pl.BlockSpec((1, tq, w), lambda b, i: (b, i, 0))
    full = lambda a: pl.BlockSpec((1,) + a.shape[1:], lambda b, i: (b, 0, 0))
    const = lambda a: pl.BlockSpec(a.shape, lambda b, i: (0, 0))
    ovt = jnp.asarray(overlap_t, BF16)
    ext = jnp.asarray(expand_t, BF16)
    return pl.pallas_call(
        functools.partial(_nsa_kernel, tq=tq, tk=tk, n_sel=n_sel),
        grid=(B, S // tq),
        in_specs=[row(MIX_W), full(kc2), full(vct), full(ks2), full(vst), full(kw2), full(vwt),
                  row(LANES), row(MIX_W), const(ovt), const(ext)],
        out_specs=row(MIX_W),
        out_shape=jax.ShapeDtypeStruct((B, S, MIX_W), BF16),
        scratch_shapes=[pltpu.VMEM((2, 1, tk, N_HEADS * tq), F32)],
        compiler_params=pltpu.CompilerParams(
            dimension_semantics=("parallel", "parallel"), vmem_limit_bytes=VMEM_LIMIT),
        name="native_sparse",
    )(q, kc2, vct, ks2, vst, kw2, vwt, gc, z, ovt, ext)


def _moba_kernel(q_ref, k_ref, vt_ref, z_ref, o_ref, km_scr, bias_scr, tile_scr, *, n_blk, tq):
    i = pl.program_id(1)
    tk = MOBA_BLOCK
    n_diag = tq // tk
    nb_pad = km_scr.shape[0]
    is_a = _head_masks((tq, LANES))

    @pl.when(i == 0)
    def _():
        km_scr[...] = jnp.zeros_like(km_scr)
        for b in range(n_blk):
            kb = k_ref[0, b * tk:(b + 1) * tk, :].astype(F32)
            km_scr[b:b + 1, :] = jnp.mean(kb, axis=0, keepdims=True)

    q = q_ref[0]
    blk = lax.broadcasted_iota(jnp.int32, (nb_pad, 2 * tq), 0)
    qloc = jnp.bitwise_and(lax.broadcasted_iota(jnp.int32, (1, 2 * tq), 1), tq - 1)
    own = i * n_diag + lax.shift_right_logical(qloc, int(np.log2(tk)))
    n_top = min(MOBA_TOPK, max(n_blk - 1, 1))
    qcat = []
    for p in range(N_PAIRS):
        qcat.append(_stack_pair(q[:, _pair_lanes(p)], is_a))
        km_hi, km_lo = _split_bf16(km_scr[:, _pair_lanes(p)])
        gsc = _nt_dot(km_hi, qcat[p]) + _nt_dot(km_lo, qcat[p])
        gsc = jnp.where(blk < own, gsc, NEG)
        rank = _rank_rows(gsc, n_blk)
        attends = jnp.logical_or(jnp.logical_and(rank < float(n_top), blk < own), blk == own)
        bias_scr[p] = jnp.where(attends, 0.0, NEG)

    kloc = lax.broadcasted_iota(jnp.int32, (tk, 2 * tq), 0)

    def scores(j, diag):
        ks = _tile_rows(k_ref, j, tk)
        ss = [_nt_dot(ks[:, _pair_lanes(p)], qcat[p]) for p in range(N_PAIRS)]
        if diag:
            causal = kloc <= qloc - (j * tk - i * tq)
            ss = [jnp.where(causal, s, NEG) for s in ss]
        return (j, [jnp.max(s, axis=0, keepdims=True) for s in ss]), ss

    def consume(blk_s, st):
        (j, maxes), ss = blk_s
        vts = _tile_lanes(vt_ref, j, tk)
        pr = range(N_PAIRS)
        bias = [bias_scr[p, pl.ds(j, 1), :] for p in pr]
        m_new = [jnp.maximum(st[p][0], maxes[p] + bias[p]) for p in pr]
        ps = [jnp.exp2(ss[p] - (m_new[p] - bias[p])) for p in pr]
        pvs = [_pv_pair(vts[_pair_lanes(p)], ps[p].astype(BF16), tq) for p in pr]
        out = []
        for p in pr:
            m, l, acc = st[p]
            alpha = jnp.exp2(m - m_new[p])
            out.append((m_new[p], alpha * l + jnp.sum(ps[p], axis=0, keepdims=True),
                        alpha * acc + pvs[p]))
        return out

    st = [(jnp.full((1, 2 * tq), NEG, F32), jnp.zeros((1, 2 * tq), F32),
           jnp.zeros((HEAD_DIM, 2 * tq), F32))] * N_PAIRS
    for d in range(n_diag - 1, 0, -1):
        st = consume(scores(i * n_diag + d, True), st)
    st = _pipelined(i * n_diag, scores(i * n_diag, True), lambda t: scores(t, False), consume, st,
                    tile_scr, quads=True)
    _finish_pairs([acc / l for _, l, acc in st], tq, z_ref, o_ref)


def _moba(q, k, vt, z, tq=256):
    B, S, _ = q.shape
    tq = min(tq, S)
    n_blk = S // MOBA_BLOCK
    nb_pad = -(-n_blk // SUBLANES) * SUBLANES
    qspec, kspec, vtspec = _all_pair_specs(S, tq)
    return pl.pallas_call(
        functools.partial(_moba_kernel, n_blk=n_blk, tq=tq),
        grid=(B, S // tq),
        in_specs=[qspec, kspec, vtspec, qspec],
        out_specs=qspec,
        out_shape=jax.ShapeDtypeStruct((B, S, MIX_W), BF16),
        scratch_shapes=[pltpu.VMEM((nb_pad, MIX_W), F32),
                        pltpu.VMEM((N_PAIRS, nb_pad, 2 * tq), F32),
                        pltpu.VMEM((2, N_PAIRS, MOBA_BLOCK, 2 * tq), F32)],
        compiler_params=pltpu.CompilerParams(
            dimension_semantics=("parallel", "arbitrary"), vmem_limit_bytes=VMEM_LIMIT),
        name="moba",
    )(q, k, vt, z)


def _out_kernel(ma_ref, mb_ref, mc_ref, md_ref, w_ref, x_ref, mod_ref, g_ref, o_ref):
    D = x_ref.shape[-1]
    acc = _dot(ma_ref[0], w_ref[0:MIX_W, :])
    acc = acc + _dot(mb_ref[0], w_ref[MIX_W:2 * MIX_W, :])
    acc = acc + _dot(mc_ref[0], w_ref[2 * MIX_W:3 * MIX_W, :])
    acc = acc + _dot(md_ref[0], w_ref[3 * MIX_W:, :])
    ms = jnp.mean(acc * acc, axis=-1, keepdims=True)
    y = acc * lax.rsqrt(ms + EPS) * g_ref[...]
    o_ref[0] = x_ref[0] + mod_ref[0, :, 2 * D:] * y


def _out_projection(mixed, w_out, x, mod, g_post, tm):
    B, S, D = x.shape
    row = lambda w: pl.BlockSpec((1, tm, w), lambda b, i: (b, i, 0))
    return pl.pallas_call(
        _out_kernel,
        grid=(B, S // tm),
        in_specs=[row(MIX_W)] * 4 + [pl.BlockSpec(w_out.shape, lambda b, i: (0, 0)), row(D),
                                     pl.BlockSpec((1, 1, 3 * D), lambda b, i: (b, 0, 0)),
                                     pl.BlockSpec((1, D), lambda b, i: (0, 0))],
        out_specs=row(D),
        out_shape=jax.ShapeDtypeStruct((B, S, D), F32),
        compiler_params=pltpu.CompilerParams(
            dimension_semantics=("parallel", "parallel"), vmem_limit_bytes=VMEM_LIMIT),
        name="out_projection",
    )(*mixed, w_out.astype(BF16), x, mod[:, None, :], g_post[None])


def _layer(x, c, norm_pre, norm_post, w_mod, b_mod, w_in, w_out, cmp_pos, cmp_w1, cmp_b1, cmp_w2,
           tables, tables_c, tm):
    mod = _modulation(c, w_mod, b_mod)
    (qa, ka, za, qb, kb, vb, zb, qc, kcr, vcr, gc, ks2, kw2, zc,
     qd, kd, zd, vat, vdt, vst, vwt) = _projection(x, mod, norm_pre, w_in, tables, tm)
    oa = _stick_breaking(qa, ka, vat, za)
    ob = _dilated(qb, kb, vb, zb)
    kc2, vct = _compress(kcr, vcr, cmp_pos, cmp_w1, cmp_b1, cmp_w2, tables_c)
    oc = _nsa(qc, kc2, vct, ks2, vst, kw2, vwt, gc, zc)
    od = _moba(qd, kd, vdt, zd)
    return _out_projection((oa, ob, oc, od), w_out, x, mod, norm_post, min(2 * tm, x.shape[1]))


def kernel(x, c, norm_pre, norm_post, w_mod, b_mod, w_in, w_out, cmp_pos, cmp_w1, cmp_b1, cmp_w2):
    S = x.shape[1]
    tables = _rope_lane_tables(np.arange(S))
    tables_c = _rope_lane_tables(np.arange(S // CMP_STRIDE) * CMP_STRIDE + CMP_LEN - 1)
    tm = min(512, S)
    for l in range(norm_pre.shape[0]):
        x = _layer(x, c, norm_pre[l], norm_post[l], w_mod[l], b_mod[l], w_in[l], w_out[l],
                   cmp_pos[l], cmp_w1[l], cmp_b1[l], cmp_w2[l], tables, tables_c, tm)
    return x
```

```python
import functools

import numpy as np
import jax
import jax.numpy as jnp
from jax import lax
from jax.experimental import pallas as pl
from jax.experimental.pallas import tpu as pltpu

F32 = jnp.float32
BF16 = jnp.bfloat16

N_HEADS = 4
HEAD_DIM = 64
MIX_W = N_HEADS * HEAD_DIM
ROPE_THETA = 500000.0
ROPE_DIM = HEAD_DIM // 4
ROPE_HALF = ROPE_DIM // 2
EPS = 1e-6
NEG = -1e30
BIG = 1e9
SCALE = HEAD_DIM ** -0.5
LOG2E = 1.4426950408889634
SCALE_LOG2 = SCALE * LOG2E
DIL_CONFIGS = ((128, 1), (512, 4), (2048, 16))
CMP_LEN = 32
CMP_STRIDE = 16
CMP_HID = 256
SEL_LEN = 64
SEL_TOPN = 16
WIN_LEN = 512
MOBA_BLOCK = 256
MOBA_TOPK = 3

N_PAIRS = N_HEADS // 2
LANES = 128
SUBLANES = 8
TILE_W = 2 * LANES
KEY_TILE = 256
DIL_GROUP = 2048
VMEM_LIMIT = 56 * 1024 * 1024


def _nt_dot(a, b):
    return lax.dot_general(a, b, (((1,), (1,)), ((), ())), preferred_element_type=F32)


def _dot(a, b):
    return jnp.dot(a, b, preferred_element_type=F32)


def _split_bf16(x):
    hi = x.astype(BF16)
    lo = (x - hi.astype(F32)).astype(BF16)
    return hi, lo


def _silu(x):
    return x * jax.nn.sigmoid(x)


def _head_masks(shape):
    lane = lax.broadcasted_iota(jnp.int32, shape, 1)
    return lane < HEAD_DIM


def _pv_pair(vt, p, tq):
    return jnp.concatenate([_dot(vt[:HEAD_DIM], p[:, :tq]), _dot(vt[HEAD_DIM:], p[:, tq:])], axis=1)


def _mod_kernel(c_ref, w_ref, b_ref, o_ref):
    c = c_ref[...]
    o_ref[...] = jnp.dot(_silu(c), w_ref[...], preferred_element_type=F32,
                         precision=lax.Precision.HIGHEST) + b_ref[...]


def _modulation(c, w_mod, b_mod):
    B, D = c.shape
    N = w_mod.shape[1]
    rows = SUBLANES
    cp = jnp.zeros((rows, D), F32).at[:B].set(c)
    tn = N // 4
    out = pl.pallas_call(
        _mod_kernel,
        grid=(N // tn,),
        in_specs=[pl.BlockSpec((rows, D), lambda j: (0, 0)),
                  pl.BlockSpec((D, tn), lambda j: (0, j)),
                  pl.BlockSpec((1, tn), lambda j: (0, j))],
        out_specs=pl.BlockSpec((rows, tn), lambda j: (0, j)),
        out_shape=jax.ShapeDtypeStruct((rows, N), F32),
        name="modulation",
    )(cp, w_mod, b_mod[None])
    return out[:B]


def _rope_lane_tile(y, cos, sa, sb):
    return y * cos + pltpu.roll(y, LANES - ROPE_HALF, 1) * sa + pltpu.roll(y, ROPE_HALF, 1) * sb


def _pack_weights(win_ref, w_ref, wt_ref):
    D = win_ref.shape[0]
    o = IN_OFFS
    chunk = 256
    for r0 in range(0, D, chunk):
        rs = slice(r0, r0 + chunk)

        def cp(dst, a, b):
            w_ref[rs, dst:dst + (b - a)] = win_ref[rs, a:b].astype(BF16)

        cp(0, o[0], o[11])
        small = PROJ_TILES["c_small"] * TILE_W
        cp(small + LANES, o[15], o[16])
        pad = small + LANES + 3 * N_HEADS
        w_ref[rs, pad:small + TILE_W] = jnp.zeros((chunk, small + TILE_W - pad), BF16)
        keys = PROJ_TILES["c_keys"] * TILE_W
        for rep in range(2):
            cp(keys + rep * HEAD_DIM, o[11], o[12])
            cp(keys + LANES + rep * HEAD_DIM, o[13], o[14])
        cp(PROJ_TILES["zc"] * TILE_W, o[16], o[21])

    def put_t(dst, a, lo=0, n=LANES):
        wt_ref[dst:dst + n, :] = win_ref[:, a:a + LANES].astype(F32).T[lo:lo + n].astype(BF16)

    for h in range(2):
        put_t(h * LANES, o[2] + h * LANES)
        put_t(MIX_W + h * LANES, o[19] + h * LANES)
    put_t(2 * MIX_W, o[11], HEAD_DIM, HEAD_DIM)
    put_t(2 * MIX_W + HEAD_DIM, o[13], HEAD_DIM, HEAD_DIM)


def _proj_kernel(x_ref, mod_ref, g_ref, win_ref, cos_ref, sa_ref, sb_ref,
                 qa_ref, ka_ref, za_ref,
                 qb_ref, kb_ref, vb_ref, zb_ref,
                 qc_ref, kcr_ref, vcr_ref, gc_ref, ks_ref, kw_ref, zc_ref,
                 qd_ref, kd_ref, zd_ref, vat_ref, vdt_ref, vst_ref, vwt_ref, w_ref, wt_ref):
    @pl.when(jnp.logical_and(pl.program_id(0) == 0, pl.program_id(1) == 0))
    def _():
        _pack_weights(win_ref, w_ref, wt_ref)

    D = x_ref.shape[-1]
    x = x_ref[0]
    ms = jnp.mean(x * x, axis=-1, keepdims=True)
    y = x * lax.rsqrt(ms + EPS) * g_ref[...]
    shift = mod_ref[0, :, 0:D]
    scale = mod_ref[0, :, D:2 * D]
    h = (y * (1.0 + scale) + shift).astype(BF16)
    cos, sa, sb = cos_ref[...], sa_ref[...], sb_ref[...]

    def tile(t):
        return _dot(h, w_ref[:, t * TILE_W:(t + 1) * TILE_W])

    def rope_lo(y):
        return _rope_lane_tile(y[:, :LANES], cos, sa, sb)

    def rope_all(y):
        return jnp.concatenate([rope_lo(y), _rope_lane_tile(y[:, LANES:], cos, sa, sb)], axis=1)

    T = PROJ_TILES
    qa_ref[0] = tile(T["qa"]).astype(BF16)
    ka_ref[0] = tile(T["ka"]).astype(BF16)
    za_ref[0] = tile(T["za"]).astype(BF16)
    qb_ref[0] = rope_all(tile(T["qb"]))
    kb_ref[0] = rope_all(tile(T["kb"]))
    vb_ref[0] = tile(T["vb"])
    zb_ref[0] = tile(T["zb"]).astype(BF16)
    qc_ref[0] = (rope_all(tile(T["qc"])) * SCALE_LOG2).astype(BF16)
    small = tile(T["c_small"])
    kcr_ref[0] = small[:, 0:HEAD_DIM]
    vcr_ref[0] = small[:, HEAD_DIM:LANES]
    gc_ref[0] = small[:, LANES:]
    keys = rope_all(tile(T["c_keys"])).astype(BF16)
    ks_ref[0] = keys[:, :LANES]
    kw_ref[0] = keys[:, LANES:]
    zc_ref[0] = tile(T["zc"]).astype(BF16)
    qd_ref[0] = (rope_all(tile(T["qd"])) * SCALE_LOG2).astype(BF16)
    kd_ref[0] = rope_all(tile(T["kd"])).astype(BF16)
    zd_ref[0] = tile(T["zd"]).astype(BF16)
    vt = _nt_dot(wt_ref[...], h).astype(BF16)
    vat_ref[0] = vt[0:MIX_W]
    vdt_ref[0] = vt[MIX_W:2 * MIX_W]
    vst_ref[0] = vt[2 * MIX_W:2 * MIX_W + HEAD_DIM]
    vwt_ref[0] = vt[2 * MIX_W + HEAD_DIM:]


PROJ_TILES = dict(qa=0, ka=1, za=3, qb=4, kb=5, vb=6, zb=7, qc=8, c_small=9, c_keys=10,
                  zc=11, qd=12, kd=13, zd=15)
N_PROJ_TILES = 16
N_VT_ROWS = 2 * MIX_W + 2 * HEAD_DIM
IN_OFFS = [int(v) for v in np.cumsum([0] + [MIX_W] * 9 + [HEAD_DIM] * 6 + [3 * N_HEADS] + [MIX_W] * 5)]


def _rope_lane_tables(pos):
    f32 = np.float32
    inv_freq = (1.0 / (ROPE_THETA ** (np.arange(0, ROPE_DIM, 2, dtype=f32) / ROPE_DIM))).astype(f32)
    ang = np.asarray(pos, f32)[:, None] * inv_freq[None, :]
    cos, sin = np.cos(ang).astype(f32), np.sin(ang).astype(f32)
    n = cos.shape[0]
    rest = HEAD_DIM - ROPE_DIM
    cos_h = np.concatenate([cos, cos, np.ones((n, rest), f32)], axis=1)
    sa_h = np.concatenate([-sin, np.zeros((n, HEAD_DIM - ROPE_HALF), f32)], axis=1)
    sb_h = np.concatenate([np.zeros((n, ROPE_HALF), f32), sin, np.zeros((n, rest), f32)], axis=1)
    two = lambda t: jnp.asarray(np.concatenate([t, t], axis=1))
    return two(cos_h), two(sa_h), two(sb_h)


def _projection(x, mod, g_pre, w_in, tables, tm):
    B, S, D = x.shape
    cos, sa, sb = tables
    layer = w_in[1]
    w_in = w_in[0]
    row = lambda w: pl.BlockSpec((1, tm, w), lambda b, i: (b, i, 0))
    tab = pl.BlockSpec((tm, LANES), lambda b, i: (i, 0))
    widths = [(MIX_W, BF16)] * 3 + [(MIX_W, F32)] * 3 + [(MIX_W, BF16)] + \
             [(MIX_W, BF16), (HEAD_DIM, F32), (HEAD_DIM, F32), (LANES, F32),
              (LANES, BF16), (LANES, BF16), (MIX_W, BF16)] + \
             [(MIX_W, BF16)] * 3
    t_rows = (MIX_W, MIX_W, HEAD_DIM, HEAD_DIM)
    t_specs = [pl.BlockSpec((1, r, tm), lambda b, i: (b, 0, i)) for r in t_rows]
    t_shapes = [jax.ShapeDtypeStruct((B, r, S), BF16) for r in t_rows]
    return pl.pallas_call(
        _proj_kernel,
        grid=(B, S // tm),
        in_specs=[row(D),
                  pl.BlockSpec((1, 1, 3 * D), lambda b, i: (b, 0, 0)),
                  pl.BlockSpec((1, D), lambda b, i: (0, 0)),
                  pl.BlockSpec((None,) + w_in.shape[1:], lambda b, i: (layer, 0, 0),
                               pipeline_mode=pl.Buffered(1)),
                  tab, tab, tab],
        out_specs=[row(w) for w, _ in widths] + t_specs,
        out_shape=[jax.ShapeDtypeStruct((B, S, w), dt) for w, dt in widths] + t_shapes,
        scratch_shapes=[pltpu.VMEM((D, N_PROJ_TILES * TILE_W), BF16),
                        pltpu.VMEM((N_VT_ROWS, D), BF16)],
        compiler_params=pltpu.CompilerParams(
            dimension_semantics=("arbitrary", "arbitrary"), vmem_limit_bytes=VMEM_LIMIT),
        name="in_projection",
    )(x, mod[:, None, :], g_pre[None], w_in, cos, sa, sb)


def _all_pair_specs(S, tq):
    qspec = pl.BlockSpec((1, tq, MIX_W), lambda b, i: (b, i, 0))
    kspec = pl.BlockSpec((1, S, MIX_W), lambda b, i: (b, 0, 0))
    vtspec = pl.BlockSpec((1, MIX_W, S), lambda b, i: (b, 0, 0))
    return qspec, kspec, vtspec


def _pair_lanes(p):
    return slice(p * LANES, (p + 1) * LANES)


def _tile_rows(ref, j, tk):
    return ref[0, pl.ds(pl.multiple_of(j * tk, tk), tk), :]


def _tile_rows_2d(ref, j, tk):
    return ref[pl.ds(pl.multiple_of(j * tk, tk), tk), :]


def _tile_lanes(ref, j, tk):
    return ref[0, :, pl.ds(pl.multiple_of(j * tk, tk), tk)]


def _stack_pair(q, is_a):
    return jnp.concatenate([jnp.where(is_a, q, 0), jnp.where(is_a, 0, q)], axis=0)


def _rank_rows(v, n):
    rows = v.shape[0]
    rank = jnp.zeros(v.shape, F32)
    for s in range(n):
        c = v[s:s + 1, :]
        lo = s // SUBLANES * SUBLANES
        hi = lo + SUBLANES
        grp = lax.broadcasted_iota(jnp.int32, (SUBLANES, v.shape[1]), 0) + lo
        mixed = jnp.logical_or(c > v[lo:hi], jnp.logical_and(c == v[lo:hi], grp > s))
        parts = [jnp.where(c > v[:lo], 1.0, 0.0)] if lo else []
        parts.append(jnp.where(mixed, 1.0, 0.0))
        if hi < rows:
            parts.append(jnp.where(c >= v[hi:], 1.0, 0.0))
        rank = rank + jnp.concatenate(parts, axis=0)
    return rank


def _pipelined(n, first, produce, consume, state, scr, quads=False):
    def put(slot, tiles):
        for a, tile in enumerate(tiles):
            scr[slot, a] = tile

    def get(slot):
        return [scr[slot, a] for a in range(scr.shape[1])]

    def step(t, slot, carry):
        small, st = carry
        small_new, tiles = produce(t)
        put(1 - slot, tiles)
        return small_new, consume((small, get(slot)), st)

    def pair(t, c):
        return step(t + 1, 1, step(t, 0, c))

    put(0, first[1])
    carry = (first[0], state)
    done = 0
    if quads:
        carry = lax.fori_loop(0, lax.shift_right_logical(n, 2),
                              lambda u, c: pair(4 * u + 2, pair(4 * u, c)), carry)
        done = lax.shift_left(lax.shift_right_logical(n, 2), 2)
    carry = lax.fori_loop(0, lax.shift_right_logical(n - done, 1),
                          lambda u, c: pair(done + 2 * u, c), carry)

    def odd_tail(c):
        small, st = step(n - 1, 0, c)
        return consume((small, get(1)), st)

    return lax.cond(jnp.bitwise_and(n, 1) == 1, odd_tail,
                    lambda c: consume((c[0], get(0)), c[1]), carry)


def _finish_pairs(accs, tq, z_ref, o_ref):
    heads = [a[:, h * tq:(h + 1) * tq] for a in accs for h in range(2)]
    out = jnp.concatenate(heads, axis=0).T
    o_ref[0] = (out * _silu(z_ref[0].astype(F32))).astype(BF16)


def _stick_kernel(q_ref, k_ref, vt_ref, z_ref, o_ref, tile_scr, *, tq):
    i = pl.program_id(1)
    tk = KEY_TILE
    n_diag = tq // tk
    is_a = _head_masks((tq, LANES))
    qs = (q_ref[0].astype(F32) * SCALE).astype(BF16)
    qcat = [_stack_pair(qs[:, _pair_lanes(p)], is_a) for p in range(N_PAIRS)]
    r_i = lax.broadcasted_iota(jnp.int32, (tk, tk), 0)
    c_i = lax.broadcasted_iota(jnp.int32, (tk, tk), 1)
    suffix = jnp.where(c_i >= r_i, 1.0, 0.0).astype(BF16)
    kloc = lax.broadcasted_iota(jnp.int32, (tk, 2 * tq), 0)
    qloc = jnp.bitwise_and(lax.broadcasted_iota(jnp.int32, (1, 2 * tq), 1), tq - 1)

    def log_weights(j, diag):
        ks = _tile_rows(k_ref, j, tk)
        zs = [_nt_dot(ks[:, _pair_lanes(p)], qcat[p]) for p in range(N_PAIRS)]
        sps = [jnp.maximum(z, 0.0) + jnp.log(1.0 + jnp.exp2(jnp.abs(z) * (-LOG2E))) for z in zs]
        if diag:
            past = kloc < qloc - (j * tk - i * tq)
            sps = [jnp.where(past, sp, 0.0) for sp in sps]
        css = [_dot(suffix, sp.astype(BF16)) for sp in sps]
        xs = [z - cs for z, cs in zip(zs, css)]
        if diag:
            xs = [jnp.where(past, x, NEG) for x in xs]
        return (j, [cs[0:1, :] for cs in css]), xs

    def accumulate(blk, st):
        (j, tots), xs = blk
        vts = _tile_lanes(vt_ref, j, tk)
        ws = [jnp.exp(xs[p] - st[p][1]).astype(BF16) for p in range(N_PAIRS)]
        return [(st[p][0] + _pv_pair(vts[_pair_lanes(p)], ws[p], tq), st[p][1] + tots[p])
                for p in range(N_PAIRS)]

    st = [(jnp.zeros((HEAD_DIM, 2 * tq), F32), jnp.zeros((1, 2 * tq), F32))] * N_PAIRS
    top = i * n_diag + n_diag - 1
    for d in range(n_diag - 1):
        st = accumulate(log_weights(top - d, True), st)
    st = _pipelined(i * n_diag, log_weights(i * n_diag, True),
                    lambda t: log_weights(i * n_diag - 1 - t, False), accumulate, st, tile_scr)
    _finish_pairs([acc for acc, _ in st], tq, z_ref, o_ref)


def _stick_breaking(q, k, vt, z, tq=512):
    B, S, _ = q.shape
    tq = min(tq, S)
    qspec, kspec, vtspec = _all_pair_specs(S, tq)
    return pl.pallas_call(
        functools.partial(_stick_kernel, tq=tq),
        grid=(B, S // tq),
        in_specs=[qspec, kspec, vtspec, qspec],
        out_specs=qspec,
        out_shape=jax.ShapeDtypeStruct((B, S, MIX_W), BF16),
        scratch_shapes=[pltpu.VMEM((2, N_PAIRS, KEY_TILE, 2 * tq), F32)],
        compiler_params=pltpu.CompilerParams(
            dimension_semantics=("parallel", "arbitrary"), vmem_limit_bytes=VMEM_LIMIT),
        name="stick_breaking",
    )(q, k, vt, z)


def _dil_kernel(q_ref, k_ref, v_ref, z_ref, o_ref, m_scr, l_scr, a_scr, *, G):
    g = pl.program_id(2)
    blk = LANES
    is_a = _head_masks((blk, LANES))
    qrow = jnp.bitwise_and(lax.broadcasted_iota(jnp.int32, (2 * blk, 2 * blk), 0), blk - 1)
    col = lax.broadcasted_iota(jnp.int32, (2 * blk, 2 * blk), 1)
    band = jnp.logical_and(col >= qrow, col <= qrow + blk)
    n_sub = G // blk
    unroll = 16

    for ci, (_, d) in enumerate(DIL_CONFIGS):
        nb = G // (blk * d)
        sh = int(np.log2(nb))

        def sub(n, ci=ci, d=d, nb=nb, sh=sh):
            r = lax.shift_right_logical(n, sh)
            ub = jnp.bitwise_and(n, nb - 1)
            loc = ub * (blk * d) + r
            glob = g * G + loc
            has_prev = glob >= blk * d
            pstart = jnp.maximum(glob - blk * d, r)
            qs = (q_ref[0, pl.ds(loc, blk, stride=d), :] * SCALE_LOG2).astype(BF16)
            kk = jnp.concatenate([k_ref[0, pl.ds(pstart, blk, stride=d), :],
                                  k_ref[0, pl.ds(glob, blk, stride=d), :]], axis=0).astype(BF16)
            vv = jnp.concatenate([v_ref[0, pl.ds(pstart, blk, stride=d), :],
                                  v_ref[0, pl.ds(glob, blk, stride=d), :]], axis=0).astype(BF16)
            ok = jnp.logical_and(band, jnp.logical_or(col >= blk, has_prev))
            s = jnp.where(ok, _nt_dot(_stack_pair(qs, is_a), kk), NEG)
            m = jnp.max(s, axis=-1, keepdims=True)
            p = jnp.exp2(s - m)
            l = jnp.sum(p, axis=-1, keepdims=True)
            acc = _dot(p.astype(BF16), vv)
            rows = pl.ds(loc, blk, stride=d)
            m_scr[ci, rows, :] = jnp.where(is_a, m[:blk], m[blk:])
            l_scr[ci, rows, :] = jnp.where(is_a, l[:blk], l[blk:])
            a_scr[ci, rows, :] = jnp.where(is_a, acc[:blk], acc[blk:])

        def trip(t, _, sub=sub):
            for u in range(unroll):
                sub(t * unroll + u)
            return 0

        lax.fori_loop(0, n_sub // unroll, trip, 0)

    chunk = 256

    def combine(c, _):
        rows = pl.ds(pl.multiple_of(c * chunk, chunk), chunk)
        m0, m1, m2 = m_scr[0, rows, :], m_scr[1, rows, :], m_scr[2, rows, :]
        mm = jnp.maximum(jnp.maximum(m0, m1), m2)
        w0, w1, w2 = jnp.exp2(m0 - mm), jnp.exp2(m1 - mm), jnp.exp2(m2 - mm)
        num = w0 * a_scr[0, rows, :] + w1 * a_scr[1, rows, :] + w2 * a_scr[2, rows, :]
        den = w0 * l_scr[0, rows, :] + w1 * l_scr[1, rows, :] + w2 * l_scr[2, rows, :]
        o_ref[0, rows, :] = (num / den * _silu(z_ref[0, rows, :].astype(F32))).astype(BF16)
        return 0

    lax.fori_loop(0, G // chunk, combine, 0)


def _dilated(q, k, v, z):
    B, S, _ = q.shape
    G = min(DIL_GROUP, S)
    gspec = pl.BlockSpec((1, G, LANES), lambda b, p, g: (b, g, p))
    kspec = pl.BlockSpec((1, S, LANES), lambda b, p, g: (b, 0, p))
    scr = pltpu.VMEM((len(DIL_CONFIGS), G, LANES), F32)
    return pl.pallas_call(
        functools.partial(_dil_kernel, G=G),
        grid=(B, 2, S // G),
        in_specs=[gspec, kspec, kspec, gspec],
        out_specs=gspec,
        out_shape=jax.ShapeDtypeStruct((B, S, MIX_W), BF16),
        scratch_shapes=[scr, scr, scr],
        compiler_params=pltpu.CompilerParams(
            dimension_semantics=("parallel", "parallel", "arbitrary"), vmem_limit_bytes=VMEM_LIMIT),
        name="dilated_window",
    )(q, k, v, z)


def _compress_kernel(k_ref, v_ref, p_ref, w1_ref, b1_ref, w2k_ref, w2vt_ref, cos_ref, sa_ref, sb_ref,
                     kc_ref, vc_ref):
    n = kc_ref.shape[1]
    dh = k_ref.shape[-1]
    for s, (t_ref, o_ref) in enumerate(((k_ref, kc_ref), (v_ref, vc_ref))):
        lo = jnp.zeros((n, CMP_HID), F32)
        hi = jnp.zeros((n, CMP_HID), F32)
        for l in range(CMP_STRIDE):
            t = t_ref[0, pl.ds(l, n, stride=CMP_STRIDE), :]
            lo = lo + _dot((t + p_ref[s, l:l + 1, :]).astype(BF16),
                           w1_ref[s, l * dh:(l + 1) * dh, :])
            lh = l + CMP_STRIDE
            hi = hi + _dot((t + p_ref[s, lh:lh + 1, :]).astype(BF16),
                           w1_ref[s, lh * dh:(lh + 1) * dh, :])
        pre = lo + pltpu.roll(hi, n - 1, 0) + b1_ref[s]
        hid = 0.5 * pre * (1.0 + jnp.tanh(np.sqrt(2.0 / np.pi).astype(np.float32)
                                          * (pre + 0.044715 * (pre * pre * pre))))
        hid = hid.astype(BF16)
        if s == 0:
            out = _dot(hid, w2k_ref[...])
            o_ref[0] = _rope_lane_tile(out, cos_ref[...], sa_ref[...], sb_ref[...]).astype(BF16)
        else:
            o_ref[0] = _nt_dot(w2vt_ref[...], hid).astype(BF16)


def _compress(kcr, vcr, cmp_pos, cmp_w1, cmp_b1, cmp_w2, tables_c):
    B, S, dh = kcr.shape
    n = S // CMP_STRIDE
    w1 = cmp_w1.astype(BF16)
    w2k = jnp.concatenate([cmp_w2[0], cmp_w2[0]], axis=-1).astype(BF16)
    w2vt = cmp_w2[1].T.astype(BF16)
    b1 = cmp_b1[:, None, :]
    cos, sa, sb = tables_c
    const = lambda a: pl.BlockSpec(a.shape, lambda b: (0,) * a.ndim)
    tspec = pl.BlockSpec((1, S, dh), lambda b: (b, 0, 0))
    return pl.pallas_call(
        _compress_kernel,
        grid=(B,),
        in_specs=[tspec, tspec, const(cmp_pos), const(w1), const(b1), const(w2k), const(w2vt),
                  const(cos), const(sa), const(sb)],
        out_specs=[pl.BlockSpec((1, n, LANES), lambda b: (b, 0, 0)),
                   pl.BlockSpec((1, dh, n), lambda b: (b, 0, 0))],
        out_shape=[jax.ShapeDtypeStruct((B, n, LANES), BF16),
                   jax.ShapeDtypeStruct((B, dh, n), BF16)],
        compiler_params=pltpu.CompilerParams(
            dimension_semantics=("parallel",), vmem_limit_bytes=VMEM_LIMIT),
        name="compress_tokens",
    )(kcr, vcr, cmp_pos, w1, b1, w2k, w2vt, cos, sa, sb)


def _nsa_kernel(q_ref, kc_ref, vct_ref, ks_ref, vst_ref, kw_ref, vwt_ref, gc_ref, z_ref,
                ovt_ref, ext_ref, o_ref, tile_scr, *, tq, tk, n_sel):
    i = pl.program_id(1)
    Q = N_HEADS * tq
    is_a = _head_masks((tq, LANES))
    qs = q_ref[0]
    q4 = jnp.concatenate([_stack_pair(qs[:, :LANES], is_a), _stack_pair(qs[:, LANES:], is_a)], axis=0)
    heads = lambda x: jnp.concatenate([x] * N_HEADS, axis=1)
    qpos1 = i * tq + lax.broadcasted_iota(jnp.int32, (1, tq), 1)
    qpos = heads(qpos1)

    kpos0 = lax.broadcasted_iota(jnp.int32, (tk, Q), 0)
    J = (i * tq) // tk

    n_c = kc_ref.shape[1]
    c_end = lax.broadcasted_iota(jnp.int32, (n_c, Q), 0) * CMP_STRIDE + (CMP_LEN - 1)
    c_valid = c_end <= qpos
    sc = jnp.where(c_valid, _nt_dot(kc_ref[0], q4), NEG)
    pc = jnp.exp2(sc - jnp.max(sc, axis=0, keepdims=True)) * jnp.where(c_valid, 1.0, 0.0)
    lc = jnp.sum(pc, axis=0, keepdims=True)
    pc = pc * jnp.where(lc > 0.0, 1.0 / lc, 0.0)
    o_cmp = _dot(vct_ref[0], pc.astype(BF16))

    pcs = pc[:, 0:tq] + pc[:, tq:2 * tq] + pc[:, 2 * tq:3 * tq] + pc[:, 3 * tq:]
    hi, lo = _split_bf16(pcs)
    imp = _dot(ovt_ref[...], hi) + _dot(ovt_ref[...], lo)
    nsp = imp.shape[0]
    blk = lax.broadcasted_iota(jnp.int32, (nsp, tq), 0)
    own = lax.shift_right_logical(qpos1, int(np.log2(SEL_LEN)))
    s_valid = blk <= own
    forced = jnp.logical_or(blk == 0, blk >= own - 1)
    imp = jnp.where(s_valid, jnp.where(forced, BIG, imp), NEG)
    rank = _rank_rows(imp, n_sel)
    chosen = jnp.logical_and(rank < float(min(SEL_TOPN, n_sel)), s_valid)
    sel = jnp.where(chosen, 1.0, 0.0).astype(BF16)

    n_win = -(-WIN_LEN // tk) + 1
    s_w, vt_w = [], []
    for a in range(n_win):
        j_a = J - (n_win - 1) + a
        qrel = qpos - j_a * tk
        ok = jnp.broadcast_to(j_a >= 0, (tk, Q))
        if a == 0:
            ok = jnp.logical_and(ok, kpos0 > qrel - WIN_LEN)
        if a == n_win - 1:
            ok = kpos0 <= qrel
        j_c = jnp.maximum(j_a, 0)
        s_w.append(jnp.where(ok, _nt_dot(_tile_rows(kw_ref, j_c, tk), q4), NEG))
        vt_w.append(_tile_lanes(vwt_ref, j_c, tk))
    m_w = functools.reduce(jnp.maximum, [jnp.max(s, axis=0, keepdims=True) for s in s_w])
    p_w = [jnp.exp2(s - m_w) for s in s_w]
    l_w = sum(jnp.sum(p, axis=0, keepdims=True) for p in p_w)
    o_win = sum(_dot(vt, p.astype(BF16)) for vt, p in zip(vt_w, p_w)) / l_w

    def sel_scores(j, diag):
        bias = (_dot(_tile_rows_2d(ext_ref, j, tk), sel) - 1.0) * (-NEG)
        s = _nt_dot(_tile_rows(ks_ref, j, tk), q4) + heads(bias)
        if diag:
            s = jnp.where(j * tk + kpos0 <= qpos, s, NEG)
        return (j, jnp.max(s, axis=0, keepdims=True)), [s]

    def consume(blk_s, st):
        (j, s_max), (s,) = blk_s
        m, l, acc = st
        m_new = jnp.maximum(m, s_max)
        alpha = jnp.exp2(m - m_new)
        p = jnp.exp2(s - m_new)
        l = alpha * l + jnp.sum(p, axis=0, keepdims=True)
        acc = alpha * acc + _dot(_tile_lanes(vst_ref, j, tk), p.astype(BF16))
        return m_new, l, acc

    init = (jnp.full((1, Q), NEG, F32), jnp.zeros((1, Q), F32), jnp.zeros((HEAD_DIM, Q), F32))

    _, l_s, a_s = _pipelined(J, sel_scores(J, True), lambda t: sel_scores(t, False), consume, init,
                             tile_scr, quads=True)
    o_sel = a_s / l_s

    gates = jax.nn.sigmoid(gc_ref[0]).T
    outs = []
    for h in range(N_HEADS):
        cols = slice(h * tq, (h + 1) * tq)
        outs.append(gates[3 * h:3 * h + 1, :] * o_cmp[:, cols]
                    + gates[3 * h + 1:3 * h + 2, :] * o_sel[:, cols]
                    + gates[3 * h + 2:3 * h + 3, :] * o_win[:, cols])
    out = jnp.concatenate(outs, axis=0).T
    o_ref[0] = (out * _silu(z_ref[0].astype(F32))).astype(BF16)


def _nsa(q, kc2, vct, ks2, vst, kw2, vwt, gc, z, tq=256, tk=256):
    B, S, _ = q.shape
    n_c = kc2.shape[1]
    n_sel = S // SEL_LEN
    nsp = -(-n_sel // SUBLANES) * SUBLANES
    n_cmp = (S - CMP_LEN) // CMP_STRIDE + 1
    c_start = np.arange(n_c) * CMP_STRIDE
    s_start = np.arange(nsp) * SEL_LEN
    overlap_t = np.clip(np.minimum(c_start[None, :] + CMP_LEN, s_start[:, None] + SEL_LEN)
                        - np.maximum(c_start[None, :], s_start[:, None]), 0, None) / CMP_LEN
    overlap_t[:, n_cmp:] = 0.0
    overlap_t[n_sel:, :] = 0.0
    expand_t = (np.arange(S)[:, None] // SEL_LEN == np.arange(nsp)[None, :]).astype(np.float32)
    row = lambda w: pl.BlockSpec((1, tq, w), lambda b, i: (b, i, 0))
    full = lambda a: pl.BlockSpec((1,) + a.shape[1:], lambda b, i: (b, 0, 0))
    const = lambda a: pl.BlockSpec(a.shape, lambda b, i: (0, 0))
    ovt = jnp.asarray(overlap_t, BF16)
    ext = jnp.asarray(expand_t, BF16)
    return pl.pallas_call(
        functools.partial(_nsa_kernel, tq=tq, tk=tk, n_sel=n_sel),
        grid=(B, S // tq),
        in_specs=[row(MIX_W), full(kc2), full(vct), full(ks2), full(vst), full(kw2), full(vwt),
                  row(LANES), row(MIX_W), const(ovt), const(ext)],
        out_specs=row(MIX_W),
        out_shape=jax.ShapeDtypeStruct((B, S, MIX_W), BF16),
        scratch_shapes=[pltpu.VMEM((2, 1, tk, N_HEADS * tq), F32)],
        compiler_params=pltpu.CompilerParams(
            dimension_semantics=("parallel", "parallel"), vmem_limit_bytes=VMEM_LIMIT),
        name="native_sparse",
    )(q, kc2, vct, ks2, vst, kw2, vwt, gc, z, ovt, ext)


def _moba_kernel(q_ref, k_ref, vt_ref, z_ref, o_ref, km_scr, bias_scr, tile_scr, *, n_blk, tq):
    i = pl.program_id(1)
    tk = MOBA_BLOCK
    n_diag = tq // tk
    nb_pad = km_scr.shape[0]
    is_a = _head_masks((tq, LANES))

    @pl.when(i == 0)
    def _():
        km_scr[...] = jnp.zeros_like(km_scr)
        for b in range(n_blk):
            kb = k_ref[0, b * tk:(b + 1) * tk, :].astype(F32)
            km_scr[b:b + 1, :] = jnp.mean(kb, axis=0, keepdims=True)

    q = q_ref[0]
    blk = lax.broadcasted_iota(jnp.int32, (nb_pad, 2 * tq), 0)
    qloc = jnp.bitwise_and(lax.broadcasted_iota(jnp.int32, (1, 2 * tq), 1), tq - 1)
    own = i * n_diag + lax.shift_right_logical(qloc, int(np.log2(tk)))
    n_top = min(MOBA_TOPK, max(n_blk - 1, 1))
    qcat = []
    for p in range(N_PAIRS):
        qcat.append(_stack_pair(q[:, _pair_lanes(p)], is_a))
        km_hi, km_lo = _split_bf16(km_scr[:, _pair_lanes(p)])
        gsc = _nt_dot(km_hi, qcat[p]) + _nt_dot(km_lo, qcat[p])
        gsc = jnp.where(blk < own, gsc, NEG)
        rank = _rank_rows(gsc, n_blk)
        attends = jnp.logical_or(jnp.logical_and(rank < float(n_top), blk < own), blk == own)
        bias_scr[p] = jnp.where(attends, 0.0, NEG)

    kloc = lax.broadcasted_iota(jnp.int32, (tk, 2 * tq), 0)

    def scores(j, diag):
        ks = _tile_rows(k_ref, j, tk)
        ss = [_nt_dot(ks[:, _pair_lanes(p)], qcat[p]) for p in range(N_PAIRS)]
        if diag:
            causal = kloc <= qloc - (j * tk - i * tq)
            ss = [jnp.where(causal, s, NEG) for s in ss]
        return (j, [jnp.max(s, axis=0, keepdims=True) for s in ss]), ss

    def consume(blk_s, st):
        (j, maxes), ss = blk_s
        vts = _tile_lanes(vt_ref, j, tk)
        pr = range(N_PAIRS)
        bias = [bias_scr[p, pl.ds(j, 1), :] for p in pr]
        m_new = [jnp.maximum(st[p][0], maxes[p] + bias[p]) for p in pr]
        ps = [jnp.exp2(ss[p] - (m_new[p] - bias[p])) for p in pr]
        pvs = [_pv_pair(vts[_pair_lanes(p)], ps[p].astype(BF16), tq) for p in pr]
        out = []
        for p in pr:
            m, l, acc = st[p]
            alpha = jnp.exp2(m - m_new[p])
            out.append((m_new[p], alpha * l + jnp.sum(ps[p], axis=0, keepdims=True),
                        alpha * acc + pvs[p]))
        return out

    st = [(jnp.full((1, 2 * tq), NEG, F32), jnp.zeros((1, 2 * tq), F32),
           jnp.zeros((HEAD_DIM, 2 * tq), F32))] * N_PAIRS
    for d in range(n_diag - 1, 0, -1):
        st = consume(scores(i * n_diag + d, True), st)
    st = _pipelined(i * n_diag, scores(i * n_diag, True), lambda t: scores(t, False), consume, st,
                    tile_scr, quads=True)
    _finish_pairs([acc / l for _, l, acc in st], tq, z_ref, o_ref)


def _moba(q, k, vt, z, tq=256):
    B, S, _ = q.shape
    tq = min(tq, S)
    n_blk = S // MOBA_BLOCK
    nb_pad = -(-n_blk // SUBLANES) * SUBLANES
    qspec, kspec, vtspec = _all_pair_specs(S, tq)
    return pl.pallas_call(
        functools.partial(_moba_kernel, n_blk=n_blk, tq=tq),
        grid=(B, S // tq),
        in_specs=[qspec, kspec, vtspec, qspec],
        out_specs=qspec,
        out_shape=jax.ShapeDtypeStruct((B, S, MIX_W), BF16),
        scratch_shapes=[pltpu.VMEM((nb_pad, MIX_W), F32),
                        pltpu.VMEM((N_PAIRS, nb_pad, 2 * tq), F32),
                        pltpu.VMEM((2, N_PAIRS, MOBA_BLOCK, 2 * tq), F32)],
        compiler_params=pltpu.CompilerParams(
            dimension_semantics=("parallel", "arbitrary"), vmem_limit_bytes=VMEM_LIMIT),
        name="moba",
    )(q, k, vt, z)


def _out_kernel(ma_ref, mb_ref, mc_ref, md_ref, w_ref, x_ref, mod_ref, g_ref, o_ref):
    D = x_ref.shape[-1]
    acc = _dot(ma_ref[0], w_ref[0:MIX_W, :])
    acc = acc + _dot(mb_ref[0], w_ref[MIX_W:2 * MIX_W, :])
    acc = acc + _dot(mc_ref[0], w_ref[2 * MIX_W:3 * MIX_W, :])
    acc = acc + _dot(md_ref[0], w_ref[3 * MIX_W:, :])
    ms = jnp.mean(acc * acc, axis=-1, keepdims=True)
    y = acc * lax.rsqrt(ms + EPS) * g_ref[...]
    o_ref[0] = x_ref[0] + mod_ref[0, :, 2 * D:] * y


def _out_projection(mixed, w_out, x, mod, g_post, tm):
    B, S, D = x.shape
    row = lambda w: pl.BlockSpec((1, tm, w), lambda b, i: (b, i, 0))
    return pl.pallas_call(
        _out_kernel,
        grid=(B, S // tm),
        in_specs=[row(MIX_W)] * 4 + [pl.BlockSpec(w_out.shape, lambda b, i: (0, 0)), row(D),
                                     pl.BlockSpec((1, 1, 3 * D), lambda b, i: (b, 0, 0)),
                                     pl.BlockSpec((1, D), lambda b, i: (0, 0))],
        out_specs=row(D),
        out_shape=jax.ShapeDtypeStruct((B, S, D), F32),
        compiler_params=pltpu.CompilerParams(
            dimension_semantics=("parallel", "parallel"), vmem_limit_bytes=VMEM_LIMIT),
        name="out_projection",
    )(*mixed, w_out.astype(BF16), x, mod[:, None, :], g_post[None])


def _layer(x, c, norm_pre, norm_post, w_mod, b_mod, w_in, w_out, cmp_pos, cmp_w1, cmp_b1, cmp_w2,
           tables, tables_c, tm):
    mod = _modulation(c, w_mod, b_mod)
    (qa, ka, za, qb, kb, vb, zb, qc, kcr, vcr, gc, ks2, kw2, zc,
     qd, kd, zd, vat, vdt, vst, vwt) = _projection(x, mod, norm_pre, w_in, tables, tm)
    oa = _stick_breaking(qa, ka, vat, za)
    ob = _dilated(qb, kb, vb, zb)
    kc2, vct = _compress(kcr, vcr, cmp_pos, cmp_w1, cmp_b1, cmp_w2, tables_c)
    oc = _nsa(qc, kc2, vct, ks2, vst, kw2, vwt, gc, zc)
    od = _moba(qd, kd, vdt, zd)
    return _out_projection((oa, ob, oc, od), w_out, x, mod, norm_post, min(2 * tm, x.shape[1]))


def kernel(x, c, norm_pre, norm_post, w_mod, b_mod, w_in, w_out, cmp_pos, cmp_w1, cmp_b1, cmp_w2):
    S = x.shape[1]
    tables = _rope_lane_tables(np.arange(S))
    tables_c = _rope_lane_tables(np.arange(S // CMP_STRIDE) * CMP_STRIDE + CMP_LEN - 1)
    tm = min(512, S)
    for l in range(norm_pre.shape[0]):
        x = _layer(x, c, norm_pre[l], norm_post[l], w_mod[l], b_mod[l], (w_in, l), w_out[l],
                   cmp_pos[l], cmp_w1[l], cmp_b1[l], cmp_w2[l], tables, tables_c, tm)
    return x
```

```python
import functools

import numpy as np
import jax
import jax.numpy as jnp
from jax import lax
from jax.experimental import pallas as pl
from jax.experimental.pallas import tpu as pltpu

F32 = jnp.float32
BF16 = jnp.bfloat16

N_HEADS = 4
HEAD_DIM = 64
MIX_W = N_HEADS * HEAD_DIM
ROPE_THETA = 500000.0
ROPE_DIM = HEAD_DIM // 4
ROPE_HALF = ROPE_DIM // 2
EPS = 1e-6
NEG = -1e30
BIG = 1e9
SCALE = HEAD_DIM ** -0.5
LOG2E = 1.4426950408889634
SCALE_LOG2 = SCALE * LOG2E
DIL_CONFIGS = ((128, 1), (512, 4), (2048, 16))
CMP_LEN = 32
CMP_STRIDE = 16
CMP_HID = 256
SEL_LEN = 64
SEL_TOPN = 16
WIN_LEN = 512
MOBA_BLOCK = 256
MOBA_TOPK = 3

N_PAIRS = N_HEADS // 2
LANES = 128
SUBLANES = 8
TILE_W = 2 * LANES
KEY_TILE = 256
DIL_GROUP = 2048
VMEM_LIMIT = 56 * 1024 * 1024


def _nt_dot(a, b):
    return lax.dot_general(a, b, (((1,), (1,)), ((), ())), preferred_element_type=F32)


def _dot(a, b):
    return jnp.dot(a, b, preferred_element_type=F32)


def _split_bf16(x):
    hi = x.astype(BF16)
    lo = (x - hi.astype(F32)).astype(BF16)
    return hi, lo


def _silu(x):
    return x * jax.nn.sigmoid(x)


def _head_masks(shape):
    lane = lax.broadcasted_iota(jnp.int32, shape, 1)
    return lane < HEAD_DIM


def _pv_pair(vt, p, tq):
    return jnp.concatenate([_dot(vt[:HEAD_DIM], p[:, :tq]), _dot(vt[HEAD_DIM:], p[:, tq:])], axis=1)


def _mod_kernel(c_ref, w_ref, b_ref, o_ref):
    c = c_ref[...]
    o_ref[...] = jnp.dot(_silu(c), w_ref[...], preferred_element_type=F32,
                         precision=lax.Precision.HIGHEST) + b_ref[...]


def _modulation(c, w_mod_all, layer, b_mod):
    B, D = c.shape
    N = w_mod_all.shape[2]
    rows = SUBLANES
    cp = jnp.zeros((rows, D), F32).at[:B].set(c)
    tn = N // 4
    out = pl.pallas_call(
        _mod_kernel,
        grid=(N // tn,),
        in_specs=[pl.BlockSpec((rows, D), lambda j: (0, 0)),
                  pl.BlockSpec((None, D, tn), lambda j: (layer, 0, j)),
                  pl.BlockSpec((1, tn), lambda j: (0, j))],
        out_specs=pl.BlockSpec((rows, tn), lambda j: (0, j)),
        out_shape=jax.ShapeDtypeStruct((rows, N), F32),
        name="modulation",
    )(cp, w_mod_all, b_mod[None])
    return out[:B]


def _rope_lane_tile(y, cos, sa, sb):
    return y * cos + pltpu.roll(y, LANES - ROPE_HALF, 1) * sa + pltpu.roll(y, ROPE_HALF, 1) * sb


def _pack_weights(win_ref, w_ref, wt_ref):
    D = win_ref.shape[0]
    o = IN_OFFS
    chunk = 256
    for r0 in range(0, D, chunk):
        rs = slice(r0, r0 + chunk)

        def cp(dst, a, b):
            w_ref[rs, dst:dst + (b - a)] = win_ref[rs, a:b].astype(BF16)

        cp(0, o[0], o[11])
        small = PROJ_TILES["c_small"] * TILE_W
        cp(small + LANES, o[15], o[16])
        pad = small + LANES + 3 * N_HEADS
        w_ref[rs, pad:small + TILE_W] = jnp.zeros((chunk, small + TILE_W - pad), BF16)
        keys = PROJ_TILES["c_keys"] * TILE_W
        for rep in range(2):
            cp(keys + rep * HEAD_DIM, o[11], o[12])
            cp(keys + LANES + rep * HEAD_DIM, o[13], o[14])
        cp(PROJ_TILES["zc"] * TILE_W, o[16], o[21])

    def put_t(dst, a, lo=0, n=LANES):
        wt_ref[dst:dst + n, :] = win_ref[:, a:a + LANES].astype(F32).T[lo:lo + n].astype(BF16)

    for h in range(2):
        put_t(h * LANES, o[2] + h * LANES)
        put_t(MIX_W + h * LANES, o[19] + h * LANES)
    put_t(2 * MIX_W, o[11], HEAD_DIM, HEAD_DIM)
    put_t(2 * MIX_W + HEAD_DIM, o[13], HEAD_DIM, HEAD_DIM)


def _proj_kernel(x_ref, mod_ref, g_ref, win_ref, cos_ref, sa_ref, sb_ref,
                 qa_ref, ka_ref, za_ref,
                 qb_ref, kb_ref, vb_ref, zb_ref,
                 qc_ref, kcr_ref, vcr_ref, gc_ref, ks_ref, kw_ref, zc_ref,
                 qd_ref, kd_ref, zd_ref, vat_ref, vdt_ref, vst_ref, vwt_ref, w_ref, wt_ref):
    @pl.when(jnp.logical_and(pl.program_id(0) == 0, pl.program_id(1) == 0))
    def _():
        _pack_weights(win_ref, w_ref, wt_ref)

    D = x_ref.shape[-1]
    x = x_ref[0]
    ms = jnp.mean(x * x, axis=-1, keepdims=True)
    y = x * lax.rsqrt(ms + EPS) * g_ref[...]
    shift = mod_ref[0, :, 0:D]
    scale = mod_ref[0, :, D:2 * D]
    h = (y * (1.0 + scale) + shift).astype(BF16)
    cos, sa, sb = cos_ref[...], sa_ref[...], sb_ref[...]

    def tile(t):
        return _dot(h, w_ref[:, t * TILE_W:(t + 1) * TILE_W])

    def rope_lo(y):
        return _rope_lane_tile(y[:, :LANES], cos, sa, sb)

    def rope_all(y):
        return jnp.concatenate([rope_lo(y), _rope_lane_tile(y[:, LANES:], cos, sa, sb)], axis=1)

    T = PROJ_TILES
    qa_ref[0] = tile(T["qa"]).astype(BF16)
    ka_ref[0] = tile(T["ka"]).astype(BF16)
    za_ref[0] = tile(T["za"]).astype(BF16)
    qb_ref[0] = rope_all(tile(T["qb"]))
    kb_ref[0] = rope_all(tile(T["kb"]))
    vb_ref[0] = tile(T["vb"])
    zb_ref[0] = tile(T["zb"]).astype(BF16)
    qc_ref[0] = (rope_all(tile(T["qc"])) * SCALE_LOG2).astype(BF16)
    small = tile(T["c_small"])
    kcr_ref[0] = small[:, 0:HEAD_DIM]
    vcr_ref[0] = small[:, HEAD_DIM:LANES]
    gc_ref[0] = small[:, LANES:]
    keys = rope_all(tile(T["c_keys"])).astype(BF16)
    ks_ref[0] = keys[:, :LANES]
    kw_ref[0] = keys[:, LANES:]
    zc_ref[0] = tile(T["zc"]).astype(BF16)
    qd_ref[0] = (rope_all(tile(T["qd"])) * SCALE_LOG2).astype(BF16)
    kd_ref[0] = rope_all(tile(T["kd"])).astype(BF16)
    zd_ref[0] = tile(T["zd"]).astype(BF16)
    vt = _nt_dot(wt_ref[...], h).astype(BF16)
    vat_ref[0] = vt[0:MIX_W]
    vdt_ref[0] = vt[MIX_W:2 * MIX_W]
    vst_ref[0] = vt[2 * MIX_W:2 * MIX_W + HEAD_DIM]
    vwt_ref[0] = vt[2 * MIX_W + HEAD_DIM:]


PROJ_TILES = dict(qa=0, ka=1, za=3, qb=4, kb=5, vb=6, zb=7, qc=8, c_small=9, c_keys=10,
                  zc=11, qd=12, kd=13, zd=15)
N_PROJ_TILES = 16
N_VT_ROWS = 2 * MIX_W + 2 * HEAD_DIM
IN_OFFS = [int(v) for v in np.cumsum([0] + [MIX_W] * 9 + [HEAD_DIM] * 6 + [3 * N_HEADS] + [MIX_W] * 5)]


def _rope_lane_tables(pos):
    f32 = np.float32
    inv_freq = (1.0 / (ROPE_THETA ** (np.arange(0, ROPE_DIM, 2, dtype=f32) / ROPE_DIM))).astype(f32)
    ang = np.asarray(pos, f32)[:, None] * inv_freq[None, :]
    cos, sin = np.cos(ang).astype(f32), np.sin(ang).astype(f32)
    n = cos.shape[0]
    rest = HEAD_DIM - ROPE_DIM
    cos_h = np.concatenate([cos, cos, np.ones((n, rest), f32)], axis=1)
    sa_h = np.concatenate([-sin, np.zeros((n, HEAD_DIM - ROPE_HALF), f32)], axis=1)
    sb_h = np.concatenate([np.zeros((n, ROPE_HALF), f32), sin, np.zeros((n, rest), f32)], axis=1)
    two = lambda t: jnp.asarray(np.concatenate([t, t], axis=1))
    return two(cos_h), two(sa_h), two(sb_h)


def _projection(x, mod, g_pre, w_in, layer, tables, tm):
    B, S, D = x.shape
    cos, sa, sb = tables
    row = lambda w: pl.BlockSpec((1, tm, w), lambda b, i: (b, i, 0))
    tab = pl.BlockSpec((tm, LANES), lambda b, i: (i, 0))
    widths = [(MIX_W, BF16)] * 3 + [(MIX_W, F32)] * 3 + [(MIX_W, BF16)] + \
             [(MIX_W, BF16), (HEAD_DIM, F32), (HEAD_DIM, F32), (LANES, F32),
              (LANES, BF16), (LANES, BF16), (MIX_W, BF16)] + \
             [(MIX_W, BF16)] * 3
    t_rows = (MIX_W, MIX_W, HEAD_DIM, HEAD_DIM)
    t_specs = [pl.BlockSpec((1, r, tm), lambda b, i: (b, 0, i)) for r in t_rows]
    t_shapes = [jax.ShapeDtypeStruct((B, r, S), BF16) for r in t_rows]
    return pl.pallas_call(
        _proj_kernel,
        grid=(B, S // tm),
        in_specs=[row(D),
                  pl.BlockSpec((1, 1, 3 * D), lambda b, i: (b, 0, 0)),
                  pl.BlockSpec((1, D), lambda b, i: (0, 0)),
                  pl.BlockSpec((None,) + w_in.shape[1:], lambda b, i: (layer, 0, 0),
                               pipeline_mode=pl.Buffered(1)),
                  tab, tab, tab],
        out_specs=[row(w) for w, _ in widths] + t_specs,
        out_shape=[jax.ShapeDtypeStruct((B, S, w), dt) for w, dt in widths] + t_shapes,
        scratch_shapes=[pltpu.VMEM((D, N_PROJ_TILES * TILE_W), BF16),
                        pltpu.VMEM((N_VT_ROWS, D), BF16)],
        compiler_params=pltpu.CompilerParams(
            dimension_semantics=("arbitrary", "arbitrary"), vmem_limit_bytes=VMEM_LIMIT),
        name="in_projection",
    )(x, mod[:, None, :], g_pre[None], w_in, cos, sa, sb)


def _all_pair_specs(S, tq):
    qspec = pl.BlockSpec((1, tq, MIX_W), lambda b, i: (b, i, 0))
    kspec = pl.BlockSpec((1, S, MIX_W), lambda b, i: (b, 0, 0))
    vtspec = pl.BlockSpec((1, MIX_W, S), lambda b, i: (b, 0, 0))
    return qspec, kspec, vtspec


def _pair_lanes(p):
    return slice(p * LANES, (p + 1) * LANES)


def _tile_rows(ref, j, tk):
    return ref[0, pl.ds(pl.multiple_of(j * tk, tk), tk), :]


def _tile_rows_2d(ref, j, tk):
    return ref[pl.ds(pl.multiple_of(j * tk, tk), tk), :]


def _tile_lanes(ref, j, tk):
    return ref[0, :, pl.ds(pl.multiple_of(j * tk, tk), tk)]


def _stack_pair(q, is_a):
    return jnp.concatenate([jnp.where(is_a, q, 0), jnp.where(is_a, 0, q)], axis=0)


def _rank_rows(v, n):
    rows = v.shape[0]
    rank = jnp.zeros(v.shape, F32)
    for s in range(n):
        c = v[s:s + 1, :]
        lo = s // SUBLANES * SUBLANES
        hi = lo + SUBLANES
        grp = lax.broadcasted_iota(jnp.int32, (SUBLANES, v.shape[1]), 0) + lo
        mixed = jnp.logical_or(c > v[lo:hi], jnp.logical_and(c == v[lo:hi], grp > s))
        parts = [jnp.where(c > v[:lo], 1.0, 0.0)] if lo else []
        parts.append(jnp.where(mixed, 1.0, 0.0))
        if hi < rows:
            parts.append(jnp.where(c >= v[hi:], 1.0, 0.0))
        rank = rank + jnp.concatenate(parts, axis=0)
    return rank


def _pipelined(n, first, produce, consume, state, scr, quads=False):
    def put(slot, tiles):
        for a, tile in enumerate(tiles):
            scr[slot, a] = tile

    def get(slot):
        return [scr[slot, a] for a in range(scr.shape[1])]

    def step(t, slot, carry):
        small, st = carry
        small_new, tiles = produce(t)
        put(1 - slot, tiles)
        return small_new, consume((small, get(slot)), st)

    def pair(t, c):
        return step(t + 1, 1, step(t, 0, c))

    put(0, first[1])
    carry = (first[0], state)
    done = 0
    if quads:
        carry = lax.fori_loop(0, lax.shift_right_logical(n, 2),
                              lambda u, c: pair(4 * u + 2, pair(4 * u, c)), carry)
        done = lax.shift_left(lax.shift_right_logical(n, 2), 2)
    carry = lax.fori_loop(0, lax.shift_right_logical(n - done, 1),
                          lambda u, c: pair(done + 2 * u, c), carry)

    def odd_tail(c):
        small, st = step(n - 1, 0, c)
        return consume((small, get(1)), st)

    return lax.cond(jnp.bitwise_and(n, 1) == 1, odd_tail,
                    lambda c: consume((c[0], get(0)), c[1]), carry)


def _finish_pairs(accs, tq, z_ref, o_ref):
    heads = [a[:, h * tq:(h + 1) * tq] for a in accs for h in range(2)]
    out = jnp.concatenate(heads, axis=0).T
    o_ref[0] = (out * _silu(z_ref[0].astype(F32))).astype(BF16)


def _stick_kernel(q_ref, k_ref, vt_ref, z_ref, o_ref, tile_scr, *, tq):
    i = pl.program_id(1)
    tk = KEY_TILE
    n_diag = tq // tk
    is_a = _head_masks((tq, LANES))
    qs = (q_ref[0].astype(F32) * SCALE).astype(BF16)
    qcat = [_stack_pair(qs[:, _pair_lanes(p)], is_a) for p in range(N_PAIRS)]
    r_i = lax.broadcasted_iota(jnp.int32, (tk, tk), 0)
    c_i = lax.broadcasted_iota(jnp.int32, (tk, tk), 1)
    suffix = jnp.where(c_i >= r_i, 1.0, 0.0).astype(BF16)
    kloc = lax.broadcasted_iota(jnp.int32, (tk, 2 * tq), 0)
    qloc = jnp.bitwise_and(lax.broadcasted_iota(jnp.int32, (1, 2 * tq), 1), tq - 1)

    def log_weights(j, diag):
        ks = _tile_rows(k_ref, j, tk)
        zs = [_nt_dot(ks[:, _pair_lanes(p)], qcat[p]) for p in range(N_PAIRS)]
        sps = [jnp.maximum(z, 0.0) + jnp.log(1.0 + jnp.exp2(jnp.abs(z) * (-LOG2E))) for z in zs]
        if diag:
            past = kloc < qloc - (j * tk - i * tq)
            sps = [jnp.where(past, sp, 0.0) for sp in sps]
        css = [_dot(suffix, sp.astype(BF16)) for sp in sps]
        xs = [z - cs for z, cs in zip(zs, css)]
        if diag:
            xs = [jnp.where(past, x, NEG) for x in xs]
        return (j, [cs[0:1, :] for cs in css]), xs

    def accumulate(blk, st):
        (j, tots), xs = blk
        vts = _tile_lanes(vt_ref, j, tk)
        ws = [jnp.exp(xs[p] - st[p][1]).astype(BF16) for p in range(N_PAIRS)]
        return [(st[p][0] + _pv_pair(vts[_pair_lanes(p)], ws[p], tq), st[p][1] + tots[p])
                for p in range(N_PAIRS)]

    st = [(jnp.zeros((HEAD_DIM, 2 * tq), F32), jnp.zeros((1, 2 * tq), F32))] * N_PAIRS
    top = i * n_diag + n_diag - 1
    for d in range(n_diag - 1):
        st = accumulate(log_weights(top - d, True), st)
    st = _pipelined(i * n_diag, log_weights(i * n_diag, True),
                    lambda t: log_weights(i * n_diag - 1 - t, False), accumulate, st, tile_scr)
    _finish_pairs([acc for acc, _ in st], tq, z_ref, o_ref)


def _stick_breaking(q, k, vt, z, tq=512):
    B, S, _ = q.shape
    tq = min(tq, S)
    qspec, kspec, vtspec = _all_pair_specs(S, tq)
    return pl.pallas_call(
        functools.partial(_stick_kernel, tq=tq),
        grid=(B, S // tq),
        in_specs=[qspec, kspec, vtspec, qspec],
        out_specs=qspec,
        out_shape=jax.ShapeDtypeStruct((B, S, MIX_W), BF16),
        scratch_shapes=[pltpu.VMEM((2, N_PAIRS, KEY_TILE, 2 * tq), F32)],
        compiler_params=pltpu.CompilerParams(
            dimension_semantics=("parallel", "arbitrary"), vmem_limit_bytes=VMEM_LIMIT),
        name="stick_breaking",
    )(q, k, vt, z)


def _dil_kernel(q_ref, k_ref, v_ref, z_ref, o_ref, m_scr, l_scr, a_scr, *, G):
    g = pl.program_id(2)
    blk = LANES
    is_a = _head_masks((blk, LANES))
    qrow = jnp.bitwise_and(lax.broadcasted_iota(jnp.int32, (2 * blk, 2 * blk), 0), blk - 1)
    col = lax.broadcasted_iota(jnp.int32, (2 * blk, 2 * blk), 1)
    band = jnp.logical_and(col >= qrow, col <= qrow + blk)
    n_sub = G // blk
    unroll = 16

    for ci, (_, d) in enumerate(DIL_CONFIGS):
        nb = G // (blk * d)
        sh = int(np.log2(nb))

        def sub(n, ci=ci, d=d, nb=nb, sh=sh):
            r = lax.shift_right_logical(n, sh)
            ub = jnp.bitwise_and(n, nb - 1)
            loc = ub * (blk * d) + r
            glob = g * G + loc
            has_prev = glob >= blk * d
            pstart = jnp.maximum(glob - blk * d, r)
            qs = (q_ref[0, pl.ds(loc, blk, stride=d), :] * SCALE_LOG2).astype(BF16)
            kk = jnp.concatenate([k_ref[0, pl.ds(pstart, blk, stride=d), :],
                                  k_ref[0, pl.ds(glob, blk, stride=d), :]], axis=0).astype(BF16)
            vv = jnp.concatenate([v_ref[0, pl.ds(pstart, blk, stride=d), :],
                                  v_ref[0, pl.ds(glob, blk, stride=d), :]], axis=0).astype(BF16)
            ok = jnp.logical_and(band, jnp.logical_or(col >= blk, has_prev))
            s = jnp.where(ok, _nt_dot(_stack_pair(qs, is_a), kk), NEG)
            m = jnp.max(s, axis=-1, keepdims=True)
            p = jnp.exp2(s - m)
            l = jnp.sum(p, axis=-1, keepdims=True)
            acc = _dot(p.astype(BF16), vv)
            rows = pl.ds(loc, blk, stride=d)
            m_scr[ci, rows, :] = jnp.where(is_a, m[:blk], m[blk:])
            l_scr[ci, rows, :] = jnp.where(is_a, l[:blk], l[blk:])
            a_scr[ci, rows, :] = jnp.where(is_a, acc[:blk], acc[blk:])

        def trip(t, _, sub=sub):
            for u in range(unroll):
                sub(t * unroll + u)
            return 0

        lax.fori_loop(0, n_sub // unroll, trip, 0)

    chunk = 256

    def combine(c, _):
        rows = pl.ds(pl.multiple_of(c * chunk, chunk), chunk)
        m0, m1, m2 = m_scr[0, rows, :], m_scr[1, rows, :], m_scr[2, rows, :]
        mm = jnp.maximum(jnp.maximum(m0, m1), m2)
        w0, w1, w2 = jnp.exp2(m0 - mm), jnp.exp2(m1 - mm), jnp.exp2(m2 - mm)
        num = w0 * a_scr[0, rows, :] + w1 * a_scr[1, rows, :] + w2 * a_scr[2, rows, :]
        den = w0 * l_scr[0, rows, :] + w1 * l_scr[1, rows, :] + w2 * l_scr[2, rows, :]
        o_ref[0, rows, :] = (num / den * _silu(z_ref[0, rows, :].astype(F32))).astype(BF16)
        return 0

    lax.fori_loop(0, G // chunk, combine, 0)


def _dilated(q, k, v, z):
    B, S, _ = q.shape
    G = min(DIL_GROUP, S)
    gspec = pl.BlockSpec((1, G, LANES), lambda b, p, g: (b, g, p))
    kspec = pl.BlockSpec((1, S, LANES), lambda b, p, g: (b, 0, p))
    scr = pltpu.VMEM((len(DIL_CONFIGS), G, LANES), F32)
    return pl.pallas_call(
        functools.partial(_dil_kernel, G=G),
        grid=(B, 2, S // G),
        in_specs=[gspec, kspec, kspec, gspec],
        out_specs=gspec,
        out_shape=jax.ShapeDtypeStruct((B, S, MIX_W), BF16),
        scratch_shapes=[scr, scr, scr],
        compiler_params=pltpu.CompilerParams(
            dimension_semantics=("parallel", "parallel", "arbitrary"), vmem_limit_bytes=VMEM_LIMIT),
        name="dilated_window",
    )(q, k, v, z)


def _compress_kernel(k_ref, v_ref, p_ref, w1_ref, b1_ref, w2k_ref, w2vt_ref, cos_ref, sa_ref, sb_ref,
                     kc_ref, vc_ref):
    n = kc_ref.shape[1]
    dh = k_ref.shape[-1]
    for s, (t_ref, o_ref) in enumerate(((k_ref, kc_ref), (v_ref, vc_ref))):
        lo = jnp.zeros((n, CMP_HID), F32)
        hi = jnp.zeros((n, CMP_HID), F32)
        for l in range(CMP_STRIDE):
            t = t_ref[0, pl.ds(l, n, stride=CMP_STRIDE), :]
            lo = lo + _dot((t + p_ref[s, l:l + 1, :]).astype(BF16),
                           w1_ref[s, l * dh:(l + 1) * dh, :])
            lh = l + CMP_STRIDE
            hi = hi + _dot((t + p_ref[s, lh:lh + 1, :]).astype(BF16),
                           w1_ref[s, lh * dh:(lh + 1) * dh, :])
        pre = lo + pltpu.roll(hi, n - 1, 0) + b1_ref[s]
        hid = 0.5 * pre * (1.0 + jnp.tanh(np.sqrt(2.0 / np.pi).astype(np.float32)
                                          * (pre + 0.044715 * (pre * pre * pre))))
        hid = hid.astype(BF16)
        if s == 0:
            out = _dot(hid, w2k_ref[...])
            o_ref[0] = _rope_lane_tile(out, cos_ref[...], sa_ref[...], sb_ref[...]).astype(BF16)
        else:
            o_ref[0] = _nt_dot(w2vt_ref[...], hid).astype(BF16)


def _compress(kcr, vcr, cmp_pos, cmp_w1, cmp_b1, cmp_w2, tables_c):
    B, S, dh = kcr.shape
    n = S // CMP_STRIDE
    w1 = cmp_w1.astype(BF16)
    w2k = jnp.concatenate([cmp_w2[0], cmp_w2[0]], axis=-1).astype(BF16)
    w2vt = cmp_w2[1].T.astype(BF16)
    b1 = cmp_b1[:, None, :]
    cos, sa, sb = tables_c
    const = lambda a: pl.BlockSpec(a.shape, lambda b: (0,) * a.ndim)
    tspec = pl.BlockSpec((1, S, dh), lambda b: (b, 0, 0))
    return pl.pallas_call(
        _compress_kernel,
        grid=(B,),
        in_specs=[tspec, tspec, const(cmp_pos), const(w1), const(b1), const(w2k), const(w2vt),
                  const(cos), const(sa), const(sb)],
        out_specs=[pl.BlockSpec((1, n, LANES), lambda b: (b, 0, 0)),
                   pl.BlockSpec((1, dh, n), lambda b: (b, 0, 0))],
        out_shape=[jax.ShapeDtypeStruct((B, n, LANES), BF16),
                   jax.ShapeDtypeStruct((B, dh, n), BF16)],
        compiler_params=pltpu.CompilerParams(
            dimension_semantics=("parallel",), vmem_limit_bytes=VMEM_LIMIT),
        name="compress_tokens",
    )(kcr, vcr, cmp_pos, w1, b1, w2k, w2vt, cos, sa, sb)


def _nsa_kernel(q_ref, kc_ref, vct_ref, ks_ref, vst_ref, kw_ref, vwt_ref, gc_ref, z_ref,
                ovt_ref, ext_ref, o_ref, tile_scr, *, tq, tk, n_sel):
    i = pl.program_id(1)
    Q = N_HEADS * tq
    is_a = _head_masks((tq, LANES))
    qs = q_ref[0]
    q4 = jnp.concatenate([_stack_pair(qs[:, :LANES], is_a), _stack_pair(qs[:, LANES:], is_a)], axis=0)
    heads = lambda x: jnp.concatenate([x] * N_HEADS, axis=1)
    qpos1 = i * tq + lax.broadcasted_iota(jnp.int32, (1, tq), 1)
    qpos = heads(qpos1)

    kpos0 = lax.broadcasted_iota(jnp.int32, (tk, Q), 0)
    J = (i * tq) // tk

    n_c = kc_ref.shape[1]
    c_end = lax.broadcasted_iota(jnp.int32, (n_c, Q), 0) * CMP_STRIDE + (CMP_LEN - 1)
    c_valid = c_end <= qpos
    sc = jnp.where(c_valid, _nt_dot(kc_ref[0], q4), NEG)
    pc = jnp.exp2(sc - jnp.max(sc, axis=0, keepdims=True)) * jnp.where(c_valid, 1.0, 0.0)
    lc = jnp.sum(pc, axis=0, keepdims=True)
    pc = pc * jnp.where(lc > 0.0, 1.0 / lc, 0.0)
    o_cmp = _dot(vct_ref[0], pc.astype(BF16))

    pcs = pc[:, 0:tq] + pc[:, tq:2 * tq] + pc[:, 2 * tq:3 * tq] + pc[:, 3 * tq:]
    hi, lo = _split_bf16(pcs)
    imp = _dot(ovt_ref[...], hi) + _dot(ovt_ref[...], lo)
    nsp = imp.shape[0]
    blk = lax.broadcasted_iota(jnp.int32, (nsp, tq), 0)
    own = lax.shift_right_logical(qpos1, int(np.log2(SEL_LEN)))
    s_valid = blk <= own
    forced = jnp.logical_or(blk == 0, blk >= own - 1)
    imp = jnp.where(s_valid, jnp.where(forced, BIG, imp), NEG)
    rank = _rank_rows(imp, n_sel)
    chosen = jnp.logical_and(rank < float(min(SEL_TOPN, n_sel)), s_valid)
    sel = jnp.where(chosen, 1.0, 0.0).astype(BF16)

    n_win = -(-WIN_LEN // tk) + 1
    s_w, vt_w = [], []
    for a in range(n_win):
        j_a = J - (n_win - 1) + a
        qrel = qpos - j_a * tk
        ok = jnp.broadcast_to(j_a >= 0, (tk, Q))
        if a == 0:
            ok = jnp.logical_and(ok, kpos0 > qrel - WIN_LEN)
        if a == n_win - 1:
            ok = kpos0 <= qrel
        j_c = jnp.maximum(j_a, 0)
        s_w.append(jnp.where(ok, _nt_dot(_tile_rows(kw_ref, j_c, tk), q4), NEG))
        vt_w.append(_tile_lanes(vwt_ref, j_c, tk))
    m_w = functools.reduce(jnp.maximum, [jnp.max(s, axis=0, keepdims=True) for s in s_w])
    p_w = [jnp.exp2(s - m_w) for s in s_w]
    l_w = sum(jnp.sum(p, axis=0, keepdims=True) for p in p_w)
    o_win = sum(_dot(vt, p.astype(BF16)) for vt, p in zip(vt_w, p_w)) / l_w

    def sel_scores(j, diag):
        bias = (_dot(_tile_rows_2d(ext_ref, j, tk), sel) - 1.0) * (-NEG)
        s = _nt_dot(_tile_rows(ks_ref, j, tk), q4) + heads(bias)
        if diag:
            s = jnp.where(j * tk + kpos0 <= qpos, s, NEG)
        return (j, jnp.max(s, axis=0, keepdims=True)), [s]

    def consume(blk_s, st):
        (j, s_max), (s,) = blk_s
        m, l, acc = st
        m_new = jnp.maximum(m, s_max)
        alpha = jnp.exp2(m - m_new)
        p = jnp.exp2(s - m_new)
        l = alpha * l + jnp.sum(p, axis=0, keepdims=True)
        acc = alpha * acc + _dot(_tile_lanes(vst_ref, j, tk), p.astype(BF16))
        return m_new, l, acc

    init = (jnp.full((1, Q), NEG, F32), jnp.zeros((1, Q), F32), jnp.zeros((HEAD_DIM, Q), F32))

    _, l_s, a_s = _pipelined(J, sel_scores(J, True), lambda t: sel_scores(t, False), consume, init,
                             tile_scr, quads=True)
    o_sel = a_s / l_s

    gates = jax.nn.sigmoid(gc_ref[0]).T
    outs = []
    for h in range(N_HEADS):
        cols = slice(h * tq, (h + 1) * tq)
        outs.append(gates[3 * h:3 * h + 1, :] * o_cmp[:, cols]
                    + gates[3 * h + 1:3 * h + 2, :] * o_sel[:, cols]
                    + gates[3 * h + 2:3 * h + 3, :] * o_win[:, cols])
    out = jnp.concatenate(outs, axis=0).T
    o_ref[0] = (out * _silu(z_ref[0].astype(F32))).astype(BF16)


def _nsa(q, kc2, vct, ks2, vst, kw2, vwt, gc, z, tq=256, tk=256):
    B, S, _ = q.shape
    n_c = kc2.shape[1]
    n_sel = S // SEL_LEN
    nsp = -(-n_sel // SUBLANES) * SUBLANES
    n_cmp = (S - CMP_LEN) // CMP_STRIDE + 1
    c_start = np.arange(n_c) * CMP_STRIDE
    s_start = np.arange(nsp) * SEL_LEN
    overlap_t = np.clip(np.minimum(c_start[None, :] + CMP_LEN, s_start[:, None] + SEL_LEN)
                        - np.maximum(c_start[None, :], s_start[:, None]), 0, None) / CMP_LEN
    overlap_t[:, n_cmp:] = 0.0
    overlap_t[n_sel:, :] = 0.0
    expand_t = (np.arange(S)[:, None] // SEL_LEN == np.arange(nsp)[None, :]).astype(np.float32)
    row = lambda w: pl.BlockSpec((1, tq, w), lambda b, i: (b, i, 0))
    full = lambda a: pl.BlockSpec((1,) + a.shape[1:], lambda b, i: (b, 0, 0))
    const = lambda a: pl.BlockSpec(a.shape, lambda b, i: (0, 0))
    ovt = jnp.asarray(overlap_t, BF16)
    ext = jnp.asarray(expand_t, BF16)
    return pl.pallas_call(
        functools.partial(_nsa_kernel, tq=tq, tk=tk, n_sel=n_sel),
        grid=(B, S // tq),
        in_specs=[row(MIX_W), full(kc2), full(vct), full(ks2), full(vst), full(kw2), full(vwt),
                  row(LANES), row(MIX_W), const(ovt), const(ext)],
        out_specs=row(MIX_W),
        out_shape=jax.ShapeDtypeStruct((B, S, MIX_W), BF16),
        scratch_shapes=[pltpu.VMEM((2, 1, tk, N_HEADS * tq), F32)],
        compiler_params=pltpu.CompilerParams(
            dimension_semantics=("parallel", "parallel"), vmem_limit_bytes=VMEM_LIMIT),
        name="native_sparse",
    )(q, kc2, vct, ks2, vst, kw2, vwt, gc, z, ovt, ext)


def _moba_kernel(q_ref, k_ref, vt_ref, z_ref, o_ref, km_scr, bias_scr, tile_scr, *, n_blk, tq):
    i = pl.program_id(1)
    tk = MOBA_BLOCK
    n_diag = tq // tk
    nb_pad = km_scr.shape[0]
    is_a = _head_masks((tq, LANES))

    @pl.when(i == 0)
    def _():
        km_scr[...] = jnp.zeros_like(km_scr)
        for b in range(n_blk):
            kb = k_ref[0, b * tk:(b + 1) * tk, :].astype(F32)
            km_scr[b:b + 1, :] = jnp.mean(kb, axis=0, keepdims=True)

    q = q_ref[0]
    blk = lax.broadcasted_iota(jnp.int32, (nb_pad, 2 * tq), 0)
    qloc = jnp.bitwise_and(lax.broadcasted_iota(jnp.int32, (1, 2 * tq), 1), tq - 1)
    own = i * n_diag + lax.shift_right_logical(qloc, int(np.log2(tk)))
    n_top = min(MOBA_TOPK, max(n_blk - 1, 1))
    qcat = []
    for p in range(N_PAIRS):
        qcat.append(_stack_pair(q[:, _pair_lanes(p)], is_a))
        km_hi, km_lo = _split_bf16(km_scr[:, _pair_lanes(p)])
        gsc = _nt_dot(km_hi, qcat[p]) + _nt_dot(km_lo, qcat[p])
        gsc = jnp.where(blk < own, gsc, NEG)
        rank = _rank_rows(gsc, n_blk)
        attends = jnp.logical_or(jnp.logical_and(rank < float(n_top), blk < own), blk == own)
        bias_scr[p] = jnp.where(attends, 0.0, NEG)

    kloc = lax.broadcasted_iota(jnp.int32, (tk, 2 * tq), 0)

    def scores(j, diag):
        ks = _tile_rows(k_ref, j, tk)
        ss = [_nt_dot(ks[:, _pair_lanes(p)], qcat[p]) for p in range(N_PAIRS)]
        if diag:
            causal = kloc <= qloc - (j * tk - i * tq)
            ss = [jnp.where(causal, s, NEG) for s in ss]
        return (j, [jnp.max(s, axis=0, keepdims=True) for s in ss]), ss

    def consume(blk_s, st):
        (j, maxes), ss = blk_s
        vts = _tile_lanes(vt_ref, j, tk)
        pr = range(N_PAIRS)
        bias = [bias_scr[p, pl.ds(j, 1), :] for p in pr]
        m_new = [jnp.maximum(st[p][0], maxes[p] + bias[p]) for p in pr]
        ps = [jnp.exp2(ss[p] - (m_new[p] - bias[p])) for p in pr]
        pvs = [_pv_pair(vts[_pair_lanes(p)], ps[p].astype(BF16), tq) for p in pr]
        out = []
        for p in pr:
            m, l, acc = st[p]
            alpha = jnp.exp2(m - m_new[p])
            out.append((m_new[p], alpha * l + jnp.sum(ps[p], axis=0, keepdims=True),
                        alpha * acc + pvs[p]))
        return out

    st = [(jnp.full((1, 2 * tq), NEG, F32), jnp.zeros((1, 2 * tq), F32),
           jnp.zeros((HEAD_DIM, 2 * tq), F32))] * N_PAIRS
    for d in range(n_diag - 1, 0, -1):
        st = consume(scores(i * n_diag + d, True), st)
    st = _pipelined(i * n_diag, scores(i * n_diag, True), lambda t: scores(t, False), consume, st,
                    tile_scr, quads=True)
    _finish_pairs([acc / l for _, l, acc in st], tq, z_ref, o_ref)


def _moba(q, k, vt, z, tq=256):
    B, S, _ = q.shape
    tq = min(tq, S)
    n_blk = S // MOBA_BLOCK
    nb_pad = -(-n_blk // SUBLANES) * SUBLANES
    qspec, kspec, vtspec = _all_pair_specs(S, tq)
    return pl.pallas_call(
        functools.partial(_moba_kernel, n_blk=n_blk, tq=tq),
        grid=(B, S // tq),
        in_specs=[qspec, kspec, vtspec, qspec],
        out_specs=qspec,
        out_shape=jax.ShapeDtypeStruct((B, S, MIX_W), BF16),
        scratch_shapes=[pltpu.VMEM((nb_pad, MIX_W), F32),
                        pltpu.VMEM((N_PAIRS, nb_pad, 2 * tq), F32),
                        pltpu.VMEM((2, N_PAIRS, MOBA_BLOCK, 2 * tq), F32)],
        compiler_params=pltpu.CompilerParams(
            dimension_semantics=("parallel", "arbitrary"), vmem_limit_bytes=VMEM_LIMIT),
        name="moba",
    )(q, k, vt, z)


def _out_kernel(ma_ref, mb_ref, mc_ref, md_ref, w_ref, x_ref, mod_ref, g_ref, o_ref):
    D = x_ref.shape[-1]
    acc = _dot(ma_ref[0], w_ref[0:MIX_W, :])
    acc = acc + _dot(mb_ref[0], w_ref[MIX_W:2 * MIX_W, :])
    acc = acc + _dot(mc_ref[0], w_ref[2 * MIX_W:3 * MIX_W, :])
    acc = acc + _dot(md_ref[0], w_ref[3 * MIX_W:, :])
    ms = jnp.mean(acc * acc, axis=-1, keepdims=True)
    y = acc * lax.rsqrt(ms + EPS) * g_ref[...]
    o_ref[0] = x_ref[0] + mod_ref[0, :, 2 * D:] * y


def _out_projection(mixed, w_out, x, mod, g_post, tm):
    B, S, D = x.shape
    row = lambda w: pl.BlockSpec((1, tm, w), lambda b, i: (b, i, 0))
    return pl.pallas_call(
        _out_kernel,
        grid=(B, S // tm),
        in_specs=[row(MIX_W)] * 4 + [pl.BlockSpec(w_out.shape, lambda b, i: (0, 0)), row(D),
                                     pl.BlockSpec((1, 1, 3 * D), lambda b, i: (b, 0, 0)),
                                     pl.BlockSpec((1, D), lambda b, i: (0, 0))],
        out_specs=row(D),
        out_shape=jax.ShapeDtypeStruct((B, S, D), F32),
        compiler_params=pltpu.CompilerParams(
            dimension_semantics=("parallel", "parallel"), vmem_limit_bytes=VMEM_LIMIT),
        name="out_projection",
    )(*mixed, w_out.astype(BF16), x, mod[:, None, :], g_post[None])


def _layer(x, c, layer, norm_pre, norm_post, w_mod_all, b_mod, w_in_all, w_out, cmp_pos, cmp_w1,
           cmp_b1, cmp_w2, tables, tables_c, tm):
    mod = _modulation(c, w_mod_all, layer, b_mod)
    (qa, ka, za, qb, kb, vb, zb, qc, kcr, vcr, gc, ks2, kw2, zc,
     qd, kd, zd, vat, vdt, vst, vwt) = _projection(x, mod, norm_pre, w_in_all, layer, tables, tm)
    oa = _stick_breaking(qa, ka, vat, za)
    ob = _dilated(qb, kb, vb, zb)
    kc2, vct = _compress(kcr, vcr, cmp_pos, cmp_w1, cmp_b1, cmp_w2, tables_c)
    oc = _nsa(qc, kc2, vct, ks2, vst, kw2, vwt, gc, zc)
    od = _moba(qd, kd, vdt, zd)
    return _out_projection((oa, ob, oc, od), w_out, x, mod, norm_post, min(2 * tm, x.shape[1]))


def kernel(x, c, norm_pre, norm_post, w_mod, b_mod, w_in, w_out, cmp_pos, cmp_w1, cmp_b1, cmp_w2):
    S = x.shape[1]
    tables = _rope_lane_tables(np.arange(S))
    tables_c = _rope_lane_tables(np.arange(S // CMP_STRIDE) * CMP_STRIDE + CMP_LEN - 1)
    tm = min(512, S)
    for l in range(norm_pre.shape[0]):
        x = _layer(x, c, l, norm_pre[l], norm_post[l], w_mod, b_mod[l], w_in, w_out[l],
                   cmp_pos[l], cmp_w1[l], cmp_b1[l], cmp_w2[l], tables, tables_c, tm)
    return x
```

```python
import functools

import numpy as np
import jax
import jax.numpy as jnp
from jax import lax
from jax.experimental import pallas as pl
from jax.experimental.pallas import tpu as pltpu

F32 = jnp.float32
BF16 = jnp.bfloat16

N_HEADS = 4
HEAD_DIM = 64
MIX_W = N_HEADS * HEAD_DIM
ROPE_THETA = 500000.0
ROPE_DIM = HEAD_DIM // 4
ROPE_HALF = ROPE_DIM // 2
EPS = 1e-6
NEG = -1e30
BIG = 1e9
SCALE = HEAD_DIM ** -0.5
LOG2E = 1.4426950408889634
SCALE_LOG2 = SCALE * LOG2E
DIL_CONFIGS = ((128, 1), (512, 4), (2048, 16))
CMP_LEN = 32
CMP_STRIDE = 16
CMP_HID = 256
SEL_LEN = 64
SEL_TOPN = 16
WIN_LEN = 512
MOBA_BLOCK = 256
MOBA_TOPK = 3

N_PAIRS = N_HEADS // 2
LANES = 128
SUBLANES = 8
TILE_W = 2 * LANES
KEY_TILE = 256
DIL_GROUP = 2048
VMEM_LIMIT = 56 * 1024 * 1024


def _nt_dot(a, b):
    return lax.dot_general(a, b, (((1,), (1,)), ((), ())), preferred_element_type=F32)


def _dot(a, b):
    return jnp.dot(a, b, preferred_element_type=F32)


def _split_bf16(x):
    hi = x.astype(BF16)
    lo = (x - hi.astype(F32)).astype(BF16)
    return hi, lo


def _silu(x):
    return x * jax.nn.sigmoid(x)


def _head_masks(shape):
    lane = lax.broadcasted_iota(jnp.int32, shape, 1)
    return lane < HEAD_DIM


def _pv_pair(vt, p, tq):
    return jnp.concatenate([_dot(vt[:HEAD_DIM], p[:, :tq]), _dot(vt[HEAD_DIM:], p[:, tq:])], axis=1)


def _mod_kernel(c_ref, w_ref, b_ref, o_ref):
    c = c_ref[...]
    o_ref[...] = jnp.dot(_silu(c), w_ref[...], preferred_element_type=F32,
                         precision=lax.Precision.HIGHEST) + b_ref[...]


def _modulation(c, w_mod_all, layer, b_mod):
    B, D = c.shape
    N = w_mod_all.shape[2]
    rows = SUBLANES
    cp = jnp.zeros((rows, D), F32).at[:B].set(c)
    tn = N // 4
    out = pl.pallas_call(
        _mod_kernel,
        grid=(N // tn,),
        in_specs=[pl.BlockSpec((rows, D), lambda j: (0, 0)),
                  pl.BlockSpec((None, D, tn), lambda j: (layer, 0, j)),
                  pl.BlockSpec((1, tn), lambda j: (0, j))],
        out_specs=pl.BlockSpec((rows, tn), lambda j: (0, j)),
        out_shape=jax.ShapeDtypeStruct((rows, N), F32),
        name="modulation",
    )(cp, w_mod_all, b_mod[None])
    return out[:B]


def _rope_lane_tile(y, cos, sa, sb):
    return y * cos + pltpu.roll(y, LANES - ROPE_HALF, 1) * sa + pltpu.roll(y, ROPE_HALF, 1) * sb


def _pack_weights(win_ref, w_ref, wt_ref):
    D = win_ref.shape[0]
    o = IN_OFFS
    chunk = 256
    for r0 in range(0, D, chunk):
        rs = slice(r0, r0 + chunk)

        def cp(dst, a, b):
            w_ref[rs, dst:dst + (b - a)] = win_ref[rs, a:b].astype(BF16)

        cp(0, o[0], o[11])
        small = PROJ_TILES["c_small"] * TILE_W
        cp(small + LANES, o[15], o[16])
        pad = small + LANES + 3 * N_HEADS
        w_ref[rs, pad:small + TILE_W] = jnp.zeros((chunk, small + TILE_W - pad), BF16)
        keys = PROJ_TILES["c_keys"] * TILE_W
        for rep in range(2):
            cp(keys + rep * HEAD_DIM, o[11], o[12])
            cp(keys + LANES + rep * HEAD_DIM, o[13], o[14])
        cp(PROJ_TILES["zc"] * TILE_W, o[16], o[21])

    def put_t(dst, a, lo=0, n=LANES):
        wt_ref[dst:dst + n, :] = win_ref[:, a:a + LANES].astype(F32).T[lo:lo + n].astype(BF16)

    for h in range(2):
        put_t(h * LANES, o[2] + h * LANES)
        put_t(MIX_W + h * LANES, o[19] + h * LANES)
    put_t(2 * MIX_W, o[11], HEAD_DIM, HEAD_DIM)
    put_t(2 * MIX_W + HEAD_DIM, o[13], HEAD_DIM, HEAD_DIM)


def _proj_kernel(x_ref, mod_ref, g_ref, win_ref, cos_ref, sa_ref, sb_ref,
                 qa_ref, ka_ref, za_ref,
                 qb_ref, kb_ref, vb_ref, zb_ref,
                 qc_ref, kcr_ref, vcr_ref, gc_ref, ks_ref, kw_ref, zc_ref,
                 qd_ref, kd_ref, zd_ref, vat_ref, vdt_ref, vst_ref, vwt_ref, w_ref, wt_ref):
    @pl.when(jnp.logical_and(pl.program_id(0) == 0, pl.program_id(1) == 0))
    def _():
        _pack_weights(win_ref, w_ref, wt_ref)

    D = x_ref.shape[-1]
    x = x_ref[0]
    ms = jnp.mean(x * x, axis=-1, keepdims=True)
    y = x * lax.rsqrt(ms + EPS) * g_ref[...]
    shift = mod_ref[0, :, 0:D]
    scale = mod_ref[0, :, D:2 * D]
    h = (y * (1.0 + scale) + shift).astype(BF16)
    cos, sa, sb = cos_ref[...], sa_ref[...], sb_ref[...]

    def tile(t):
        return _dot(h, w_ref[:, t * TILE_W:(t + 1) * TILE_W])

    def rope_lo(y):
        return _rope_lane_tile(y[:, :LANES], cos, sa, sb)

    def rope_all(y):
        return jnp.concatenate([rope_lo(y), _rope_lane_tile(y[:, LANES:], cos, sa, sb)], axis=1)

    T = PROJ_TILES
    qa_ref[0] = tile(T["qa"]).astype(BF16)
    ka_ref[0] = tile(T["ka"]).astype(BF16)
    za_ref[0] = tile(T["za"]).astype(BF16)
    qb_ref[0] = rope_all(tile(T["qb"]))
    kb_ref[0] = rope_all(tile(T["kb"]))
    vb_ref[0] = tile(T["vb"])
    zb_ref[0] = tile(T["zb"]).astype(BF16)
    qc_ref[0] = (rope_all(tile(T["qc"])) * SCALE_LOG2).astype(BF16)
    small = tile(T["c_small"])
    kcr_ref[0] = small[:, 0:HEAD_DIM]
    vcr_ref[0] = small[:, HEAD_DIM:LANES]
    gc_ref[0] = small[:, LANES:]
    keys = rope_all(tile(T["c_keys"])).astype(BF16)
    ks_ref[0] = keys[:, :LANES]
    kw_ref[0] = keys[:, LANES:]
    zc_ref[0] = tile(T["zc"]).astype(BF16)
    qd_ref[0] = (rope_all(tile(T["qd"])) * SCALE_LOG2).astype(BF16)
    kd_ref[0] = rope_all(tile(T["kd"])).astype(BF16)
    zd_ref[0] = tile(T["zd"]).astype(BF16)
    vt = _nt_dot(wt_ref[...], h).astype(BF16)
    vat_ref[0] = vt[0:MIX_W]
    vdt_ref[0] = vt[MIX_W:2 * MIX_W]
    vst_ref[0] = vt[2 * MIX_W:2 * MIX_W + HEAD_DIM]
    vwt_ref[0] = vt[2 * MIX_W + HEAD_DIM:]


PROJ_TILES = dict(qa=0, ka=1, za=3, qb=4, kb=5, vb=6, zb=7, qc=8, c_small=9, c_keys=10,
                  zc=11, qd=12, kd=13, zd=15)
N_PROJ_TILES = 16
N_VT_ROWS = 2 * MIX_W + 2 * HEAD_DIM
IN_OFFS = [int(v) for v in np.cumsum([0] + [MIX_W] * 9 + [HEAD_DIM] * 6 + [3 * N_HEADS] + [MIX_W] * 5)]


def _rope_lane_tables(pos):
    f32 = np.float32
    inv_freq = (1.0 / (ROPE_THETA ** (np.arange(0, ROPE_DIM, 2, dtype=f32) / ROPE_DIM))).astype(f32)
    ang = np.asarray(pos, f32)[:, None] * inv_freq[None, :]
    cos, sin = np.cos(ang).astype(f32), np.sin(ang).astype(f32)
    n = cos.shape[0]
    rest = HEAD_DIM - ROPE_DIM
    cos_h = np.concatenate([cos, cos, np.ones((n, rest), f32)], axis=1)
    sa_h = np.concatenate([-sin, np.zeros((n, HEAD_DIM - ROPE_HALF), f32)], axis=1)
    sb_h = np.concatenate([np.zeros((n, ROPE_HALF), f32), sin, np.zeros((n, rest), f32)], axis=1)
    two = lambda t: jnp.asarray(np.concatenate([t, t], axis=1))
    return two(cos_h), two(sa_h), two(sb_h)


def _projection(x, mod, g_pre, w_in, layer, tables, tm):
    B, S, D = x.shape
    cos, sa, sb = tables
    row = lambda w: pl.BlockSpec((1, tm, w), lambda b, i: (b, i, 0))
    tab = pl.BlockSpec((tm, LANES), lambda b, i: (i, 0))
    widths = [(MIX_W, BF16)] * 3 + [(MIX_W, F32)] * 3 + [(MIX_W, BF16)] + \
             [(MIX_W, BF16), (HEAD_DIM, F32), (HEAD_DIM, F32), (LANES, F32),
              (LANES, BF16), (LANES, BF16), (MIX_W, BF16)] + \
             [(MIX_W, BF16)] * 3
    t_rows = (MIX_W, MIX_W, HEAD_DIM, HEAD_DIM)
    t_specs = [pl.BlockSpec((1, r, tm), lambda b, i: (b, 0, i)) for r in t_rows]
    t_shapes = [jax.ShapeDtypeStruct((B, r, S), BF16) for r in t_rows]
    return pl.pallas_call(
        _proj_kernel,
        grid=(B, S // tm),
        in_specs=[row(D),
                  pl.BlockSpec((1, 1, 3 * D), lambda b, i: (b, 0, 0)),
                  pl.BlockSpec((1, D), lambda b, i: (0, 0)),
                  pl.BlockSpec((None,) + w_in.shape[1:], lambda b, i: (layer, 0, 0),
                               pipeline_mode=pl.Buffered(1)),
                  tab, tab, tab],
        out_specs=[row(w) for w, _ in widths] + t_specs,
        out_shape=[jax.ShapeDtypeStruct((B, S, w), dt) for w, dt in widths] + t_shapes,
        scratch_shapes=[pltpu.VMEM((D, N_PROJ_TILES * TILE_W), BF16),
                        pltpu.VMEM((N_VT_ROWS, D), BF16)],
        compiler_params=pltpu.CompilerParams(
            dimension_semantics=("arbitrary", "arbitrary"), vmem_limit_bytes=VMEM_LIMIT),
        name="in_projection",
    )(x, mod[:, None, :], g_pre[None], w_in, cos, sa, sb)


def _all_pair_specs(S, tq):
    qspec = pl.BlockSpec((1, tq, MIX_W), lambda b, i: (b, i, 0))
    kspec = pl.BlockSpec((1, S, MIX_W), lambda b, i: (b, 0, 0))
    vtspec = pl.BlockSpec((1, MIX_W, S), lambda b, i: (b, 0, 0))
    return qspec, kspec, vtspec


def _pair_lanes(p):
    return slice(p * LANES, (p + 1) * LANES)


def _tile_rows(ref, j, tk):
    return ref[0, pl.ds(pl.multiple_of(j * tk, tk), tk), :]


def _tile_rows_2d(ref, j, tk):
    return ref[pl.ds(pl.multiple_of(j * tk, tk), tk), :]


def _tile_lanes(ref, j, tk):
    return ref[0, :, pl.ds(pl.multiple_of(j * tk, tk), tk)]


def _stack_pair(q, is_a):
    return jnp.concatenate([jnp.where(is_a, q, 0), jnp.where(is_a, 0, q)], axis=0)


def _rank_rows(v, n):
    rows = v.shape[0]
    rank = jnp.zeros(v.shape, F32)
    for s in range(n):
        c = v[s:s + 1, :]
        lo = s // SUBLANES * SUBLANES
        hi = lo + SUBLANES
        grp = lax.broadcasted_iota(jnp.int32, (SUBLANES, v.shape[1]), 0) + lo
        mixed = jnp.logical_or(c > v[lo:hi], jnp.logical_and(c == v[lo:hi], grp > s))
        parts = [jnp.where(c > v[:lo], 1.0, 0.0)] if lo else []
        parts.append(jnp.where(mixed, 1.0, 0.0))
        if hi < rows:
            parts.append(jnp.where(c >= v[hi:], 1.0, 0.0))
        rank = rank + jnp.concatenate(parts, axis=0)
    return rank


def _pipelined(n, first, produce, consume, state, scr, quads=False):
    def put(slot, tiles):
        for a, tile in enumerate(tiles):
            scr[slot, a] = tile

    def get(slot):
        return [scr[slot, a] for a in range(scr.shape[1])]

    def step(t, slot, carry):
        small, st = carry
        small_new, tiles = produce(t)
        put(1 - slot, tiles)
        return small_new, consume((small, get(slot)), st)

    def pair(t, c):
        return step(t + 1, 1, step(t, 0, c))

    put(0, first[1])
    carry = (first[0], state)
    done = 0
    if quads:
        carry = lax.fori_loop(0, lax.shift_right_logical(n, 2),
                              lambda u, c: pair(4 * u + 2, pair(4 * u, c)), carry)
        done = lax.shift_left(lax.shift_right_logical(n, 2), 2)
    carry = lax.fori_loop(0, lax.shift_right_logical(n - done, 1),
                          lambda u, c: pair(done + 2 * u, c), carry)

    def odd_tail(c):
        small, st = step(n - 1, 0, c)
        return consume((small, get(1)), st)

    return lax.cond(jnp.bitwise_and(n, 1) == 1, odd_tail,
                    lambda c: consume((c[0], get(0)), c[1]), carry)


def _finish_pairs(accs, tq, z_ref, o_ref):
    heads = [a[:, h * tq:(h + 1) * tq] for a in accs for h in range(2)]
    out = jnp.concatenate(heads, axis=0).T
    o_ref[0] = (out * _silu(z_ref[0].astype(F32))).astype(BF16)


def _stick_kernel(q_ref, k_ref, vt_ref, z_ref, o_ref, tile_scr, *, tq):
    i = pl.program_id(1)
    tk = KEY_TILE
    n_diag = tq // tk
    is_a = _head_masks((tq, LANES))
    qs = (q_ref[0].astype(F32) * SCALE).astype(BF16)
    qcat = [_stack_pair(qs[:, _pair_lanes(p)], is_a) for p in range(N_PAIRS)]
    r_i = lax.broadcasted_iota(jnp.int32, (tk, tk), 0)
    c_i = lax.broadcasted_iota(jnp.int32, (tk, tk), 1)
    suffix = jnp.where(c_i >= r_i, 1.0, 0.0).astype(BF16)
    kloc = lax.broadcasted_iota(jnp.int32, (tk, 2 * tq), 0)
    qloc = jnp.bitwise_and(lax.broadcasted_iota(jnp.int32, (1, 2 * tq), 1), tq - 1)

    def log_weights(j, diag):
        ks = _tile_rows(k_ref, j, tk)
        zs = [_nt_dot(ks[:, _pair_lanes(p)], qcat[p]) for p in range(N_PAIRS)]
        sps = [jnp.maximum(z, 0.0) + jnp.log(1.0 + jnp.exp2(jnp.abs(z) * (-LOG2E))) for z in zs]
        if diag:
            past = kloc < qloc - (j * tk - i * tq)
            sps = [jnp.where(past, sp, 0.0) for sp in sps]
        css = [_dot(suffix, sp.astype(BF16)) for sp in sps]
        xs = [z - cs for z, cs in zip(zs, css)]
        if diag:
            xs = [jnp.where(past, x, NEG) for x in xs]
        return (j, [cs[0:1, :] for cs in css]), xs

    def accumulate(blk, st):
        (j, tots), xs = blk
        vts = _tile_lanes(vt_ref, j, tk)
        ws = [jnp.exp(xs[p] - st[p][1]).astype(BF16) for p in range(N_PAIRS)]
        return [(st[p][0] + _pv_pair(vts[_pair_lanes(p)], ws[p], tq), st[p][1] + tots[p])
                for p in range(N_PAIRS)]

    def upper_diag(d, st):
        lo, nq = d * tk, tq - d * tk
        j = i * n_diag + d
        ks, vts = _tile_rows(k_ref, j, tk), _tile_lanes(vt_ref, j, tk)
        r = lax.broadcasted_iota(jnp.int32, (1, nq), 1)
        past = lax.broadcasted_iota(jnp.int32, (tk, 2 * nq), 0) < jnp.concatenate([r, r], axis=1)
        sub = lambda a: jnp.concatenate([a[:, lo:tq], a[:, tq + lo:]], axis=1)
        add = lambda a, u: jnp.concatenate([a[:, :lo], a[:, lo:tq] + u[:, :nq],
                                            a[:, tq:tq + lo], a[:, tq + lo:] + u[:, nq:]], axis=1)
        out = []
        for p in range(N_PAIRS):
            acc, car = st[p]
            z = _nt_dot(ks[:, _pair_lanes(p)],
                        _stack_pair(qs[lo:, _pair_lanes(p)], _head_masks((nq, LANES))))
            sp = jnp.maximum(z, 0.0) + jnp.log(1.0 + jnp.exp2(jnp.abs(z) * (-LOG2E)))
            cs = _dot(suffix, jnp.where(past, sp, 0.0).astype(BF16))
            w = jnp.exp(jnp.where(past, z - cs, NEG) - sub(car)).astype(BF16)
            out.append((add(acc, _pv_pair(vts[_pair_lanes(p)], w, nq)), add(car, cs[0:1, :])))
        return out

    st = [(jnp.zeros((HEAD_DIM, 2 * tq), F32), jnp.zeros((1, 2 * tq), F32))] * N_PAIRS
    for d in range(n_diag - 1, 0, -1):
        st = upper_diag(d, st)
    st = _pipelined(i * n_diag, log_weights(i * n_diag, True),
                    lambda t: log_weights(i * n_diag - 1 - t, False), accumulate, st, tile_scr)
    _finish_pairs([acc for acc, _ in st], tq, z_ref, o_ref)


def _stick_breaking(q, k, vt, z, tq=512):
    B, S, _ = q.shape
    tq = min(tq, S)
    qspec, kspec, vtspec = _all_pair_specs(S, tq)
    return pl.pallas_call(
        functools.partial(_stick_kernel, tq=tq),
        grid=(B, S // tq),
        in_specs=[qspec, kspec, vtspec, qspec],
        out_specs=qspec,
        out_shape=jax.ShapeDtypeStruct((B, S, MIX_W), BF16),
        scratch_shapes=[pltpu.VMEM((2, N_PAIRS, KEY_TILE, 2 * tq), F32)],
        compiler_params=pltpu.CompilerParams(
            dimension_semantics=("parallel", "arbitrary"), vmem_limit_bytes=VMEM_LIMIT),
        name="stick_breaking",
    )(q, k, vt, z)


def _dil_kernel(q_ref, k_ref, v_ref, z_ref, o_ref, m_scr, l_scr, a_scr, *, G):
    g = pl.program_id(2)
    blk = LANES
    is_a = _head_masks((blk, LANES))
    qrow = jnp.bitwise_and(lax.broadcasted_iota(jnp.int32, (2 * blk, 2 * blk), 0), blk - 1)
    col = lax.broadcasted_iota(jnp.int32, (2 * blk, 2 * blk), 1)
    band = jnp.logical_and(col >= qrow, col <= qrow + blk)
    n_sub = G // blk
    unroll = 16

    for ci, (_, d) in enumerate(DIL_CONFIGS):
        nb = G // (blk * d)
        sh = int(np.log2(nb))

        def sub(n, ci=ci, d=d, nb=nb, sh=sh):
            r = lax.shift_right_logical(n, sh)
            ub = jnp.bitwise_and(n, nb - 1)
            loc = ub * (blk * d) + r
            glob = g * G + loc
            has_prev = glob >= blk * d
            pstart = jnp.maximum(glob - blk * d, r)
            qs = (q_ref[0, pl.ds(loc, blk, stride=d), :] * SCALE_LOG2).astype(BF16)
            kk = jnp.concatenate([k_ref[0, pl.ds(pstart, blk, stride=d), :],
                                  k_ref[0, pl.ds(glob, blk, stride=d), :]], axis=0).astype(BF16)
            vv = jnp.concatenate([v_ref[0, pl.ds(pstart, blk, stride=d), :],
                                  v_ref[0, pl.ds(glob, blk, stride=d), :]], axis=0).astype(BF16)
            ok = jnp.logical_and(band, jnp.logical_or(col >= blk, has_prev))
            s = jnp.where(ok, _nt_dot(_stack_pair(qs, is_a), kk), NEG)
            m = jnp.max(s, axis=-1, keepdims=True)
            p = jnp.exp2(s - m)
            l = jnp.sum(p, axis=-1, keepdims=True)
            acc = _dot(p.astype(BF16), vv)
            rows = pl.ds(loc, blk, stride=d)
            m_scr[ci, rows, :] = jnp.where(is_a, m[:blk], m[blk:])
            l_scr[ci, rows, :] = jnp.where(is_a, l[:blk], l[blk:])
            a_scr[ci, rows, :] = jnp.where(is_a, acc[:blk], acc[blk:])

        def trip(t, _, sub=sub):
            for u in range(unroll):
                sub(t * unroll + u)
            return 0

        lax.fori_loop(0, n_sub // unroll, trip, 0)

    chunk = 256

    def combine(c, _):
        rows = pl.ds(pl.multiple_of(c * chunk, chunk), chunk)
        m0, m1, m2 = m_scr[0, rows, :], m_scr[1, rows, :], m_scr[2, rows, :]
        mm = jnp.maximum(jnp.maximum(m0, m1), m2)
        w0, w1, w2 = jnp.exp2(m0 - mm), jnp.exp2(m1 - mm), jnp.exp2(m2 - mm)
        num = w0 * a_scr[0, rows, :] + w1 * a_scr[1, rows, :] + w2 * a_scr[2, rows, :]
        den = w0 * l_scr[0, rows, :] + w1 * l_scr[1, rows, :] + w2 * l_scr[2, rows, :]
        o_ref[0, rows, :] = (num / den * _silu(z_ref[0, rows, :].astype(F32))).astype(BF16)
        return 0

    lax.fori_loop(0, G // chunk, combine, 0)


def _dilated(q, k, v, z):
    B, S, _ = q.shape
    G = min(DIL_GROUP, S)
    gspec = pl.BlockSpec((1, G, LANES), lambda b, p, g: (b, g, p))
    kspec = pl.BlockSpec((1, S, LANES), lambda b, p, g: (b, 0, p))
    scr = pltpu.VMEM((len(DIL_CONFIGS), G, LANES), F32)
    return pl.pallas_call(
        functools.partial(_dil_kernel, G=G),
        grid=(B, 2, S // G),
        in_specs=[gspec, kspec, kspec, gspec],
        out_specs=gspec,
        out_shape=jax.ShapeDtypeStruct((B, S, MIX_W), BF16),
        scratch_shapes=[scr, scr, scr],
        compiler_params=pltpu.CompilerParams(
            dimension_semantics=("parallel", "parallel", "arbitrary"), vmem_limit_bytes=VMEM_LIMIT),
        name="dilated_window",
    )(q, k, v, z)


def _compress_kernel(k_ref, v_ref, p_ref, w1_ref, b1_ref, w2k_ref, w2vt_ref, cos_ref, sa_ref, sb_ref,
                     kc_ref, vc_ref):
    n = kc_ref.shape[1]
    dh = k_ref.shape[-1]
    for s, (t_ref, o_ref) in enumerate(((k_ref, kc_ref), (v_ref, vc_ref))):
        lo = jnp.zeros((n, CMP_HID), F32)
        hi = jnp.zeros((n, CMP_HID), F32)
        for l in range(CMP_STRIDE):
            t = t_ref[0, pl.ds(l, n, stride=CMP_STRIDE), :]
            lo = lo + _dot((t + p_ref[s, l:l + 1, :]).astype(BF16),
                           w1_ref[s, l * dh:(l + 1) * dh, :])
            lh = l + CMP_STRIDE
            hi = hi + _dot((t + p_ref[s, lh:lh + 1, :]).astype(BF16),
                           w1_ref[s, lh * dh:(lh + 1) * dh, :])
        pre = lo + pltpu.roll(hi, n - 1, 0) + b1_ref[s]
        hid = 0.5 * pre * (1.0 + jnp.tanh(np.sqrt(2.0 / np.pi).astype(np.float32)
                                          * (pre + 0.044715 * (pre * pre * pre))))
        hid = hid.astype(BF16)
        if s == 0:
            out = _dot(hid, w2k_ref[...])
            o_ref[0] = _rope_lane_tile(out, cos_ref[...], sa_ref[...], sb_ref[...]).astype(BF16)
        else:
            o_ref[0] = _nt_dot(w2vt_ref[...], hid).astype(BF16)


def _compress(kcr, vcr, cmp_pos, cmp_w1, cmp_b1, cmp_w2, tables_c):
    B, S, dh = kcr.shape
    n = S // CMP_STRIDE
    w1 = cmp_w1.astype(BF16)
    w2k = jnp.concatenate([cmp_w2[0], cmp_w2[0]], axis=-1).astype(BF16)
    w2vt = cmp_w2[1].T.astype(BF16)
    b1 = cmp_b1[:, None, :]
    cos, sa, sb = tables_c
    const = lambda a: pl.BlockSpec(a.shape, lambda b: (0,) * a.ndim)
    tspec = pl.BlockSpec((1, S, dh), lambda b: (b, 0, 0))
    return pl.pallas_call(
        _compress_kernel,
        grid=(B,),
        in_specs=[tspec, tspec, const(cmp_pos), const(w1), const(b1), const(w2k), const(w2vt),
                  const(cos), const(sa), const(sb)],
        out_specs=[pl.BlockSpec((1, n, LANES), lambda b: (b, 0, 0)),
                   pl.BlockSpec((1, dh, n), lambda b: (b, 0, 0))],
        out_shape=[jax.ShapeDtypeStruct((B, n, LANES), BF16),
                   jax.ShapeDtypeStruct((B, dh, n), BF16)],
        compiler_params=pltpu.CompilerParams(
            dimension_semantics=("parallel",), vmem_limit_bytes=VMEM_LIMIT),
        name="compress_tokens",
    )(kcr, vcr, cmp_pos, w1, b1, w2k, w2vt, cos, sa, sb)


def _nsa_kernel(q_ref, kc_ref, vct_ref, ks_ref, vst_ref, kw_ref, vwt_ref, gc_ref, z_ref,
                ovt_ref, ext_ref, o_ref, tile_scr, *, tq, tk, n_sel):
    i = pl.program_id(1)
    Q = N_HEADS * tq
    is_a = _head_masks((tq, LANES))
    qs = q_ref[0]
    q4 = jnp.concatenate([_stack_pair(qs[:, :LANES], is_a), _stack_pair(qs[:, LANES:], is_a)], axis=0)
    heads = lambda x: jnp.concatenate([x] * N_HEADS, axis=1)
    qpos1 = i * tq + lax.broadcasted_iota(jnp.int32, (1, tq), 1)
    qpos = heads(qpos1)

    kpos0 = lax.broadcasted_iota(jnp.int32, (tk, Q), 0)
    J = (i * tq) // tk

    n_c = kc_ref.shape[1]
    c_end = lax.broadcasted_iota(jnp.int32, (n_c, Q), 0) * CMP_STRIDE + (CMP_LEN - 1)
    c_valid = c_end <= qpos
    sc = jnp.where(c_valid, _nt_dot(kc_ref[0], q4), NEG)
    pc = jnp.exp2(sc - jnp.max(sc, axis=0, keepdims=True)) * jnp.where(c_valid, 1.0, 0.0)
    lc = jnp.sum(pc, axis=0, keepdims=True)
    pc = pc * jnp.where(lc > 0.0, 1.0 / lc, 0.0)
    o_cmp = _dot(vct_ref[0], pc.astype(BF16))

    pcs = pc[:, 0:tq] + pc[:, tq:2 * tq] + pc[:, 2 * tq:3 * tq] + pc[:, 3 * tq:]
    hi, lo = _split_bf16(pcs)
    imp = _dot(ovt_ref[...], hi) + _dot(ovt_ref[...], lo)
    nsp = imp.shape[0]
    blk = lax.broadcasted_iota(jnp.int32, (nsp, tq), 0)
    own = lax.shift_right_logical(qpos1, int(np.log2(SEL_LEN)))
    s_valid = blk <= own
    forced = jnp.logical_or(blk == 0, blk >= own - 1)
    imp = jnp.where(s_valid, jnp.where(forced, BIG, imp), NEG)
    rank = _rank_rows(imp, n_sel)
    chosen = jnp.logical_and(rank < float(min(SEL_TOPN, n_sel)), s_valid)
    sel = jnp.where(chosen, 1.0, 0.0).astype(BF16)

    n_win = -(-WIN_LEN // tk) + 1
    s_w, vt_w = [], []
    for a in range(n_win):
        j_a = J - (n_win - 1) + a
        qrel = qpos - j_a * tk
        ok = jnp.broadcast_to(j_a >= 0, (tk, Q))
        if a == 0:
            ok = jnp.logical_and(ok, kpos0 > qrel - WIN_LEN)
        if a == n_win - 1:
            ok = kpos0 <= qrel
        j_c = jnp.maximum(j_a, 0)
        s_w.append(jnp.where(ok, _nt_dot(_tile_rows(kw_ref, j_c, tk), q4), NEG))
        vt_w.append(_tile_lanes(vwt_ref, j_c, tk))
    m_w = functools.reduce(jnp.maximum, [jnp.max(s, axis=0, keepdims=True) for s in s_w])
    p_w = [jnp.exp2(s - m_w) for s in s_w]
    l_w = sum(jnp.sum(p, axis=0, keepdims=True) for p in p_w)
    o_win = sum(_dot(vt, p.astype(BF16)) for vt, p in zip(vt_w, p_w)) / l_w

    def sel_scores(j, diag):
        bias = (_dot(_tile_rows_2d(ext_ref, j, tk), sel) - 1.0) * (-NEG)
        s = _nt_dot(_tile_rows(ks_ref, j, tk), q4) + heads(bias)
        if diag:
            s = jnp.where(j * tk + kpos0 <= qpos, s, NEG)
        return (j, jnp.max(s, axis=0, keepdims=True)), [s]

    def consume(blk_s, st):
        (j, s_max), (s,) = blk_s
        m, l, acc = st
        m_new = jnp.maximum(m, s_max)
        alpha = jnp.exp2(m - m_new)
        p = jnp.exp2(s - m_new)
        l = alpha * l + jnp.sum(p, axis=0, keepdims=True)
        acc = alpha * acc + _dot(_tile_lanes(vst_ref, j, tk), p.astype(BF16))
        return m_new, l, acc

    init = (jnp.full((1, Q), NEG, F32), jnp.zeros((1, Q), F32), jnp.zeros((HEAD_DIM, Q), F32))

    _, l_s, a_s = _pipelined(J, sel_scores(J, True), lambda t: sel_scores(t, False), consume, init,
                             tile_scr, quads=True)
    o_sel = a_s / l_s

    gates = jax.nn.sigmoid(gc_ref[0]).T
    outs = []
    for h in range(N_HEADS):
        cols = slice(h * tq, (h + 1) * tq)
        outs.append(gates[3 * h:3 * h + 1, :] * o_cmp[:, cols]
                    + gates[3 * h + 1:3 * h + 2, :] * o_sel[:, cols]
                    + gates[3 * h + 2:3 * h + 3, :] * o_win[:, cols])
    out = jnp.concatenate(outs, axis=0).T
    o_ref[0] = (out * _silu(z_ref[0].astype(F32))).astype(BF16)


def _nsa(q, kc2, vct, ks2, vst, kw2, vwt, gc, z, tq=256, tk=256):
    B, S, _ = q.shape
    n_c = kc2.shape[1]
    n_sel = S // SEL_LEN
    nsp = -(-n_sel // SUBLANES) * SUBLANES
    n_cmp = (S - CMP_LEN) // CMP_STRIDE + 1
    c_start = np.arange(n_c) * CMP_STRIDE
    s_start = np.arange(nsp) * SEL_LEN
    overlap_t = np.clip(np.minimum(c_start[None, :] + CMP_LEN, s_start[:, None] + SEL_LEN)
                        - np.maximum(c_start[None, :], s_start[:, None]), 0, None) / CMP_LEN
    overlap_t[:, n_cmp:] = 0.0
    overlap_t[n_sel:, :] = 0.0
    expand_t = (np.arange(S)[:, None] // SEL_LEN == np.arange(nsp)[None, :]).astype(np.float32)
    row = lambda w: pl.BlockSpec((1, tq, w), lambda b, i: (b, i, 0))
    full = lambda a: pl.BlockSpec((1,) + a.shape[1:], lambda b, i: (b, 0, 0))
    const = lambda a: pl.BlockSpec(a.shape, lambda b, i: (0, 0))
    ovt = jnp.asarray(overlap_t, BF16)
    ext = jnp.asarray(expand_t, BF16)
    return pl.pallas_call(
        functools.partial(_nsa_kernel, tq=tq, tk=tk, n_sel=n_sel),
        grid=(B, S // tq),
        in_specs=[row(MIX_W), full(kc2), full(vct), full(ks2), full(vst), full(kw2), full(vwt),
                  row(LANES), row(MIX_W), const(ovt), const(ext)],
        out_specs=row(MIX_W),
        out_shape=jax.ShapeDtypeStruct((B, S, MIX_W), BF16),
        scratch_shapes=[pltpu.VMEM((2, 1, tk, N_HEADS * tq), F32)],
        compiler_params=pltpu.CompilerParams(
            dimension_semantics=("parallel", "parallel"), vmem_limit_bytes=VMEM_LIMIT),
        name="native_sparse",
    )(q, kc2, vct, ks2, vst, kw2, vwt, gc, z, ovt, ext)


def _moba_kernel(q_ref, k_ref, vt_ref, z_ref, o_ref, km_scr, bias_scr, tile_scr, *, n_blk, tq):
    i = pl.program_id(1)
    tk = MOBA_BLOCK
    n_diag = tq // tk
    nb_pad = km_scr.shape[0]
    is_a = _head_masks((tq, LANES))

    @pl.when(i == 0)
    def _():
        km_scr[...] = jnp.zeros_like(km_scr)
        for b in range(n_blk):
            kb = k_ref[0, b * tk:(b + 1) * tk, :].astype(F32)
            km_scr[b:b + 1, :] = jnp.mean(kb, axis=0, keepdims=True)

    q = q_ref[0]
    blk = lax.broadcasted_iota(jnp.int32, (nb_pad, 2 * tq), 0)
    qloc = jnp.bitwise_and(lax.broadcasted_iota(jnp.int32, (1, 2 * tq), 1), tq - 1)
    own = i * n_diag + lax.shift_right_logical(qloc, int(np.log2(tk)))
    n_top = min(MOBA_TOPK, max(n_blk - 1, 1))
    qcat = []
    for p in range(N_PAIRS):
        qcat.append(_stack_pair(q[:, _pair_lanes(p)], is_a))
        km_hi, km_lo = _split_bf16(km_scr[:, _pair_lanes(p)])
        gsc = _nt_dot(km_hi, qcat[p]) + _nt_dot(km_lo, qcat[p])
        gsc = jnp.where(blk < own, gsc, NEG)
        rank = _rank_rows(gsc, n_blk)
        attends = jnp.logical_or(jnp.logical_and(rank < float(n_top), blk < own), blk == own)
        bias_scr[p] = jnp.where(attends, 0.0, NEG)

    kloc = lax.broadcasted_iota(jnp.int32, (tk, 2 * tq), 0)

    def scores(j, diag):
        ks = _tile_rows(k_ref, j, tk)
        ss = [_nt_dot(ks[:, _pair_lanes(p)], qcat[p]) for p in range(N_PAIRS)]
        if diag:
            causal = kloc <= qloc - (j * tk - i * tq)
            ss = [jnp.where(causal, s, NEG) for s in ss]
        return (j, [jnp.max(s, axis=0, keepdims=True) for s in ss]), ss

    def consume(blk_s, st):
        (j, maxes), ss = blk_s
        vts = _tile_lanes(vt_ref, j, tk)
        pr = range(N_PAIRS)
        bias = [bias_scr[p, pl.ds(j, 1), :] for p in pr]
        m_new = [jnp.maximum(st[p][0], maxes[p] + bias[p]) for p in pr]
        ps = [jnp.exp2(ss[p] - (m_new[p] - bias[p])) for p in pr]
        pvs = [_pv_pair(vts[_pair_lanes(p)], ps[p].astype(BF16), tq) for p in pr]
        out = []
        for p in pr:
            m, l, acc = st[p]
            alpha = jnp.exp2(m - m_new[p])
            out.append((m_new[p], alpha * l + jnp.sum(ps[p], axis=0, keepdims=True),
                        alpha * acc + pvs[p]))
        return out

    st = [(jnp.full((1, 2 * tq), NEG, F32), jnp.zeros((1, 2 * tq), F32),
           jnp.zeros((HEAD_DIM, 2 * tq), F32))] * N_PAIRS
    for d in range(n_diag - 1, 0, -1):
        st = consume(scores(i * n_diag + d, True), st)
    st = _pipelined(i * n_diag, scores(i * n_diag, True), lambda t: scores(t, False), consume, st,
                    tile_scr, quads=True)
    _finish_pairs([acc / l for _, l, acc in st], tq, z_ref, o_ref)


def _moba(q, k, vt, z, tq=256):
    B, S, _ = q.shape
    tq = min(tq, S)
    n_blk = S // MOBA_BLOCK
    nb_pad = -(-n_blk // SUBLANES) * SUBLANES
    qspec, kspec, vtspec = _all_pair_specs(S, tq)
    return pl.pallas_call(
        functools.partial(_moba_kernel, n_blk=n_blk, tq=tq),
        grid=(B, S // tq),
        in_specs=[qspec, kspec, vtspec, qspec],
        out_specs=qspec,
        out_shape=jax.ShapeDtypeStruct((B, S, MIX_W), BF16),
        scratch_shapes=[pltpu.VMEM((nb_pad, MIX_W), F32),
                        pltpu.VMEM((N_PAIRS, nb_pad, 2 * tq), F32),
                        pltpu.VMEM((2, N_PAIRS, MOBA_BLOCK, 2 * tq), F32)],
        compiler_params=pltpu.CompilerParams(
            dimension_semantics=("parallel", "arbitrary"), vmem_limit_bytes=VMEM_LIMIT),
        name="moba",
    )(q, k, vt, z)


def _out_kernel(ma_ref, mb_ref, mc_ref, md_ref, w_ref, x_ref, mod_ref, g_ref, o_ref):
    D = x_ref.shape[-1]
    acc = _dot(ma_ref[0], w_ref[0:MIX_W, :])
    acc = acc + _dot(mb_ref[0], w_ref[MIX_W:2 * MIX_W, :])
    acc = acc + _dot(mc_ref[0], w_ref[2 * MIX_W:3 * MIX_W, :])
    acc = acc + _dot(md_ref[0], w_ref[3 * MIX_W:, :])
    ms = jnp.mean(acc * acc, axis=-1, keepdims=True)
    y = acc * lax.rsqrt(ms + EPS) * g_ref[...]
    o_ref[0] = x_ref[0] + mod_ref[0, :, 2 * D:] * y


def _out_projection(mixed, w_out, x, mod, g_post, tm):
    B, S, D = x.shape
    row = lambda w: pl.BlockSpec((1, tm, w), lambda b, i: (b, i, 0))
    return pl.pallas_call(
        _out_kernel,
        grid=(B, S // tm),
        in_specs=[row(MIX_W)] * 4 + [pl.BlockSpec(w_out.shape, lambda b, i: (0, 0)), row(D),
                                     pl.BlockSpec((1, 1, 3 * D), lambda b, i: (b, 0, 0)),
                                     pl.BlockSpec((1, D), lambda b, i: (0, 0))],
        out_specs=row(D),
        out_shape=jax.ShapeDtypeStruct((B, S, D), F32),
        compiler_params=pltpu.CompilerParams(
            dimension_semantics=("parallel", "parallel"), vmem_limit_bytes=VMEM_LIMIT),
        name="out_projection",
    )(*mixed, w_out.astype(BF16), x, mod[:, None, :], g_post[None])


def _layer(x, c, layer, norm_pre, norm_post, w_mod_all, b_mod, w_in_all, w_out, cmp_pos, cmp_w1,
           cmp_b1, cmp_w2, tables, tables_c, tm):
    mod = _modulation(c, w_mod_all, layer, b_mod)
    (qa, ka, za, qb, kb, vb, zb, qc, kcr, vcr, gc, ks2, kw2, zc,
     qd, kd, zd, vat, vdt, vst, vwt) = _projection(x, mod, norm_pre, w_in_all, layer, tables, tm)
    oa = _stick_breaking(qa, ka, vat, za)
    ob = _dilated(qb, kb, vb, zb)
    kc2, vct = _compress(kcr, vcr, cmp_pos, cmp_w1, cmp_b1, cmp_w2, tables_c)
    oc = _nsa(qc, kc2, vct, ks2, vst, kw2, vwt, gc, zc)
    od = _moba(qd, kd, vdt, zd)
    return _out_projection((oa, ob, oc, od), w_out, x, mod, norm_post, min(2 * tm, x.shape[1]))


def kernel(x, c, norm_pre, norm_post, w_mod, b_mod, w_in, w_out, cmp_pos, cmp_w1, cmp_b1, cmp_w2):
    S = x.shape[1]
    tables = _rope_lane_tables(np.arange(S))
    tables_c = _rope_lane_tables(np.arange(S // CMP_STRIDE) * CMP_STRIDE + CMP_LEN - 1)
    tm = min(512, S)
    for l in range(norm_pre.shape[0]):
        x = _layer(x, c, l, norm_pre[l], norm_post[l], w_mod, b_mod[l], w_in, w_out[l],
                   cmp_pos[l], cmp_w1[l], cmp_b1[l], cmp_w2[l], tables, tables_c, tm)
    return x
```

```python
import functools

import numpy as np
import jax
import jax.numpy as jnp
from jax import lax
from jax.experimental import pallas as pl
from jax.experimental.pallas import tpu as pltpu

F32 = jnp.float32
BF16 = jnp.bfloat16

N_HEADS = 4
HEAD_DIM = 64
MIX_W = N_HEADS * HEAD_DIM
ROPE_THETA = 500000.0
ROPE_DIM = HEAD_DIM // 4
ROPE_HALF = ROPE_DIM // 2
EPS = 1e-6
NEG = -1e30
BIG = 1e9
SCALE = HEAD_DIM ** -0.5
LOG2E = 1.4426950408889634
SCALE_LOG2 = SCALE * LOG2E
DIL_CONFIGS = ((128, 1), (512, 4), (2048, 16))
CMP_LEN = 32
CMP_STRIDE = 16
CMP_HID = 256
SEL_LEN = 64
SEL_TOPN = 16
WIN_LEN = 512
MOBA_BLOCK = 256
MOBA_TOPK = 3

N_PAIRS = N_HEADS // 2
LANES = 128
SUBLANES = 8
TILE_W = 2 * LANES
KEY_TILE = 256
DIL_GROUP = 2048
VMEM_LIMIT = 56 * 1024 * 1024


def _nt_dot(a, b):
    return lax.dot_general(a, b, (((1,), (1,)), ((), ())), preferred_element_type=F32)


def _dot(a, b):
    return jnp.dot(a, b, preferred_element_type=F32)


def _split_bf16(x):
    hi = x.astype(BF16)
    lo = (x - hi.astype(F32)).astype(BF16)
    return hi, lo


def _suffix_sums(suffix, x):
    hi, lo = _split_bf16(x)
    return _dot(suffix, hi) + _dot(suffix, lo)


def _silu(x):
    return x * jax.nn.sigmoid(x)


def _head_masks(shape):
    lane = lax.broadcasted_iota(jnp.int32, shape, 1)
    return lane < HEAD_DIM


def _pv_pair(vt, p, tq):
    return jnp.concatenate([_dot(vt[:HEAD_DIM], p[:, :tq]), _dot(vt[HEAD_DIM:], p[:, tq:])], axis=1)


def _mod_kernel(c_ref, w_ref, b_ref, o_ref):
    c = c_ref[...]
    o_ref[...] = jnp.dot(_silu(c), w_ref[...], preferred_element_type=F32,
                         precision=lax.Precision.HIGHEST) + b_ref[...]


def _modulation(c, w_mod_all, layer, b_mod):
    B, D = c.shape
    N = w_mod_all.shape[2]
    rows = SUBLANES
    cp = jnp.zeros((rows, D), F32).at[:B].set(c)
    tn = N // 4
    out = pl.pallas_call(
        _mod_kernel,
        grid=(N // tn,),
        in_specs=[pl.BlockSpec((rows, D), lambda j: (0, 0)),
                  pl.BlockSpec((None, D, tn), lambda j: (layer, 0, j)),
                  pl.BlockSpec((1, tn), lambda j: (0, j))],
        out_specs=pl.BlockSpec((rows, tn), lambda j: (0, j)),
        out_shape=jax.ShapeDtypeStruct((rows, N), F32),
        name="modulation",
    )(cp, w_mod_all, b_mod[None])
    return out[:B]


def _rope_lane_tile(y, cos, sa, sb):
    return y * cos + pltpu.roll(y, LANES - ROPE_HALF, 1) * sa + pltpu.roll(y, ROPE_HALF, 1) * sb


def _pack_weights(win_ref, w_ref, wt_ref):
    D = win_ref.shape[0]
    o = IN_OFFS
    chunk = 256
    for r0 in range(0, D, chunk):
        rs = slice(r0, r0 + chunk)

        def cp(dst, a, b):
            w_ref[rs, dst:dst + (b - a)] = win_ref[rs, a:b].astype(BF16)

        cp(0, o[0], o[11])
        small = PROJ_TILES["c_small"] * TILE_W
        cp(small + LANES, o[15], o[16])
        pad = small + LANES + 3 * N_HEADS
        w_ref[rs, pad:small + TILE_W] = jnp.zeros((chunk, small + TILE_W - pad), BF16)
        keys = PROJ_TILES["c_keys"] * TILE_W
        for rep in range(2):
            cp(keys + rep * HEAD_DIM, o[11], o[12])
            cp(keys + LANES + rep * HEAD_DIM, o[13], o[14])
        cp(PROJ_TILES["zc"] * TILE_W, o[16], o[21])

    def put_t(dst, a, lo=0, n=LANES):
        wt_ref[dst:dst + n, :] = win_ref[:, a:a + LANES].astype(F32).T[lo:lo + n].astype(BF16)

    for h in range(2):
        put_t(h * LANES, o[2] + h * LANES)
        put_t(MIX_W + h * LANES, o[19] + h * LANES)
    put_t(2 * MIX_W, o[11], HEAD_DIM, HEAD_DIM)
    put_t(2 * MIX_W + HEAD_DIM, o[13], HEAD_DIM, HEAD_DIM)


def _proj_kernel(x_ref, mod_ref, g_ref, win_ref, cos_ref, sa_ref, sb_ref,
                 qa_ref, ka_ref, za_ref,
                 qb_ref, kb_ref, vb_ref, zb_ref,
                 qc_ref, kcr_ref, vcr_ref, gc_ref, ks_ref, kw_ref, zc_ref,
                 qd_ref, kd_ref, zd_ref, vat_ref, vdt_ref, vst_ref, vwt_ref, w_ref, wt_ref):
    @pl.when(jnp.logical_and(pl.program_id(0) == 0, pl.program_id(1) == 0))
    def _():
        _pack_weights(win_ref, w_ref, wt_ref)

    D = x_ref.shape[-1]
    x = x_ref[0]
    ms = jnp.mean(x * x, axis=-1, keepdims=True)
    y = x * lax.rsqrt(ms + EPS) * g_ref[...]
    shift = mod_ref[0, :, 0:D]
    scale = mod_ref[0, :, D:2 * D]
    h = (y * (1.0 + scale) + shift).astype(BF16)
    cos, sa, sb = cos_ref[...], sa_ref[...], sb_ref[...]

    def tile(t):
        return _dot(h, w_ref[:, t * TILE_W:(t + 1) * TILE_W])

    def rope_lo(y):
        return _rope_lane_tile(y[:, :LANES], cos, sa, sb)

    def rope_all(y):
        return jnp.concatenate([rope_lo(y), _rope_lane_tile(y[:, LANES:], cos, sa, sb)], axis=1)

    T = PROJ_TILES
    qa_ref[0] = tile(T["qa"]).astype(BF16)
    ka_ref[0] = tile(T["ka"]).astype(BF16)
    za_ref[0] = tile(T["za"]).astype(BF16)
    qb_ref[0] = rope_all(tile(T["qb"]))
    kb_ref[0] = rope_all(tile(T["kb"]))
    vb_ref[0] = tile(T["vb"])
    zb_ref[0] = tile(T["zb"]).astype(BF16)
    qc_ref[0] = (rope_all(tile(T["qc"])) * SCALE_LOG2).astype(BF16)
    small = tile(T["c_small"])
    kcr_ref[0] = small[:, 0:HEAD_DIM]
    vcr_ref[0] = small[:, HEAD_DIM:LANES]
    gc_ref[0] = small[:, LANES:]
    keys = rope_all(tile(T["c_keys"])).astype(BF16)
    ks_ref[0] = keys[:, :LANES]
    kw_ref[0] = keys[:, LANES:]
    zc_ref[0] = tile(T["zc"]).astype(BF16)
    qd_ref[0] = (rope_all(tile(T["qd"])) * SCALE_LOG2).astype(BF16)
    kd_ref[0] = rope_all(tile(T["kd"])).astype(BF16)
    zd_ref[0] = tile(T["zd"]).astype(BF16)
    vt = _nt_dot(wt_ref[...], h).astype(BF16)
    vat_ref[0] = vt[0:MIX_W]
    vdt_ref[0] = vt[MIX_W:2 * MIX_W]
    vst_ref[0] = vt[2 * MIX_W:2 * MIX_W + HEAD_DIM]
    vwt_ref[0] = vt[2 * MIX_W + HEAD_DIM:]


PROJ_TILES = dict(qa=0, ka=1, za=3, qb=4, kb=5, vb=6, zb=7, qc=8, c_small=9, c_keys=10,
                  zc=11, qd=12, kd=13, zd=15)
N_PROJ_TILES = 16
N_VT_ROWS = 2 * MIX_W + 2 * HEAD_DIM
IN_OFFS = [int(v) for v in np.cumsum([0] + [MIX_W] * 9 + [HEAD_DIM] * 6 + [3 * N_HEADS] + [MIX_W] * 5)]


def _rope_lane_tables(pos):
    f32 = np.float32
    inv_freq = (1.0 / (ROPE_THETA ** (np.arange(0, ROPE_DIM, 2, dtype=f32) / ROPE_DIM))).astype(f32)
    ang = np.asarray(pos, f32)[:, None] * inv_freq[None, :]
    cos, sin = np.cos(ang).astype(f32), np.sin(ang).astype(f32)
    n = cos.shape[0]
    rest = HEAD_DIM - ROPE_DIM
    cos_h = np.concatenate([cos, cos, np.ones((n, rest), f32)], axis=1)
    sa_h = np.concatenate([-sin, np.zeros((n, HEAD_DIM - ROPE_HALF), f32)], axis=1)
    sb_h = np.concatenate([np.zeros((n, ROPE_HALF), f32), sin, np.zeros((n, rest), f32)], axis=1)
    two = lambda t: jnp.asarray(np.concatenate([t, t], axis=1))
    return two(cos_h), two(sa_h), two(sb_h)


def _projection(x, mod, g_pre, w_in, layer, tables, tm):
    B, S, D = x.shape
    cos, sa, sb = tables
    row = lambda w: pl.BlockSpec((1, tm, w), lambda b, i: (b, i, 0))
    tab = pl.BlockSpec((tm, LANES), lambda b, i: (i, 0))
    widths = [(MIX_W, BF16)] * 3 + [(MIX_W, F32)] * 3 + [(MIX_W, BF16)] + \
             [(MIX_W, BF16), (HEAD_DIM, F32), (HEAD_DIM, F32), (LANES, F32),
              (LANES, BF16), (LANES, BF16), (MIX_W, BF16)] + \
             [(MIX_W, BF16)] * 3
    t_rows = (MIX_W, MIX_W, HEAD_DIM, HEAD_DIM)
    t_specs = [pl.BlockSpec((1, r, tm), lambda b, i: (b, 0, i)) for r in t_rows]
    t_shapes = [jax.ShapeDtypeStruct((B, r, S), BF16) for r in t_rows]
    return pl.pallas_call(
        _proj_kernel,
        grid=(B, S // tm),
        in_specs=[row(D),
                  pl.BlockSpec((1, 1, 3 * D), lambda b, i: (b, 0, 0)),
                  pl.BlockSpec((1, D), lambda b, i: (0, 0)),
                  pl.BlockSpec((None,) + w_in.shape[1:], lambda b, i: (layer, 0, 0),
                               pipeline_mode=pl.Buffered(1)),
                  tab, tab, tab],
        out_specs=[row(w) for w, _ in widths] + t_specs,
        out_shape=[jax.ShapeDtypeStruct((B, S, w), dt) for w, dt in widths] + t_shapes,
        scratch_shapes=[pltpu.VMEM((D, N_PROJ_TILES * TILE_W), BF16),
                        pltpu.VMEM((N_VT_ROWS, D), BF16)],
        compiler_params=pltpu.CompilerParams(
            dimension_semantics=("arbitrary", "arbitrary"), vmem_limit_bytes=VMEM_LIMIT),
        name="in_projection",
    )(x, mod[:, None, :], g_pre[None], w_in, cos, sa, sb)


def _all_pair_specs(S, tq):
    qspec = pl.BlockSpec((1, tq, MIX_W), lambda b, i: (b, i, 0))
    kspec = pl.BlockSpec((1, S, MIX_W), lambda b, i: (b, 0, 0))
    vtspec = pl.BlockSpec((1, MIX_W, S), lambda b, i: (b, 0, 0))
    return qspec, kspec, vtspec


def _pair_lanes(p):
    return slice(p * LANES, (p + 1) * LANES)


def _tile_rows(ref, j, tk):
    return ref[0, pl.ds(pl.multiple_of(j * tk, tk), tk), :]


def _tile_rows_2d(ref, j, tk):
    return ref[pl.ds(pl.multiple_of(j * tk, tk), tk), :]


def _tile_lanes(ref, j, tk):
    return ref[0, :, pl.ds(pl.multiple_of(j * tk, tk), tk)]


def _stack_pair(q, is_a):
    return jnp.concatenate([jnp.where(is_a, q, 0), jnp.where(is_a, 0, q)], axis=0)


def _rank_rows(v, n):
    rows = v.shape[0]
    rank = jnp.zeros(v.shape, F32)
    for s in range(n):
        c = v[s:s + 1, :]
        lo = s // SUBLANES * SUBLANES
        hi = lo + SUBLANES
        grp = lax.broadcasted_iota(jnp.int32, (SUBLANES, v.shape[1]), 0) + lo
        mixed = jnp.logical_or(c > v[lo:hi], jnp.logical_and(c == v[lo:hi], grp > s))
        parts = [jnp.where(c > v[:lo], 1.0, 0.0)] if lo else []
        parts.append(jnp.where(mixed, 1.0, 0.0))
        if hi < rows:
            parts.append(jnp.where(c >= v[hi:], 1.0, 0.0))
        rank = rank + jnp.concatenate(parts, axis=0)
    return rank


def _pipelined(n, first, produce, consume, state, scr, quads=False):
    def put(slot, tiles):
        for a, tile in enumerate(tiles):
            scr[slot, a] = tile

    def get(slot):
        return [scr[slot, a] for a in range(scr.shape[1])]

    def step(t, slot, carry):
        small, st = carry
        small_new, tiles = produce(t)
        put(1 - slot, tiles)
        return small_new, consume((small, get(slot)), st)

    def pair(t, c):
        return step(t + 1, 1, step(t, 0, c))

    put(0, first[1])
    carry = (first[0], state)
    done = 0
    if quads:
        carry = lax.fori_loop(0, lax.shift_right_logical(n, 2),
                              lambda u, c: pair(4 * u + 2, pair(4 * u, c)), carry)
        done = lax.shift_left(lax.shift_right_logical(n, 2), 2)
    carry = lax.fori_loop(0, lax.shift_right_logical(n - done, 1),
                          lambda u, c: pair(done + 2 * u, c), carry)

    def odd_tail(c):
        small, st = step(n - 1, 0, c)
        return consume((small, get(1)), st)

    return lax.cond(jnp.bitwise_and(n, 1) == 1, odd_tail,
                    lambda c: consume((c[0], get(0)), c[1]), carry)


def _finish_pairs(accs, tq, z_ref, o_ref):
    heads = [a[:, h * tq:(h + 1) * tq] for a in accs for h in range(2)]
    out = jnp.concatenate(heads, axis=0).T
    o_ref[0] = (out * _silu(z_ref[0].astype(F32))).astype(BF16)


def _stick_kernel(q_ref, k_ref, vt_ref, z_ref, o_ref, tile_scr, *, tq):
    i = pl.program_id(1)
    tk = KEY_TILE
    n_diag = tq // tk
    is_a = _head_masks((tq, LANES))
    qs = (q_ref[0].astype(F32) * SCALE).astype(BF16)
    qcat = [_stack_pair(qs[:, _pair_lanes(p)], is_a) for p in range(N_PAIRS)]
    r_i = lax.broadcasted_iota(jnp.int32, (tk, tk), 0)
    c_i = lax.broadcasted_iota(jnp.int32, (tk, tk), 1)
    suffix = jnp.where(c_i >= r_i, 1.0, 0.0).astype(BF16)
    kloc = lax.broadcasted_iota(jnp.int32, (tk, 2 * tq), 0)
    qloc = jnp.bitwise_and(lax.broadcasted_iota(jnp.int32, (1, 2 * tq), 1), tq - 1)

    def log_weights(j, diag):
        ks = _tile_rows(k_ref, j, tk)
        zs = [_nt_dot(ks[:, _pair_lanes(p)], qcat[p]) for p in range(N_PAIRS)]
        sps = [jnp.maximum(z, 0.0) + jnp.log(1.0 + jnp.exp2(jnp.abs(z) * (-LOG2E))) for z in zs]
        if diag:
            past = kloc < qloc - (j * tk - i * tq)
            sps = [jnp.where(past, sp, 0.0) for sp in sps]
        css = [_suffix_sums(suffix, sp) for sp in sps]
        xs = [z - cs for z, cs in zip(zs, css)]
        if diag:
            xs = [jnp.where(past, x, NEG) for x in xs]
        return (j, [cs[0:1, :] for cs in css]), xs

    def accumulate(blk, st):
        (j, tots), xs = blk
        vts = _tile_lanes(vt_ref, j, tk)
        ws = [jnp.exp(xs[p] - st[p][1]).astype(BF16) for p in range(N_PAIRS)]
        return [(st[p][0] + _pv_pair(vts[_pair_lanes(p)], ws[p], tq), st[p][1] + tots[p])
                for p in range(N_PAIRS)]

    def upper_diag(d, st):
        lo, nq = d * tk, tq - d * tk
        j = i * n_diag + d
        ks, vts = _tile_rows(k_ref, j, tk), _tile_lanes(vt_ref, j, tk)
        r = lax.broadcasted_iota(jnp.int32, (1, nq), 1)
        past = lax.broadcasted_iota(jnp.int32, (tk, 2 * nq), 0) < jnp.concatenate([r, r], axis=1)
        sub = lambda a: jnp.concatenate([a[:, lo:tq], a[:, tq + lo:]], axis=1)
        add = lambda a, u: jnp.concatenate([a[:, :lo], a[:, lo:tq] + u[:, :nq],
                                            a[:, tq:tq + lo], a[:, tq + lo:] + u[:, nq:]], axis=1)
        out = []
        for p in range(N_PAIRS):
            acc, car = st[p]
            z = _nt_dot(ks[:, _pair_lanes(p)],
                        _stack_pair(qs[lo:, _pair_lanes(p)], _head_masks((nq, LANES))))
            sp = jnp.maximum(z, 0.0) + jnp.log(1.0 + jnp.exp2(jnp.abs(z) * (-LOG2E)))
            cs = _suffix_sums(suffix, jnp.where(past, sp, 0.0))
            w = jnp.exp(jnp.where(past, z - cs, NEG) - sub(car)).astype(BF16)
            out.append((add(acc, _pv_pair(vts[_pair_lanes(p)], w, nq)), add(car, cs[0:1, :])))
        return out

    st = [(jnp.zeros((HEAD_DIM, 2 * tq), F32), jnp.zeros((1, 2 * tq), F32))] * N_PAIRS
    for d in range(n_diag - 1, 0, -1):
        st = upper_diag(d, st)
    st = _pipelined(i * n_diag, log_weights(i * n_diag, True),
                    lambda t: log_weights(i * n_diag - 1 - t, False), accumulate, st, tile_scr)
    _finish_pairs([acc for acc, _ in st], tq, z_ref, o_ref)


def _stick_breaking(q, k, vt, z, tq=512):
    B, S, _ = q.shape
    tq = min(tq, S)
    qspec, kspec, vtspec = _all_pair_specs(S, tq)
    return pl.pallas_call(
        functools.partial(_stick_kernel, tq=tq),
        grid=(B, S // tq),
        in_specs=[qspec, kspec, vtspec, qspec],
        out_specs=qspec,
        out_shape=jax.ShapeDtypeStruct((B, S, MIX_W), BF16),
        scratch_shapes=[pltpu.VMEM((2, N_PAIRS, KEY_TILE, 2 * tq), F32)],
        compiler_params=pltpu.CompilerParams(
            dimension_semantics=("parallel", "arbitrary"), vmem_limit_bytes=VMEM_LIMIT),
        name="stick_breaking",
    )(q, k, vt, z)


def _dil_kernel(q_ref, k_ref, v_ref, z_ref, o_ref, m_scr, l_scr, a_scr, *, G):
    g = pl.program_id(2)
    blk = LANES
    is_a = _head_masks((blk, LANES))
    qrow = jnp.bitwise_and(lax.broadcasted_iota(jnp.int32, (2 * blk, 2 * blk), 0), blk - 1)
    col = lax.broadcasted_iota(jnp.int32, (2 * blk, 2 * blk), 1)
    band = jnp.logical_and(col >= qrow, col <= qrow + blk)
    n_sub = G // blk
    unroll = 16

    for ci, (_, d) in enumerate(DIL_CONFIGS):
        nb = G // (blk * d)
        sh = int(np.log2(nb))

        def sub(n, ci=ci, d=d, nb=nb, sh=sh):
            r = lax.shift_right_logical(n, sh)
            ub = jnp.bitwise_and(n, nb - 1)
            loc = ub * (blk * d) + r
            glob = g * G + loc
            has_prev = glob >= blk * d
            pstart = jnp.maximum(glob - blk * d, r)
            qs = (q_ref[0, pl.ds(loc, blk, stride=d), :] * SCALE_LOG2).astype(BF16)
            kk = jnp.concatenate([k_ref[0, pl.ds(pstart, blk, stride=d), :],
                                  k_ref[0, pl.ds(glob, blk, stride=d), :]], axis=0).astype(BF16)
            vv = jnp.concatenate([v_ref[0, pl.ds(pstart, blk, stride=d), :],
                                  v_ref[0, pl.ds(glob, blk, stride=d), :]], axis=0).astype(BF16)
            ok = jnp.logical_and(band, jnp.logical_or(col >= blk, has_prev))
            s = jnp.where(ok, _nt_dot(_stack_pair(qs, is_a), kk), NEG)
            m = jnp.max(s, axis=-1, keepdims=True)
            p = jnp.exp2(s - m)
            l = jnp.sum(p, axis=-1, keepdims=True)
            acc = _dot(p.astype(BF16), vv)
            rows = pl.ds(loc, blk, stride=d)
            m_scr[ci, rows, :] = jnp.where(is_a, m[:blk], m[blk:])
            l_scr[ci, rows, :] = jnp.where(is_a, l[:blk], l[blk:])
            a_scr[ci, rows, :] = jnp.where(is_a, acc[:blk], acc[blk:])

        def trip(t, _, sub=sub):
            for u in range(unroll):
                sub(t * unroll + u)
            return 0

        lax.fori_loop(0, n_sub // unroll, trip, 0)

    chunk = 256

    def combine(c, _):
        rows = pl.ds(pl.multiple_of(c * chunk, chunk), chunk)
        m0, m1, m2 = m_scr[0, rows, :], m_scr[1, rows, :], m_scr[2, rows, :]
        mm = jnp.maximum(jnp.maximum(m0, m1), m2)
        w0, w1, w2 = jnp.exp2(m0 - mm), jnp.exp2(m1 - mm), jnp.exp2(m2 - mm)
        num = w0 * a_scr[0, rows, :] + w1 * a_scr[1, rows, :] + w2 * a_scr[2, rows, :]
        den = w0 * l_scr[0, rows, :] + w1 * l_scr[1, rows, :] + w2 * l_scr[2, rows, :]
        o_ref[0, rows, :] = (num / den * _silu(z_ref[0, rows, :].astype(F32))).astype(BF16)
        return 0

    lax.fori_loop(0, G // chunk, combine, 0)


def _dilated(q, k, v, z):
    B, S, _ = q.shape
    G = min(DIL_GROUP, S)
    gspec = pl.BlockSpec((1, G, LANES), lambda b, p, g: (b, g, p))
    kspec = pl.BlockSpec((1, S, LANES), lambda b, p, g: (b, 0, p))
    scr = pltpu.VMEM((len(DIL_CONFIGS), G, LANES), F32)
    return pl.pallas_call(
        functools.partial(_dil_kernel, G=G),
        grid=(B, 2, S // G),
        in_specs=[gspec, kspec, kspec, gspec],
        out_specs=gspec,
        out_shape=jax.ShapeDtypeStruct((B, S, MIX_W), BF16),
        scratch_shapes=[scr, scr, scr],
        compiler_params=pltpu.CompilerParams(
            dimension_semantics=("parallel", "parallel", "arbitrary"), vmem_limit_bytes=VMEM_LIMIT),
        name="dilated_window",
    )(q, k, v, z)


def _compress_kernel(k_ref, v_ref, p_ref, w1_ref, b1_ref, w2k_ref, w2vt_ref, cos_ref, sa_ref, sb_ref,
                     kc_ref, vc_ref):
    n = kc_ref.shape[1]
    dh = k_ref.shape[-1]
    for s, (t_ref, o_ref) in enumerate(((k_ref, kc_ref), (v_ref, vc_ref))):
        lo = jnp.zeros((n, CMP_HID), F32)
        hi = jnp.zeros((n, CMP_HID), F32)
        for l in range(CMP_STRIDE):
            t = t_ref[0, pl.ds(l, n, stride=CMP_STRIDE), :]
            lo = lo + _dot((t + p_ref[s, l:l + 1, :]).astype(BF16),
                           w1_ref[s, l * dh:(l + 1) * dh, :])
            lh = l + CMP_STRIDE
            hi = hi + _dot((t + p_ref[s, lh:lh + 1, :]).astype(BF16),
                           w1_ref[s, lh * dh:(lh + 1) * dh, :])
        pre = lo + pltpu.roll(hi, n - 1, 0) + b1_ref[s]
        hid = 0.5 * pre * (1.0 + jnp.tanh(np.sqrt(2.0 / np.pi).astype(np.float32)
                                          * (pre + 0.044715 * (pre * pre * pre))))
        hid = hid.astype(BF16)
        if s == 0:
            out = _dot(hid, w2k_ref[...])
            o_ref[0] = _rope_lane_tile(out, cos_ref[...], sa_ref[...], sb_ref[...]).astype(BF16)
        else:
            o_ref[0] = _nt_dot(w2vt_ref[...], hid).astype(BF16)


def _compress(kcr, vcr, cmp_pos, cmp_w1, cmp_b1, cmp_w2, tables_c):
    B, S, dh = kcr.shape
    n = S // CMP_STRIDE
    w1 = cmp_w1.astype(BF16)
    w2k = jnp.concatenate([cmp_w2[0], cmp_w2[0]], axis=-1).astype(BF16)
    w2vt = cmp_w2[1].T.astype(BF16)
    b1 = cmp_b1[:, None, :]
    cos, sa, sb = tables_c
    const = lambda a: pl.BlockSpec(a.shape, lambda b: (0,) * a.ndim)
    tspec = pl.BlockSpec((1, S, dh), lambda b: (b, 0, 0))
    return pl.pallas_call(
        _compress_kernel,
        grid=(B,),
        in_specs=[tspec, tspec, const(cmp_pos), const(w1), const(b1), const(w2k), const(w2vt),
                  const(cos), const(sa), const(sb)],
        out_specs=[pl.BlockSpec((1, n, LANES), lambda b: (b, 0, 0)),
                   pl.BlockSpec((1, dh, n), lambda b: (b, 0, 0))],
        out_shape=[jax.ShapeDtypeStruct((B, n, LANES), BF16),
                   jax.ShapeDtypeStruct((B, dh, n), BF16)],
        compiler_params=pltpu.CompilerParams(
            dimension_semantics=("parallel",), vmem_limit_bytes=VMEM_LIMIT),
        name="compress_tokens",
    )(kcr, vcr, cmp_pos, w1, b1, w2k, w2vt, cos, sa, sb)


def _nsa_kernel(q_ref, kc_ref, vct_ref, ks_ref, vst_ref, kw_ref, vwt_ref, gc_ref, z_ref,
                ovt_ref, ext_ref, o_ref, tile_scr, *, tq, tk, n_sel):
    i = pl.program_id(1)
    Q = N_HEADS * tq
    is_a = _head_masks((tq, LANES))
    qs = q_ref[0]
    q4 = jnp.concatenate([_stack_pair(qs[:, :LANES], is_a), _stack_pair(qs[:, LANES:], is_a)], axis=0)
    heads = lambda x: jnp.concatenate([x] * N_HEADS, axis=1)
    qpos1 = i * tq + lax.broadcasted_iota(jnp.int32, (1, tq), 1)
    qpos = heads(qpos1)

    kpos0 = lax.broadcasted_iota(jnp.int32, (tk, Q), 0)
    J = (i * tq) // tk

    n_c = kc_ref.shape[1]
    c_end = lax.broadcasted_iota(jnp.int32, (n_c, Q), 0) * CMP_STRIDE + (CMP_LEN - 1)
    c_valid = c_end <= qpos
    sc = jnp.where(c_valid, _nt_dot(kc_ref[0], q4), NEG)
    pc = jnp.exp2(sc - jnp.max(sc, axis=0, keepdims=True)) * jnp.where(c_valid, 1.0, 0.0)
    lc = jnp.sum(pc, axis=0, keepdims=True)
    pc = pc * jnp.where(lc > 0.0, 1.0 / lc, 0.0)
    o_cmp = _dot(vct_ref[0], pc.astype(BF16))

    pcs = pc[:, 0:tq] + pc[:, tq:2 * tq] + pc[:, 2 * tq:3 * tq] + pc[:, 3 * tq:]
    hi, lo = _split_bf16(pcs)
    imp = _dot(ovt_ref[...], hi) + _dot(ovt_ref[...], lo)
    nsp = imp.shape[0]
    blk = lax.broadcasted_iota(jnp.int32, (nsp, tq), 0)
    own = lax.shift_right_logical(qpos1, int(np.log2(SEL_LEN)))
    s_valid = blk <= own
    forced = jnp.logical_or(blk == 0, blk >= own - 1)
    imp = jnp.where(s_valid, jnp.where(forced, BIG, imp), NEG)
    rank = _rank_rows(imp, n_sel)
    chosen = jnp.logical_and(rank < float(min(SEL_TOPN, n_sel)), s_valid)
    sel = jnp.where(chosen, 1.0, 0.0).astype(BF16)

    n_win = -(-WIN_LEN // tk) + 1
    s_w, vt_w = [], []
    for a in range(n_win):
        j_a = J - (n_win - 1) + a
        qrel = qpos - j_a * tk
        ok = jnp.broadcast_to(j_a >= 0, (tk, Q))
        if a == 0:
            ok = jnp.logical_and(ok, kpos0 > qrel - WIN_LEN)
        if a == n_win - 1:
            ok = kpos0 <= qrel
        j_c = jnp.maximum(j_a, 0)
        s_w.append(jnp.where(ok, _nt_dot(_tile_rows(kw_ref, j_c, tk), q4), NEG))
        vt_w.append(_tile_lanes(vwt_ref, j_c, tk))
    m_w = functools.reduce(jnp.maximum, [jnp.max(s, axis=0, keepdims=True) for s in s_w])
    p_w = [jnp.exp2(s - m_w) for s in s_w]
    l_w = sum(jnp.sum(p, axis=0, keepdims=True) for p in p_w)
    o_win = sum(_dot(vt, p.astype(BF16)) for vt, p in zip(vt_w, p_w)) / l_w

    def sel_scores(j, diag):
        bias = (_dot(_tile_rows_2d(ext_ref, j, tk), sel) - 1.0) * (-NEG)
        s = _nt_dot(_tile_rows(ks_ref, j, tk), q4) + heads(bias)
        if diag:
            s = jnp.where(j * tk + kpos0 <= qpos, s, NEG)
        return (j, jnp.max(s, axis=0, keepdims=True)), [s]

    def consume(blk_s, st):
        (j, s_max), (s,) = blk_s
        m, l, acc = st
        m_new = jnp.maximum(m, s_max)
        alpha = jnp.exp2(m - m_new)
        p = jnp.exp2(s - m_new)
        l = alpha * l + jnp.sum(p, axis=0, keepdims=True)
        acc = alpha * acc + _dot(_tile_lanes(vst_ref, j, tk), p.astype(BF16))
        return m_new, l, acc

    init = (jnp.full((1, Q), NEG, F32), jnp.zeros((1, Q), F32), jnp.zeros((HEAD_DIM, Q), F32))

    _, l_s, a_s = _pipelined(J, sel_scores(J, True), lambda t: sel_scores(t, False), consume, init,
                             tile_scr, quads=True)
    o_sel = a_s / l_s

    gates = jax.nn.sigmoid(gc_ref[0]).T
    outs = []
    for h in range(N_HEADS):
        cols = slice(h * tq, (h + 1) * tq)
        outs.append(gates[3 * h:3 * h + 1, :] * o_cmp[:, cols]
                    + gates[3 * h + 1:3 * h + 2, :] * o_sel[:, cols]
                    + gates[3 * h + 2:3 * h + 3, :] * o_win[:, cols])
    out = jnp.concatenate(outs, axis=0).T
    o_ref[0] = (out * _silu(z_ref[0].astype(F32))).astype(BF16)


def _nsa(q, kc2, vct, ks2, vst, kw2, vwt, gc, z, tq=256, tk=256):
    B, S, _ = q.shape
    n_c = kc2.shape[1]
    n_sel = S // SEL_LEN
    nsp = -(-n_sel // SUBLANES) * SUBLANES
    n_cmp = (S - CMP_LEN) // CMP_STRIDE + 1
    c_start = np.arange(n_c) * CMP_STRIDE
    s_start = np.arange(nsp) * SEL_LEN
    overlap_t = np.clip(np.minimum(c_start[None, :] + CMP_LEN, s_start[:, None] + SEL_LEN)
                        - np.maximum(c_start[None, :], s_start[:, None]), 0, None) / CMP_LEN
    overlap_t[:, n_cmp:] = 0.0
    overlap_t[n_sel:, :] = 0.0
    expand_t = (np.arange(S)[:, None] // SEL_LEN == np.arange(nsp)[None, :]).astype(np.float32)
    row = lambda w: pl.BlockSpec((1, tq, w), lambda b, i: (b, i, 0))
    full = lambda a: pl.BlockSpec((1,) + a.shape[1:], lambda b, i: (b, 0, 0))
    const = lambda a: pl.BlockSpec(a.shape, lambda b, i: (0, 0))
    ovt = jnp.asarray(overlap_t, BF16)
    ext = jnp.asarray(expand_t, BF16)
    return pl.pallas_call(
        functools.partial(_nsa_kernel, tq=tq, tk=tk, n_sel=n_sel),
        grid=(B, S // tq),
        in_specs=[row(MIX_W), full(kc2), full(vct), full(ks2), full(vst), full(kw2), full(vwt),
                  row(LANES), row(MIX_W), const(ovt), const(ext)],
        out_specs=row(MIX_W),
        out_shape=jax.ShapeDtypeStruct((B, S, MIX_W), BF16),
        scratch_shapes=[pltpu.VMEM((2, 1, tk, N_HEADS * tq), F32)],
        compiler_params=pltpu.CompilerParams(
            dimension_semantics=("parallel", "parallel"), vmem_limit_bytes=VMEM_LIMIT),
        name="native_sparse",
    )(q, kc2, vct, ks2, vst, kw2, vwt, gc, z, ovt, ext)


def _moba_kernel(q_ref, k_ref, vt_ref, z_ref, o_ref, km_scr, bias_scr, tile_scr, *, n_blk, tq):
    i = pl.program_id(1)
    tk = MOBA_BLOCK
    n_diag = tq // tk
    nb_pad = km_scr.shape[0]
    is_a = _head_masks((tq, LANES))

    @pl.when(i == 0)
    def _():
        km_scr[...] = jnp.zeros_like(km_scr)
        for b in range(n_blk):
            kb = k_ref[0, b * tk:(b + 1) * tk, :].astype(F32)
            km_scr[b:b + 1, :] = jnp.mean(kb, axis=0, keepdims=True)

    q = q_ref[0]
    blk = lax.broadcasted_iota(jnp.int32, (nb_pad, 2 * tq), 0)
    qloc = jnp.bitwise_and(lax.broadcasted_iota(jnp.int32, (1, 2 * tq), 1), tq - 1)
    own = i * n_diag + lax.shift_right_logical(qloc, int(np.log2(tk)))
    n_top = min(MOBA_TOPK, max(n_blk - 1, 1))
    qcat = []
    for p in range(N_PAIRS):
        qcat.append(_stack_pair(q[:, _pair_lanes(p)], is_a))
        km_hi, km_lo = _split_bf16(km_scr[:, _pair_lanes(p)])
        gsc = _nt_dot(km_hi, qcat[p]) + _nt_dot(km_lo, qcat[p])
        gsc = jnp.where(blk < own, gsc, NEG)
        rank = _rank_rows(gsc, n_blk)
        attends = jnp.logical_or(jnp.logical_and(rank < float(n_top), blk < own), blk == own)
        bias_scr[p] = jnp.where(attends, 0.0, NEG)

    kloc = lax.broadcasted_iota(jnp.int32, (tk, 2 * tq), 0)

    def scores(j, diag):
        ks = _tile_rows(k_ref, j, tk)
        ss = [_nt_dot(ks[:, _pair_lanes(p)], qcat[p]) for p in range(N_PAIRS)]
        if diag:
            causal = kloc <= qloc - (j * tk - i * tq)
            ss = [jnp.where(causal, s, NEG) for s in ss]
        return (j, [jnp.max(s, axis=0, keepdims=True) for s in ss]), ss

    def consume(blk_s, st):
        (j, maxes), ss = blk_s
        vts = _tile_lanes(vt_ref, j, tk)
        pr = range(N_PAIRS)
        bias = [bias_scr[p, pl.ds(j, 1), :] for p in pr]
        m_new = [jnp.maximum(st[p][0], maxes[p] + bias[p]) for p in pr]
        ps = [jnp.exp2(ss[p] - (m_new[p] - bias[p])) for p in pr]
        pvs = [_pv_pair(vts[_pair_lanes(p)], ps[p].astype(BF16), tq) for p in pr]
        out = []
        for p in pr:
            m, l, acc = st[p]
            alpha = jnp.exp2(m - m_new[p])
            out.append((m_new[p], alpha * l + jnp.sum(ps[p], axis=0, keepdims=True),
                        alpha * acc + pvs[p]))
        return out

    st = [(jnp.full((1, 2 * tq), NEG, F32), jnp.zeros((1, 2 * tq), F32),
           jnp.zeros((HEAD_DIM, 2 * tq), F32))] * N_PAIRS
    for d in range(n_diag - 1, 0, -1):
        st = consume(scores(i * n_diag + d, True), st)
    st = _pipelined(i * n_diag, scores(i * n_diag, True), lambda t: scores(t, False), consume, st,
                    tile_scr, quads=True)
    _finish_pairs([acc / l for _, l, acc in st], tq, z_ref, o_ref)


def _moba(q, k, vt, z, tq=256):
    B, S, _ = q.shape
    tq = min(tq, S)
    n_blk = S // MOBA_BLOCK
    nb_pad = -(-n_blk // SUBLANES) * SUBLANES
    qspec, kspec, vtspec = _all_pair_specs(S, tq)
    return pl.pallas_call(
        functools.partial(_moba_kernel, n_blk=n_blk, tq=tq),
        grid=(B, S // tq),
        in_specs=[qspec, kspec, vtspec, qspec],
        out_specs=qspec,
        out_shape=jax.ShapeDtypeStruct((B, S, MIX_W), BF16),
        scratch_shapes=[pltpu.VMEM((nb_pad, MIX_W), F32),
                        pltpu.VMEM((N_PAIRS, nb_pad, 2 * tq), F32),
                        pltpu.VMEM((2, N_PAIRS, MOBA_BLOCK, 2 * tq), F32)],
        compiler_params=pltpu.CompilerParams(
            dimension_semantics=("parallel", "arbitrary"), vmem_limit_bytes=VMEM_LIMIT),
        name="moba",
    )(q, k, vt, z)


def _out_kernel(ma_ref, mb_ref, mc_ref, md_ref, w_ref, x_ref, mod_ref, g_ref, o_ref):
    D = x_ref.shape[-1]
    acc = _dot(ma_ref[0], w_ref[0:MIX_W, :])
    acc = acc + _dot(mb_ref[0], w_ref[MIX_W:2 * MIX_W, :])
    acc = acc + _dot(mc_ref[0], w_ref[2 * MIX_W:3 * MIX_W, :])
    acc = acc + _dot(md_ref[0], w_ref[3 * MIX_W:, :])
    ms = jnp.mean(acc * acc, axis=-1, keepdims=True)
    y = acc * lax.rsqrt(ms + EPS) * g_ref[...]
    o_ref[0] = x_ref[0] + mod_ref[0, :, 2 * D:] * y


def _out_projection(mixed, w_out, x, mod, g_post, tm):
    B, S, D = x.shape
    row = lambda w: pl.BlockSpec((1, tm, w), lambda b, i: (b, i, 0))
    return pl.pallas_call(
        _out_kernel,
        grid=(B, S // tm),
        in_specs=[row(MIX_W)] * 4 + [pl.BlockSpec(w_out.shape, lambda b, i: (0, 0)), row(D),
                                     pl.BlockSpec((1, 1, 3 * D), lambda b, i: (b, 0, 0)),
                                     pl.BlockSpec((1, D), lambda b, i: (0, 0))],
        out_specs=row(D),
        out_shape=jax.ShapeDtypeStruct((B, S, D), F32),
        compiler_params=pltpu.CompilerParams(
            dimension_semantics=("parallel", "parallel"), vmem_limit_bytes=VMEM_LIMIT),
        name="out_projection",
    )(*mixed, w_out.astype(BF16), x, mod[:, None, :], g_post[None])


def _layer(x, c, layer, norm_pre, norm_post, w_mod_all, b_mod, w_in_all, w_out, cmp_pos, cmp_w1,
           cmp_b1, cmp_w2, tables, tables_c, tm):
    mod = _modulation(c, w_mod_all, layer, b_mod)
    (qa, ka, za, qb, kb, vb, zb, qc, kcr, vcr, gc, ks2, kw2, zc,
     qd, kd, zd, vat, vdt, vst, vwt) = _projection(x, mod, norm_pre, w_in_all, layer, tables, tm)
    oa = _stick_breaking(qa, ka, vat, za)
    ob = _dilated(qb, kb, vb, zb)
    kc2, vct = _compress(kcr, vcr, cmp_pos, cmp_w1, cmp_b1, cmp_w2, tables_c)
    oc = _nsa(qc, kc2, vct, ks2, vst, kw2, vwt, gc, zc)
    od = _moba(qd, kd, vdt, zd)
    return _out_projection((oa, ob, oc, od), w_out, x, mod, norm_post, min(2 * tm, x.shape[1]))


def kernel(x, c, norm_pre, norm_post, w_mod, b_mod, w_in, w_out, cmp_pos, cmp_w1, cmp_b1, cmp_w2):
    S = x.shape[1]
    tables = _rope_lane_tables(np.arange(S))
    tables_c = _rope_lane_tables(np.arange(S // CMP_STRIDE) * CMP_STRIDE + CMP_LEN - 1)
    tm = min(512, S)
    for l in range(norm_pre.shape[0]):
        x = _layer(x, c, l, norm_pre[l], norm_post[l], w_mod, b_mod[l], w_in, w_out[l],
                   cmp_pos[l], cmp_w1[l], cmp_b1[l], cmp_w2[l], tables, tables_c, tm)
    return x
```
